```python
import math
import jax, jax.numpy as jnp
from jax import lax
import numpy as np

D_MODEL = 2048
BATCH = 1
SEQ = 8192
DEPTH = 1

MIX_WIDTH = D_MODEL
RWKV_WIDTH = MIX_WIDTH // 2
RWKV_HEAD_DIM = 64
RWKV_HEADS = RWKV_WIDTH // RWKV_HEAD_DIM
DECAY_LORA = max(32, int(round(1.8 * RWKV_WIDTH ** 0.5 / 32)) * 32)
AAA_LORA = max(32, int(round(1.8 * RWKV_WIDTH ** 0.5 / 32)) * 32)
GATE_LORA = max(32, int(round(0.6 * RWKV_WIDTH ** 0.8 / 32)) * 32)
RWKV_COLS = 3 * RWKV_WIDTH + DECAY_LORA + AAA_LORA + GATE_LORA
DIFF_WIDTH = MIX_WIDTH - RWKV_WIDTH
DIFF_HEAD_DIM = 64
DIFF_V_DIM = 2 * DIFF_HEAD_DIM
DIFF_HEADS = DIFF_WIDTH // DIFF_V_DIM
DIFF_COLS = 3 * DIFF_WIDTH
IN_COLS = RWKV_COLS + DIFF_COLS
D_FF = 4 * D_MODEL
ROPE_THETA = 10000.0
Q_BLOCK = 128
LN_EPS = 1e-5
RWKV_GN_EPS = 64e-5
SUBLN_EPS = 1e-5
DEEPNORM_ALPHA = (2 * DEPTH) ** 0.25
DEEPNORM_BETA = (8 * DEPTH) ** -0.25

kernel_name = "hymba_rwkv7_diffattn_deepnorm"


def _layer_norm(x, gain, bias):
    xf = x.astype(jnp.float32)
    mean = xf.mean(-1, keepdims=True)
    var = jnp.square(xf - mean).mean(-1, keepdims=True)
    return ((xf - mean) * lax.rsqrt(var + LN_EPS) * gain + bias).astype(x.dtype)


def _token_shift(p):
    return jnp.pad(p, ((0, 0), (1, 0), (0, 0)))[:, :-1]


def _rwkv7_mixer(proj, mu, w0, w_up, a0, a_up, g_up, k_k, k_a, r_k, gn_gain, gn_bias):
    B, S, _ = proj.shape
    H, N = RWKV_HEADS, RWKV_HEAD_DIM
    p = proj.astype(jnp.float32)
    mixed = p + (_token_shift(p) - p) * mu
    splits = [RWKV_WIDTH, 2 * RWKV_WIDTH, 3 * RWKV_WIDTH,
              3 * RWKV_WIDTH + DECAY_LORA, 3 * RWKV_WIDTH + DECAY_LORA + AAA_LORA]
    r, k, v, w_d, a_d, g_d = jnp.split(mixed, splits, axis=-1)
    w = -jax.nn.softplus(-(w0 + jnp.tanh(w_d) @ w_up)) - 0.5
    decay = jnp.exp(-jnp.exp(w))
    a = jax.nn.sigmoid(a0 + a_d @ a_up)
    g = jax.nn.sigmoid(g_d) @ g_up
    heads = lambda t: t.reshape(B, S, H, N)
    kk = heads(k * k_k)
    kk = kk / jnp.maximum(jnp.sqrt(jnp.sum(jnp.square(kk), -1, keepdims=True)), 1e-12)
    k = k * (1.0 + (a - 1.0) * k_a)
    r_h, k_h, v_h, w_h, a_h = heads(r), heads(k), heads(v), heads(decay), heads(a)

    def step(state, inp):
        r_t, w_t, k_t, v_t, kk_t, a_t = inp
        sa = jnp.einsum('bhij,bhj->bhi', state, -kk_t)
        state = (state * w_t[..., None, :]
                 + sa[..., :, None] * (kk_t * a_t)[..., None, :]
                 + v_t[..., :, None] * k_t[..., None, :])
        return state, jnp.einsum('bhij,bhj->bhi', state, r_t)

    xs = tuple(jnp.moveaxis(t, 1, 0) for t in (r_h, w_h, k_h, v_h, kk, a_h))
    _, y = lax.scan(step, jnp.zeros((B, H, N, N), jnp.float32), xs)
    y = jnp.moveaxis(y, 0, 1)
    mean = y.mean(-1, keepdims=True)
    var = jnp.square(y - mean).mean(-1, keepdims=True)
    y = ((y - mean) * lax.rsqrt(var + RWKV_GN_EPS)).reshape(B, S, RWKV_WIDTH) * gn_gain + gn_bias
    bonus = (jnp.sum(r_h * k_h * r_k, -1, keepdims=True) * v_h).reshape(B, S, RWKV_WIDTH)
    return ((y + bonus) * g).astype(proj.dtype)


def _rotary_tables(positions):
    inv_freq = ROPE_THETA ** (-jnp.arange(0, DIFF_HEAD_DIM, 2, dtype=jnp.float32) / DIFF_HEAD_DIM)
    ang = positions.astype(jnp.float32)[..., None] * inv_freq
    return jnp.cos(ang)[:, :, None, None, :], jnp.sin(ang)[:, :, None, None, :]


def _apply_rotary(t, cos, sin):
    t1, t2 = jnp.split(t.astype(jnp.float32), 2, axis=-1)
    return jnp.concatenate([t1 * cos - t2 * sin, t2 * cos + t1 * sin], -1)


def _diff_attention(proj, cos, sin, lq1, lk1, lq2, lk2, subln_gain, lambda_init):
    B, S, _ = proj.shape
    H, d = DIFF_HEADS, DIFF_HEAD_DIM
    q, k, v = jnp.split(proj, [DIFF_WIDTH, 2 * DIFF_WIDTH], axis=-1)
    q = _apply_rotary(q.reshape(B, S, H, 2, d), cos, sin).transpose(0, 2, 3, 1, 4)
    k = _apply_rotary(k.reshape(B, S, H, 2, d), cos, sin).transpose(0, 2, 3, 1, 4)
    v = v.reshape(B, S, H, DIFF_V_DIM).transpose(0, 2, 1, 3).astype(jnp.float32)
    lam = (jnp.exp(jnp.sum(lq1.astype(jnp.float32) * lk1))
           - jnp.exp(jnp.sum(lq2.astype(jnp.float32) * lk2)) + lambda_init)
    scale = d ** -0.5
    key_pos = jnp.arange(S)

    def query_block(i):
        start = i * Q_BLOCK
        qb = lax.dynamic_slice_in_dim(q, start, Q_BLOCK, axis=3)
        s = jnp.einsum('bhcqd,bhckd->bhcqk', qb, k) * scale
        q_pos = start + jnp.arange(Q_BLOCK)
        s = jnp.where(key_pos[None, :] <= q_pos[:, None], s, -1e30)
        p = jax.nn.softmax(s, axis=-1)
        attn = p[:, :, 0] - lam * p[:, :, 1]
        return jnp.einsum('bhqk,bhkd->bhqd', attn, v)

    out = lax.map(query_block, jnp.arange(S // Q_BLOCK))
    out = out.transpose(1, 0, 3, 2, 4).reshape(B, S, H, DIFF_V_DIM)
    out = out * lax.rsqrt(jnp.square(out).mean(-1, keepdims=True) + SUBLN_EPS) * subln_gain
    out = out * (1.0 - lambda_init)
    return out.reshape(B, S, DIFF_WIDTH).astype(proj.dtype)


def setup_inputs(seed: int = 0) -> dict:
    key = jax.random.key(seed)
    ks = jax.random.split(key, 26)
    L = DEPTH

    def nrm(k, shape, scale):
        return jax.random.normal(k, shape, jnp.float32) * scale

    col_scale = np.ones((IN_COLS,), np.float32)
    col_scale[2 * RWKV_WIDTH:3 * RWKV_WIDTH] = DEEPNORM_BETA
    col_scale[RWKV_COLS + 2 * DIFF_WIDTH:] = DEEPNORM_BETA
    x = nrm(ks[0], (BATCH, SEQ, D_MODEL), 1.0)
    positions = jnp.tile(jnp.arange(SEQ, dtype=jnp.int32)[None, :], (BATCH, 1))
    w_in = nrm(ks[1], (L, D_MODEL, IN_COLS), D_MODEL ** -0.5) * jnp.asarray(col_scale)
    mu_shift = jax.random.uniform(ks[2], (L, RWKV_COLS), jnp.float32)
    rwkv_w0 = jax.random.uniform(ks[3], (L, RWKV_WIDTH), jnp.float32, -6.5, -1.5)
    rwkv_w_up = nrm(ks[4], (L, DECAY_LORA, RWKV_WIDTH), 0.5 * DECAY_LORA ** -0.5)
    rwkv_a0 = nrm(ks[5], (L, RWKV_WIDTH), 0.5)
    rwkv_a_up = nrm(ks[6], (L, AAA_LORA, RWKV_WIDTH), AAA_LORA ** -0.5)
    rwkv_g_up = nrm(ks[7], (L, GATE_LORA, RWKV_WIDTH), GATE_LORA ** -0.5)
    rwkv_k_k = 0.85 + nrm(ks[8], (L, RWKV_WIDTH), 0.05)
    rwkv_k_a = 1.0 + nrm(ks[9], (L, RWKV_WIDTH), 0.05)
    rwkv_r_k = nrm(ks[10], (L, RWKV_HEADS, RWKV_HEAD_DIM), 0.1)
    rwkv_gn_gain = 1.0 + nrm(ks[11], (L, RWKV_WIDTH), 0.05)
    rwkv_gn_bias = nrm(ks[12], (L, RWKV_WIDTH), 0.02)
    diff_lambda_q1 = nrm(ks[13], (L, DIFF_HEAD_DIM), 0.1)
    diff_lambda_k1 = nrm(ks[14], (L, DIFF_HEAD_DIM), 0.1)
    diff_lambda_q2 = nrm(ks[15], (L, DIFF_HEAD_DIM), 0.1)
    diff_lambda_k2 = nrm(ks[16], (L, DIFF_HEAD_DIM), 0.1)
    diff_subln_gain = 1.0 + nrm(ks[17], (L, DIFF_V_DIM), 0.05)
    w_out = nrm(ks[18], (L, MIX_WIDTH, D_MODEL), MIX_WIDTH ** -0.5 * DEEPNORM_BETA)
    ln1_gain = 1.0 + nrm(ks[19], (L, D_MODEL), 0.05)
    ln1_bias = nrm(ks[20], (L, D_MODEL), 0.02)
    w_ff1 = nrm(ks[21], (L, D_MODEL, D_FF), D_MODEL ** -0.5 * DEEPNORM_BETA)
    w_ff2 = nrm(ks[22], (L, D_FF, D_MODEL), D_FF ** -0.5 * DEEPNORM_BETA)
    ln2_gain = 1.0 + nrm(ks[23], (L, D_MODEL), 0.05)
    ln2_bias = nrm(ks[24], (L, D_MODEL), 0.02)
    return {
        'x': x, 'positions': positions, 'w_in': w_in, 'mu_shift': mu_shift,
        'rwkv_w0': rwkv_w0, 'rwkv_w_up': rwkv_w_up, 'rwkv_a0': rwkv_a0, 'rwkv_a_up': rwkv_a_up,
        'rwkv_g_up': rwkv_g_up, 'rwkv_k_k': rwkv_k_k, 'rwkv_k_a': rwkv_k_a, 'rwkv_r_k': rwkv_r_k,
        'rwkv_gn_gain': rwkv_gn_gain, 'rwkv_gn_bias': rwkv_gn_bias,
        'diff_lambda_q1': diff_lambda_q1, 'diff_lambda_k1': diff_lambda_k1,
        'diff_lambda_q2': diff_lambda_q2, 'diff_lambda_k2': diff_lambda_k2,
        'diff_subln_gain': diff_subln_gain, 'w_out': w_out,
        'ln1_gain': ln1_gain, 'ln1_bias': ln1_bias, 'w_ff1': w_ff1, 'w_ff2': w_ff2,
        'ln2_gain': ln2_gain, 'ln2_bias': ln2_bias,
    }


def reference(x, positions, w_in, mu_shift, rwkv_w0, rwkv_w_up, rwkv_a0, rwkv_a_up,
              rwkv_g_up, rwkv_k_k, rwkv_k_a, rwkv_r_k, rwkv_gn_gain, rwkv_gn_bias,
              diff_lambda_q1, diff_lambda_k1, diff_lambda_q2, diff_lambda_k2,
              diff_subln_gain, w_out, ln1_gain, ln1_bias, w_ff1, w_ff2, ln2_gain, ln2_bias):
    cos, sin = _rotary_tables(positions)
    for l in range(DEPTH):
        lambda_init = 0.8 - 0.6 * math.exp(-0.3 * l)
        proj = x @ w_in[l]
        y_rwkv = _rwkv7_mixer(proj[..., :RWKV_COLS], mu_shift[l], rwkv_w0[l], rwkv_w_up[l],
                              rwkv_a0[l], rwkv_a_up[l], rwkv_g_up[l], rwkv_k_k[l], rwkv_k_a[l],
                              rwkv_r_k[l], rwkv_gn_gain[l], rwkv_gn_bias[l])
        y_diff = _diff_attention(proj[..., RWKV_COLS:], cos, sin, diff_lambda_q1[l],
                                 diff_lambda_k1[l], diff_lambda_q2[l], diff_lambda_k2[l],
                                 diff_subln_gain[l], lambda_init)
        mix = jnp.concatenate([y_rwkv, y_diff], axis=-1) @ w_out[l]
        x = _layer_norm(DEEPNORM_ALPHA * x + mix, ln1_gain[l], ln1_bias[l])
        h = jnp.square(jax.nn.relu(x @ w_ff1[l])) @ w_ff2[l]
        x = _layer_norm(DEEPNORM_ALPHA * x + h, ln2_gain[l], ln2_bias[l])
    return x
```

```python
import functools
import math

import jax
import jax.numpy as jnp
from jax import lax
from jax.experimental import pallas as pl
from jax.experimental.pallas import tpu as pltpu

F32 = jnp.float32
BF16 = jnp.bfloat16

D_MODEL = 2048
RWKV_WIDTH = 1024
HEAD_DIM = 64
DECAY_LORA = 64
AAA_LORA = 64
GATE_LORA = 160
DIFF_WIDTH = 1024
DIFF_V_DIM = 128
DIFF_HEADS = 8
D_FF = 8192
ROPE_THETA = 10000.0
LN_EPS = 1e-5
RWKV_GN_EPS = 64e-5
SUBLN_EPS = 1e-5
DEEPNORM_ALPHA = 2.0 ** 0.25
LAMBDA_INIT = 0.8 - 0.6 * math.exp(0.0)

LANES = 128
SUBLANES = 8
VMEM_LIMIT_BYTES = 56 * 1024 * 1024

DECAY_PAD = 128
AAA_PAD = 128
GATE_PAD = 256
RWKV_PCOLS = 3 * RWKV_WIDTH + DECAY_PAD + AAA_PAD + GATE_PAD
CHUNK = 64
GROUP_HEADS = 4
GROUP_W = GROUP_HEADS * HEAD_DIM


def _cparams(n_axes):
    return pltpu.CompilerParams(
        dimension_semantics=("arbitrary",) * n_axes,
        vmem_limit_bytes=VMEM_LIMIT_BYTES,
    )


def _dot(a, b):
    return jnp.dot(a.astype(BF16), b.astype(BF16), preferred_element_type=F32)


def _dot_nt(a, b):
    return lax.dot_general(a.astype(BF16), b.astype(BF16), (((1,), (1,)), ((), ())),
                           preferred_element_type=F32)


def _dot_tn(a, b):
    return lax.dot_general(a.astype(BF16), b.astype(BF16), (((0,), (0,)), ((), ())),
                           preferred_element_type=F32)


def _split3(x):
    h1 = x.astype(BF16)
    r1 = x - h1.astype(F32)
    h2 = r1.astype(BF16)
    h3 = (r1 - h2.astype(F32)).astype(BF16)
    return h1, h2, h3


def _split2(x):
    h1 = x.astype(BF16)
    h2 = (x - h1.astype(F32)).astype(BF16)
    return h1, h2


def _mm_kernel(a_ref, b_ref, o_ref, *, act):
    acc = jnp.dot(a_ref[...], b_ref[...], preferred_element_type=F32)
    if act == "relu2":
        acc = jnp.square(jnp.maximum(acc, 0.0))
    o_ref[...] = acc.astype(o_ref.dtype)


def _matmul(a, b, *, bm, bn, out_dtype, act=None, name):
    m, k = a.shape
    _, n = b.shape
    assert m % bm == 0 and n % bn == 0
    return pl.pallas_call(
        functools.partial(_mm_kernel, act=act),
        grid=(m // bm, n // bn),
        in_specs=[pl.BlockSpec((bm, k), lambda i, j: (i, 0)),
                  pl.BlockSpec((k, bn), lambda i, j: (0, j))],
        out_specs=pl.BlockSpec((bm, bn), lambda i, j: (i, j)),
        out_shape=jax.ShapeDtypeStruct((m, n), out_dtype),
        compiler_params=_cparams(2),
        name=name,
    )(a, b)


def _rwkv_prep_kernel(p_ref, mu_ref, w0_ref, wup_ref, a0_ref, aup_ref, gup_ref, kk_ref, ka_ref,
                      rk_ref, seg_ref, tri_ref,
                      at_ref, bt_ref, kt_ref, rt_ref, v_ref, wc_ref, bonus_ref, g_ref,
                      carry_ref, *, tb):
    i = pl.program_id(0)

    @pl.when(i == 0)
    def _():
        carry_ref[...] = jnp.zeros_like(carry_ref)

    def mixed(lo, hi):
        p = p_ref[:, lo:hi]
        prev_last = carry_ref[0:1, lo:hi]
        shifted = pltpu.roll(p, 1, axis=0)
        row = lax.broadcasted_iota(jnp.int32, p.shape, 0)
        prev = jnp.where(row == 0, prev_last, shifted)
        carry_ref[0:1, lo:hi] = p[tb - 1:tb, :]
        return p + (prev - p) * mu_ref[:, lo:hi]

    w3 = 3 * RWKV_WIDTH
    r = mixed(0, RWKV_WIDTH)
    k = mixed(RWKV_WIDTH, 2 * RWKV_WIDTH)
    v = mixed(2 * RWKV_WIDTH, w3)
    w_d = mixed(w3, w3 + DECAY_PAD)
    a_d = mixed(w3 + DECAY_PAD, w3 + DECAY_PAD + AAA_PAD)
    g_d = mixed(w3 + DECAY_PAD + AAA_PAD, RWKV_PCOLS)

    z = w0_ref[...] + _dot(jnp.tanh(w_d), wup_ref[...])
    nz = -z
    softplus = jnp.maximum(nz, 0.0) + jnp.log1p(jnp.exp(-jnp.abs(nz)))
    lw = -jnp.exp(-softplus - 0.5)
    a = jax.nn.sigmoid(a0_ref[...] + _dot(a_d, aup_ref[...]))
    g_ref[...] = _dot(jax.nn.sigmoid(g_d), gup_ref[...])

    seg = seg_ref[...]

    def segsum(x):
        h1, h2 = _split2(x)
        return (jnp.dot(h1, seg, preferred_element_type=F32)
                + jnp.dot(h2, seg, preferred_element_type=F32))

    kk0 = k * kk_ref[...]
    kk = kk0 / jnp.maximum(jnp.sqrt(segsum(kk0 * kk0)), 1e-12)
    k2 = k * (1.0 + (a - 1.0) * ka_ref[...])
    bonus_ref[...] = segsum(r * k2 * rk_ref[...]) * v

    tri = tri_ref[...]
    l1, l2, l3 = _split3(lw)
    cs = (jnp.dot(tri, l1, preferred_element_type=F32)
          + jnp.dot(tri, l2, preferred_element_type=F32)
          + jnp.dot(tri, l3, preferred_element_type=F32))
    cl = cs[:tb]
    tot = cs[tb:]
    e_neg = jnp.exp(-cl)
    at_ref[...] = (-kk * jnp.exp(cl - lw)).astype(BF16)
    bt_ref[...] = (kk * a * e_neg).astype(BF16)
    kt_ref[...] = (k2 * e_neg).astype(BF16)
    rt_ref[...] = (r * jnp.exp(cl)).astype(BF16)
    v_ref[...] = v.astype(BF16)
    for c in range(tb // CHUNK):
        wc_ref[c] = jnp.exp(tot[c * CHUNK:c * CHUNK + 1, :])


def _rwkv_prep(proj_r, mu, w0, wup, a0, aup, gup, k_k, k_a, r_k, *, tb):
    s = proj_r.shape[0]
    assert s % tb == 0 and tb % CHUNK == 0
    w = RWKV_WIDTH
    lane = jnp.arange(w)
    seg = (lane[:, None] // HEAD_DIM == lane[None, :] // HEAD_DIM).astype(BF16)
    t = jnp.arange(tb)
    same = t[:, None] // CHUNK == t[None, :] // CHUNK
    tri = jnp.concatenate([(same & (t[:, None] >= t[None, :])), same], axis=0).astype(BF16)
    row = lambda x: x.reshape(1, -1).astype(F32)
    const = lambda shape: pl.BlockSpec(shape, lambda i: (0,) * len(shape))
    tile = lambda width: pl.BlockSpec((tb, width), lambda i: (i, 0))
    outs = pl.pallas_call(
        functools.partial(_rwkv_prep_kernel, tb=tb),
        grid=(s // tb,),
        in_specs=[tile(RWKV_PCOLS), const((1, RWKV_PCOLS)), const((1, w)), const((DECAY_PAD, w)),
                  const((1, w)), const((AAA_PAD, w)), const((GATE_PAD, w)), const((1, w)),
                  const((1, w)), const((1, w)), const((w, w)), const((2 * tb, tb))],
        out_specs=[tile(w), tile(w), tile(w), tile(w), tile(w),
                   pl.BlockSpec((tb // CHUNK, 1, w), lambda i: (i, 0, 0)),
                   tile(w), tile(w)],
        out_shape=[jax.ShapeDtypeStruct((s, w), BF16)] * 5
        + [jax.ShapeDtypeStruct((s // CHUNK, 1, w), F32),
           jax.ShapeDtypeStruct((s, w), F32), jax.ShapeDtypeStruct((s, w), F32)],
        scratch_shapes=[pltpu.VMEM((SUBLANES, RWKV_PCOLS), F32)],
        compiler_params=_cparams(1),
        name="rwkv_prep",
    )(proj_r, row(mu), row(w0), wup, row(a0), aup, gup, row(k_k), row(k_a), row(r_k), seg, tri)
    return outs


def _expand(x, lane_head):
    zero = jnp.zeros_like(x)
    return jnp.concatenate([jnp.where(lane_head == h, x, zero) for h in range(GROUP_HEADS)], axis=0)


def _rwkv_chunk_kernel(at_ref, bt_ref, kt_ref, rt_ref, v_ref, wc_ref, bonus_ref, g_ref,
                       gain_ref, bias_ref, o_ref, state_ref, *, nc):
    @pl.when(pl.program_id(1) == 0)
    def _():
        state_ref[...] = jnp.zeros_like(state_ref)

    n = GROUP_W
    row = lax.broadcasted_iota(jnp.int32, (n, n), 0)
    col = lax.broadcasted_iota(jnp.int32, (n, n), 1)
    t_row = row & (CHUNK - 1)
    t_col = col & (CHUNK - 1)
    strict = t_row > t_col
    incl = t_row >= t_col
    eye = row == col
    same_head = (row >> 6) == (col >> 6)
    lane_head = lax.broadcasted_iota(jnp.int32, (CHUNK, n), 1) >> 6
    eye_f = eye.astype(F32)

    def blk(bits):
        return (row >> bits) == (col >> bits)

    pre = []
    for c in range(nc):
        sl = slice(c * CHUNK, (c + 1) * CHUNK)
        wc = wc_ref[c]
        bt = bt_ref[sl, :].astype(F32)
        kt = kt_ref[sl, :].astype(F32)
        ae = _expand(at_ref[sl, :].astype(F32), lane_head)
        be = _expand(bt, lane_head)
        ke = _expand(kt, lane_head)
        re = _expand(rt_ref[sl, :].astype(F32), lane_head)
        ve = _expand(v_ref[sl, :].astype(F32), lane_head)
        bhe = _expand(bt * wc, lane_head)
        khe = _expand(kt * wc, lane_head)

        lab = jnp.where(strict, _dot_nt(ae, be), 0.0)
        lak = jnp.where(strict, _dot_nt(ae, ke), 0.0)
        arb = jnp.where(incl, _dot_nt(re, be), 0.0)
        ark = jnp.where(incl, _dot_nt(re, ke), 0.0)

        l8 = jnp.where(blk(3), lab, 0.0)
        minv = eye_f + l8
        pw = l8
        for _ in range(2):
            pw = _dot(pw, pw)
            minv = minv + _dot(pw, minv)
        for bits in (3, 4, 5):
            off = jnp.where(blk(bits + 1) & jnp.logical_not(blk(bits)), lab, 0.0)
            minv = minv + _dot(_dot(minv, off), minv)

        lv = _dot(lak, ve)
        p = _dot(minv, ae)
        q = _dot(minv, lv)
        g_mat = re + _dot(arb, p)
        y0 = _dot(arb, q) + _dot(ark, ve)
        phi = jnp.where(eye, jnp.broadcast_to(wc, (n, n)), 0.0) + _dot_tn(bhe, p)
        psi = _dot_tn(bhe, q) + _dot_tn(khe, ve)
        pre.append((g_mat, y0, phi, psi))

    state = state_ref[...]
    for c in range(nc):
        g_mat, y0, phi, psi = pre[c]
        y = _dot(g_mat, state) + y0
        state = _dot(phi, state) + psi
        mean = jnp.sum(y, axis=1, keepdims=True) * (1.0 / HEAD_DIM)
        yc = jnp.where(same_head, y - mean, 0.0)
        var = jnp.sum(yc * yc, axis=1, keepdims=True) * (1.0 / HEAD_DIM)
        yn = yc * lax.rsqrt(var + RWKV_GN_EPS)
        yw = yn[0:CHUNK] + yn[CHUNK:2 * CHUNK] + yn[2 * CHUNK:3 * CHUNK] + yn[3 * CHUNK:4 * CHUNK]
        sl = slice(c * CHUNK, (c + 1) * CHUNK)
        out = (yw * gain_ref[...] + bias_ref[...] + bonus_ref[sl, :]) * g_ref[sl, :]
        o_ref[sl, :] = out.astype(o_ref.dtype)
    state_ref[...] = state


def _rwkv_chunks(at, bt, kt, rt, v, wc, bonus, g, gn_gain, gn_bias, *, nc):
    s, w = at.shape
    tb = nc * CHUNK
    assert s % tb == 0 and w % GROUP_W == 0
    tile = pl.BlockSpec((tb, GROUP_W), lambda gi, ti: (ti, gi))
    rowspec = pl.BlockSpec((1, GROUP_W), lambda gi, ti: (0, gi))
    return pl.pallas_call(
        functools.partial(_rwkv_chunk_kernel, nc=nc),
        grid=(w // GROUP_W, s // tb),
        in_specs=[tile, tile, tile, tile, tile,
                  pl.BlockSpec((nc, 1, GROUP_W), lambda gi, ti: (ti, 0, gi)),
                  tile, tile, rowspec, rowspec],
        out_specs=tile,
        out_shape=jax.ShapeDtypeStruct((s, w), BF16),
        scratch_shapes=[pltpu.VMEM((GROUP_W, GROUP_W), F32)],
        compiler_params=_cparams(2),
        name="rwkv_chunks",
    )(at, bt, kt, rt, v, wc, bonus, g, gn_gain.reshape(1, -1).astype(F32),
      gn_bias.reshape(1, -1).astype(F32))


def _rope_kernel(pos_ref, invf_ref, p_ref, q_ref, k_ref, v_ref, *, tb):
    ang = pos_ref[...].astype(F32) * invf_ref[...]
    lane = lax.broadcasted_iota(jnp.int32, (tb, LANES), 1)
    first_half = (lane & (HEAD_DIM - 1)) < (HEAD_DIM // 2)
    cos = jnp.cos(ang)
    sin = jnp.sin(ang)
    sin = jnp.where(first_half, -sin, sin)

    def rope(t):
        partner = jnp.where(first_half, pltpu.roll(t, LANES - HEAD_DIM // 2, axis=1),
                            pltpu.roll(t, HEAD_DIM // 2, axis=1))
        return t * cos + partner * sin

    scale = HEAD_DIM ** -0.5
    for j in range(DIFF_WIDTH // LANES):
        sl = slice(j * LANES, (j + 1) * LANES)
        q_ref[:, sl] = (rope(p_ref[:, sl]) * scale).astype(BF16)
        ks = slice(DIFF_WIDTH + j * LANES, DIFF_WIDTH + (j + 1) * LANES)
        k_ref[:, sl] = rope(p_ref[:, ks]).astype(BF16)
    v_ref[...] = p_ref[:, 2 * DIFF_WIDTH:].astype(BF16)


def _rope(proj_d, positions, *, tb):
    s = proj_d.shape[0]
    half = jnp.arange(0, HEAD_DIM, 2, dtype=F32)
    inv_freq = ROPE_THETA ** (-half / HEAD_DIM)
    invf = jnp.tile(inv_freq, 2 * LANES // HEAD_DIM).reshape(1, LANES)
    tile = pl.BlockSpec((tb, DIFF_WIDTH), lambda i: (i, 0))
    return pl.pallas_call(
        functools.partial(_rope_kernel, tb=tb),
        grid=(s // tb,),
        in_specs=[pl.BlockSpec((tb, 1), lambda i: (i, 0)),
                  pl.BlockSpec((1, LANES), lambda i: (0, 0)),
                  pl.BlockSpec((tb, 3 * DIFF_WIDTH), lambda i: (i, 0))],
        out_specs=[tile, tile, tile],
        out_shape=[jax.ShapeDtypeStruct((s, DIFF_WIDTH), BF16)] * 3,
        compiler_params=_cparams(1),
        name="rope",
    )(positions.reshape(s, 1), invf, proj_d)


def _attn_kernel(lam_ref, gain_ref, q_ref, k_ref, v_ref, o_ref, *, bq):
    i = pl.program_id(1)
    q = q_ref[...]
    lane = lax.broadcasted_iota(jnp.int32, q.shape, 1)
    zero = jnp.zeros_like(q)
    qm = jnp.concatenate([jnp.where(lane < HEAD_DIM, q, zero),
                          jnp.where(lane >= HEAD_DIM, q, zero)], axis=0)

    def step(j, carry, diagonal):
        m, l, acc = carry
        start = pl.multiple_of(j * bq, bq)
        kb = k_ref[pl.ds(start, bq), :]
        vb = v_ref[pl.ds(start, bq), :]
        s = lax.dot_general(qm, kb, (((1,), (1,)), ((), ())), preferred_element_type=F32)
        if diagonal:
            r_in = lax.broadcasted_iota(jnp.int32, s.shape, 0) & (bq - 1)
            c_in = lax.broadcasted_iota(jnp.int32, s.shape, 1)
            s = jnp.where(c_in <= r_in, s, -1e30)
        m_new = jnp.maximum(m, jnp.max(s, axis=1, keepdims=True))
        p = jnp.exp(s - m_new)
        alpha = jnp.exp(m - m_new)
        l = alpha * l + jnp.sum(p, axis=1, keepdims=True)
        acc = alpha * acc + jnp.dot(p.astype(BF16), vb, preferred_element_type=F32)
        return m_new, l, acc

    init = (jnp.full((2 * bq, 1), -1e30, F32), jnp.zeros((2 * bq, 1), F32),
            jnp.zeros((2 * bq, DIFF_V_DIM), F32))
    carry = lax.fori_loop(0, i, lambda j, c: step(j, c, False), init)
    m, l, acc = step(i, carry, True)
    o = acc / l
    lam_v = lam_ref[...]
    lam = (jnp.exp(jnp.sum(lam_v[0:1] * lam_v[1:2], axis=1, keepdims=True))
           - jnp.exp(jnp.sum(lam_v[2:3] * lam_v[3:4], axis=1, keepdims=True)) + LAMBDA_INIT)
    out = o[:bq] - lam * o[bq:]
    out = out * lax.rsqrt(jnp.mean(jnp.square(out), axis=1, keepdims=True) + SUBLN_EPS)
    out = out * gain_ref[...] * (1.0 - LAMBDA_INIT)
    o_ref[...] = out.astype(o_ref.dtype)


def _attention(q, k, v, lam_vecs, subln_gain, *, bq):
    s = q.shape[0]
    kv_spec = pl.BlockSpec((s, DIFF_V_DIM), lambda h, i: (0, h))
    return pl.pallas_call(
        functools.partial(_attn_kernel, bq=bq),
        grid=(DIFF_HEADS, s // bq),
        in_specs=[pl.BlockSpec((4, HEAD_DIM), lambda h, i: (0, 0)),
                  pl.BlockSpec((1, DIFF_V_DIM), lambda h, i: (0, 0)),
                  pl.BlockSpec((bq, DIFF_V_DIM), lambda h, i: (i, h)),
                  kv_spec, kv_spec],
        out_specs=pl.BlockSpec((bq, DIFF_V_DIM), lambda h, i: (i, h)),
        out_shape=jax.ShapeDtypeStruct((s, DIFF_WIDTH), BF16),
        compiler_params=_cparams(2),
        name="diff_attn",
    )(lam_vecs, subln_gain.reshape(1, -1).astype(F32), q, k, v)


def _layer_norm(z, gain, bias):
    mean = jnp.mean(z, axis=-1, keepdims=True)
    zc = z - mean
    var = jnp.mean(zc * zc, axis=-1, keepdims=True)
    return zc * lax.rsqrt(var + LN_EPS) * gain + bias


def _outproj_kernel(yr_ref, yd_ref, wr_ref, wd_ref, x_ref, gain_ref, bias_ref, o32_ref, o16_ref):
    mix = (jnp.dot(yr_ref[...], wr_ref[...], preferred_element_type=F32)
           + jnp.dot(yd_ref[...], wd_ref[...], preferred_element_type=F32))
    y = _layer_norm(DEEPNORM_ALPHA * x_ref[...] + mix, gain_ref[...], bias_ref[...])
    o32_ref[...] = y
    o16_ref[...] = y.astype(BF16)


def _outproj(y_rwkv, y_diff, w_r, w_d, x, gain, bias, *, bm):
    s = x.shape[0]
    const = lambda shape: pl.BlockSpec(shape, lambda i: (0, 0))
    return pl.pallas_call(
        _outproj_kernel,
        grid=(s // bm,),
        in_specs=[pl.BlockSpec((bm, RWKV_WIDTH), lambda i: (i, 0)),
                  pl.BlockSpec((bm, DIFF_WIDTH), lambda i: (i, 0)),
                  const((RWKV_WIDTH, D_MODEL)), const((DIFF_WIDTH, D_MODEL)),
                  pl.BlockSpec((bm, D_MODEL), lambda i: (i, 0)),
                  const((1, D_MODEL)), const((1, D_MODEL))],
        out_specs=[pl.BlockSpec((bm, D_MODEL), lambda i: (i, 0))] * 2,
        out_shape=[jax.ShapeDtypeStruct((s, D_MODEL), F32), jax.ShapeDtypeStruct((s, D_MODEL), BF16)],
        compiler_params=_cparams(1),
        name="outproj_ln",
    )(y_rwkv, y_diff, w_r, w_d, x, gain.reshape(1, -1), bias.reshape(1, -1))


def _ff2_kernel(h_ref, w_ref, x_ref, gain_ref, bias_ref, o_ref, acc_ref):
    kk = pl.program_id(1)

    @pl.when(kk == 0)
    def _():
        acc_ref[...] = jnp.zeros_like(acc_ref)

    acc_ref[...] += jnp.dot(h_ref[...], w_ref[...], preferred_element_type=F32)

    @pl.when(kk == pl.num_programs(1) - 1)
    def _():
        o_ref[...] = _layer_norm(DEEPNORM_ALPHA * x_ref[...] + acc_ref[...],
                                 gain_ref[...], bias_ref[...])


def _ff2(h, w, x, gain, bias, *, bm, bk):
    s, kdim = h.shape
    const = lambda shape: pl.BlockSpec(shape, lambda i, k: (0, 0))
    return pl.pallas_call(
        _ff2_kernel,
        grid=(s // bm, kdim // bk),
        in_specs=[pl.BlockSpec((bm, bk), lambda i, k: (i, k)),
                  pl.BlockSpec((bk, D_MODEL), lambda i, k: (k, 0)),
                  pl.BlockSpec((bm, D_MODEL), lambda i, k: (i, 0)),
                  const((1, D_MODEL)), const((1, D_MODEL))],
        out_specs=pl.BlockSpec((bm, D_MODEL), lambda i, k: (i, 0)),
        out_shape=jax.ShapeDtypeStruct((s, D_MODEL), F32),
        scratch_shapes=[pltpu.VMEM((bm, D_MODEL), F32)],
        compiler_params=_cparams(2),
        name="ff2_ln",
    )(h, w, x, gain.reshape(1, -1), bias.reshape(1, -1))


def _pad_rows(w, rows):
    return jnp.pad(w, ((0, rows - w.shape[0]), (0, 0)))


def _layer(x, positions, w_in, mu_shift, w0, w_up, a0, a_up, g_up, k_k, k_a, r_k, gn_gain, gn_bias,
           lq1, lk1, lq2, lk2, subln_gain, w_out, ln1_gain, ln1_bias, w_ff1, w_ff2, ln2_gain, ln2_bias,
           *, bm, prep_tb, nc, rope_tb, bq):
    w3 = 3 * RWKV_WIDTH
    rwkv_cols = w3 + DECAY_LORA + AAA_LORA + GATE_LORA

    def pad_cols(a, lo, hi, width):
        return jnp.pad(a[..., lo:hi], [(0, 0)] * (a.ndim - 1) + [(0, width - (hi - lo))])

    def rwkv_layout(a):
        return jnp.concatenate([
            a[..., :w3],
            pad_cols(a, w3, w3 + DECAY_LORA, DECAY_PAD),
            pad_cols(a, w3 + DECAY_LORA, w3 + DECAY_LORA + AAA_LORA, AAA_PAD),
            pad_cols(a, w3 + DECAY_LORA + AAA_LORA, rwkv_cols, GATE_PAD)], axis=-1)

    xb = x.astype(BF16)
    w_r = rwkv_layout(w_in[:, :rwkv_cols]).astype(BF16)
    w_d = w_in[:, rwkv_cols:].astype(BF16)
    proj_r = _matmul(xb, w_r, bm=bm, bn=512, out_dtype=F32, name="proj_rwkv")
    proj_d = _matmul(xb, w_d, bm=bm, bn=512, out_dtype=F32, name="proj_diff")

    at, bt, kt, rt, v_r, wc, bonus, gate = _rwkv_prep(
        proj_r, rwkv_layout(mu_shift), w0,
        _pad_rows(w_up, DECAY_PAD).astype(BF16), a0, _pad_rows(a_up, AAA_PAD).astype(BF16),
        _pad_rows(g_up, GATE_PAD).astype(BF16), k_k, k_a, r_k.reshape(-1), tb=prep_tb)
    y_rwkv = _rwkv_chunks(at, bt, kt, rt, v_r, wc, bonus, gate, gn_gain, gn_bias, nc=nc)

    q, k, v = _rope(proj_d, positions, tb=rope_tb)
    lam_vecs = jnp.stack([lq1, lk1, lq2, lk2]).astype(F32)
    y_diff = _attention(q, k, v, lam_vecs, subln_gain, bq=bq)

    wo = w_out.astype(BF16)
    x1, x1b = _outproj(y_rwkv, y_diff, wo[:RWKV_WIDTH], wo[RWKV_WIDTH:], x, ln1_gain, ln1_bias, bm=256)
    h = _matmul(x1b, w_ff1.astype(BF16), bm=bm, bn=512, out_dtype=BF16, act="relu2", name="ff1")
    return _ff2(h, w_ff2.astype(BF16), x1, ln2_gain, ln2_bias, bm=512, bk=1024)


def kernel(x, positions, w_in, mu_shift, rwkv_w0, rwkv_w_up, rwkv_a0, rwkv_a_up, rwkv_g_up, rwkv_k_k,
           rwkv_k_a, rwkv_r_k, rwkv_gn_gain, rwkv_gn_bias, diff_lambda_q1, diff_lambda_k1,
           diff_lambda_q2, diff_lambda_k2, diff_subln_gain, w_out, ln1_gain, ln1_bias, w_ff1, w_ff2,
           ln2_gain, ln2_bias):
    batch, seq, _ = x.shape
    assert batch == 1 and w_in.shape[0] == 1
    out = _layer(
        x[0], positions[0], w_in[0], mu_shift[0], rwkv_w0[0], rwkv_w_up[0], rwkv_a0[0], rwkv_a_up[0],
        rwkv_g_up[0], rwkv_k_k[0], rwkv_k_a[0], rwkv_r_k[0], rwkv_gn_gain[0], rwkv_gn_bias[0],
        diff_lambda_q1[0], diff_lambda_k1[0], diff_lambda_q2[0], diff_lambda_k2[0],
        diff_subln_gain[0], w_out[0], ln1_gain[0], ln1_bias[0], w_ff1[0], w_ff2[0], ln2_gain[0],
        ln2_bias[0], bm=1024, prep_tb=256, nc=2, rope_tb=512, bq=256)
    return out[None]
```

```python
import functools
import math

import jax
import jax.numpy as jnp
from jax import lax
from jax.experimental import pallas as pl
from jax.experimental.pallas import tpu as pltpu

F32 = jnp.float32
BF16 = jnp.bfloat16

D_MODEL = 2048
RWKV_WIDTH = 1024
HEAD_DIM = 64
DECAY_LORA = 64
AAA_LORA = 64
GATE_LORA = 160
DIFF_WIDTH = 1024
DIFF_V_DIM = 128
DIFF_HEADS = 8
D_FF = 8192
ROPE_THETA = 10000.0
LN_EPS = 1e-5
RWKV_GN_EPS = 64e-5
SUBLN_EPS = 1e-5
DEEPNORM_ALPHA = 2.0 ** 0.25
LAMBDA_INIT = 0.8 - 0.6 * math.exp(0.0)
LOG2_E = math.log2(math.e)

LANES = 128
SUBLANES = 8
VMEM_LIMIT_BYTES = 56 * 1024 * 1024

DECAY_PAD = 128
AAA_PAD = 128
GATE_PAD = 256
RWKV_PCOLS = 3 * RWKV_WIDTH + DECAY_PAD + AAA_PAD + GATE_PAD
CHUNK = 64
GROUP_HEADS = 4
GROUP_W = GROUP_HEADS * HEAD_DIM


def _cparams(n_axes):
    return pltpu.CompilerParams(
        dimension_semantics=("arbitrary",) * n_axes,
        vmem_limit_bytes=VMEM_LIMIT_BYTES,
    )


def _dot(a, b):
    return jnp.dot(a.astype(BF16), b.astype(BF16), preferred_element_type=F32)


def _dot_nt(a, b):
    return lax.dot_general(a.astype(BF16), b.astype(BF16), (((1,), (1,)), ((), ())),
                           preferred_element_type=F32)


def _dot_tn(a, b):
    return lax.dot_general(a.astype(BF16), b.astype(BF16), (((0,), (0,)), ((), ())),
                           preferred_element_type=F32)


def _split3(x):
    h1 = x.astype(BF16)
    r1 = x - h1.astype(F32)
    h2 = r1.astype(BF16)
    h3 = (r1 - h2.astype(F32)).astype(BF16)
    return h1, h2, h3


def _split2(x):
    h1 = x.astype(BF16)
    h2 = (x - h1.astype(F32)).astype(BF16)
    return h1, h2


def _mm_kernel(a_ref, b_ref, o_ref, *, act):
    acc = jnp.dot(a_ref[...], b_ref[...], preferred_element_type=F32)
    if act == "relu2":
        acc = jnp.square(jnp.maximum(acc, 0.0))
    o_ref[...] = acc.astype(o_ref.dtype)


def _matmul(a, b, *, bm, bn, out_dtype, act=None, name):
    m, k = a.shape
    _, n = b.shape
    assert m % bm == 0 and n % bn == 0
    return pl.pallas_call(
        functools.partial(_mm_kernel, act=act),
        grid=(m // bm, n // bn),
        in_specs=[pl.BlockSpec((bm, k), lambda i, j: (i, 0)),
                  pl.BlockSpec((k, bn), lambda i, j: (0, j))],
        out_specs=pl.BlockSpec((bm, bn), lambda i, j: (i, j)),
        out_shape=jax.ShapeDtypeStruct((m, n), out_dtype),
        compiler_params=_cparams(2),
        name=name,
    )(a, b)


def _mm_nt_kernel(a_ref, b_ref, o_ref):
    o_ref[...] = lax.dot_general(a_ref[...], b_ref[...], (((1,), (1,)), ((), ())),
                                 preferred_element_type=F32).astype(o_ref.dtype)


def _matmul_nt(a, b, *, bn, out_dtype, name):
    m, k = a.shape
    n, _ = b.shape
    assert n % bn == 0
    return pl.pallas_call(
        _mm_nt_kernel,
        grid=(n // bn,),
        in_specs=[pl.BlockSpec((m, k), lambda j: (0, 0)),
                  pl.BlockSpec((bn, k), lambda j: (j, 0))],
        out_specs=pl.BlockSpec((m, bn), lambda j: (0, j)),
        out_shape=jax.ShapeDtypeStruct((m, n), out_dtype),
        compiler_params=_cparams(1),
        name=name,
    )(a, b)


def _rwkv_prep_kernel(p_ref, mu_ref, w0_ref, wup_ref, a0_ref, aup_ref, gup_ref, kk_ref, ka_ref,
                      rk_ref, seg_ref, tri_ref,
                      at_ref, bt_ref, kt_ref, rt_ref, v_ref, wc_ref, bonus_ref, g_ref,
                      carry_ref, *, tb):
    i = pl.program_id(0)

    @pl.when(i == 0)
    def _():
        carry_ref[...] = jnp.zeros_like(carry_ref)

    def mixed(lo, hi):
        p = p_ref[:, lo:hi]
        prev_last = carry_ref[0:1, lo:hi]
        shifted = pltpu.roll(p, 1, axis=0)
        row = lax.broadcasted_iota(jnp.int32, p.shape, 0)
        prev = jnp.where(row == 0, prev_last, shifted)
        carry_ref[0:1, lo:hi] = p[tb - 1:tb, :]
        return p + (prev - p) * mu_ref[:, lo:hi]

    w3 = 3 * RWKV_WIDTH
    r = mixed(0, RWKV_WIDTH)
    k = mixed(RWKV_WIDTH, 2 * RWKV_WIDTH)
    v = mixed(2 * RWKV_WIDTH, w3)
    w_d = mixed(w3, w3 + DECAY_PAD)
    a_d = mixed(w3 + DECAY_PAD, w3 + DECAY_PAD + AAA_PAD)
    g_d = mixed(w3 + DECAY_PAD + AAA_PAD, RWKV_PCOLS)

    z = w0_ref[...] + _dot(jnp.tanh(w_d), wup_ref[...])
    nz = -z
    softplus = jnp.maximum(nz, 0.0) + jnp.log1p(jnp.exp(-jnp.abs(nz)))
    lw = -jnp.exp(-softplus - 0.5)
    a = jax.nn.sigmoid(a0_ref[...] + _dot(a_d, aup_ref[...]))
    g_ref[...] = _dot(jax.nn.sigmoid(g_d), gup_ref[...])

    seg = seg_ref[...]

    def segsum(x):
        h1, h2 = _split2(x)
        return (jnp.dot(h1, seg, preferred_element_type=F32)
                + jnp.dot(h2, seg, preferred_element_type=F32))

    kk0 = k * kk_ref[...]
    kk = kk0 / jnp.maximum(jnp.sqrt(segsum(kk0 * kk0)), 1e-12)
    k2 = k * (1.0 + (a - 1.0) * ka_ref[...])
    bonus_ref[...] = segsum(r * k2 * rk_ref[...]) * v

    tri = tri_ref[...]
    l1, l2, l3 = _split3(lw)
    cs = (jnp.dot(tri, l1, preferred_element_type=F32)
          + jnp.dot(tri, l2, preferred_element_type=F32)
          + jnp.dot(tri, l3, preferred_element_type=F32))
    cl = cs[:tb]
    tot = cs[tb:]
    e_neg = jnp.exp(-cl)
    at_ref[...] = (-kk * jnp.exp(cl - lw)).astype(BF16)
    bt_ref[...] = (kk * a * e_neg).astype(BF16)
    kt_ref[...] = (k2 * e_neg).astype(BF16)
    rt_ref[...] = (r * jnp.exp(cl)).astype(BF16)
    v_ref[...] = v.astype(BF16)
    for c in range(tb // CHUNK):
        wc_ref[c] = jnp.exp(tot[c * CHUNK:c * CHUNK + 1, :])


def _rwkv_prep(proj_r, mu, w0, wup, a0, aup, gup, k_k, k_a, r_k, *, tb):
    s = proj_r.shape[0]
    assert s % tb == 0 and tb % CHUNK == 0
    w = RWKV_WIDTH
    lane = jnp.arange(w)
    seg = (lane[:, None] // HEAD_DIM == lane[None, :] // HEAD_DIM).astype(BF16)
    t = jnp.arange(tb)
    same = t[:, None] // CHUNK == t[None, :] // CHUNK
    tri = jnp.concatenate([(same & (t[:, None] >= t[None, :])), same], axis=0).astype(BF16)
    row = lambda x: x.reshape(1, -1).astype(F32)
    const = lambda shape: pl.BlockSpec(shape, lambda i: (0,) * len(shape))
    tile = lambda width: pl.BlockSpec((tb, width), lambda i: (i, 0))
    outs = pl.pallas_call(
        functools.partial(_rwkv_prep_kernel, tb=tb),
        grid=(s // tb,),
        in_specs=[tile(RWKV_PCOLS), const((1, RWKV_PCOLS)), const((1, w)), const((DECAY_PAD, w)),
                  const((1, w)), const((AAA_PAD, w)), const((GATE_PAD, w)), const((1, w)),
                  const((1, w)), const((1, w)), const((w, w)), const((2 * tb, tb))],
        out_specs=[tile(w), tile(w), tile(w), tile(w), tile(w),
                   pl.BlockSpec((tb // CHUNK, 1, w), lambda i: (i, 0, 0)),
                   tile(w), tile(w)],
        out_shape=[jax.ShapeDtypeStruct((s, w), BF16)] * 5
        + [jax.ShapeDtypeStruct((s // CHUNK, 1, w), F32),
           jax.ShapeDtypeStruct((s, w), F32), jax.ShapeDtypeStruct((s, w), F32)],
        scratch_shapes=[pltpu.VMEM((SUBLANES, RWKV_PCOLS), F32)],
        compiler_params=_cparams(1),
        name="rwkv_prep",
    )(proj_r, row(mu), row(w0), wup, row(a0), aup, gup, row(k_k), row(k_a), row(r_k), seg, tri)
    return outs


def _expand(x, lane_head):
    zero = jnp.zeros_like(x)
    return jnp.concatenate([jnp.where(lane_head == h, x, zero) for h in range(GROUP_HEADS)], axis=0)


def _rwkv_chunk_kernel(at_ref, bt_ref, kt_ref, rt_ref, v_ref, wc_ref, bonus_ref, g_ref,
                       gain_ref, bias_ref, o_ref, state_ref, *, nc):
    @pl.when(pl.program_id(1) == 0)
    def _():
        state_ref[...] = jnp.zeros_like(state_ref)

    n = GROUP_W
    row = lax.broadcasted_iota(jnp.int32, (n, n), 0)
    col = lax.broadcasted_iota(jnp.int32, (n, n), 1)
    t_row = row & (CHUNK - 1)
    t_col = col & (CHUNK - 1)
    strict = t_row > t_col
    incl = t_row >= t_col
    eye = row == col
    same_head = (row >> 6) == (col >> 6)
    lane_head = lax.broadcasted_iota(jnp.int32, (CHUNK, n), 1) >> 6
    eye_f = eye.astype(F32)

    def blk(bits):
        return (row >> bits) == (col >> bits)

    pre = []
    for c in range(nc):
        sl = slice(c * CHUNK, (c + 1) * CHUNK)
        wc = wc_ref[c]
        bt = bt_ref[sl, :].astype(F32)
        kt = kt_ref[sl, :].astype(F32)
        ae = _expand(at_ref[sl, :].astype(F32), lane_head)
        be = _expand(bt, lane_head)
        ke = _expand(kt, lane_head)
        re = _expand(rt_ref[sl, :].astype(F32), lane_head)
        ve = _expand(v_ref[sl, :].astype(F32), lane_head)
        bhe = _expand(bt * wc, lane_head)
        khe = _expand(kt * wc, lane_head)

        lab = jnp.where(strict, _dot_nt(ae, be), 0.0)
        lak = jnp.where(strict, _dot_nt(ae, ke), 0.0)
        arb = jnp.where(incl, _dot_nt(re, be), 0.0)
        ark = jnp.where(incl, _dot_nt(re, ke), 0.0)

        l8 = jnp.where(blk(3), lab, 0.0)
        minv = eye_f + l8
        pw = l8
        for _ in range(2):
            pw = _dot(pw, pw)
            minv = minv + _dot(pw, minv)
        for bits in (3, 4, 5):
            off = jnp.where(blk(bits + 1) & jnp.logical_not(blk(bits)), lab, 0.0)
            minv = minv + _dot(_dot(minv, off), minv)

        lv = _dot(lak, ve)
        p = _dot(minv, ae)
        q = _dot(minv, lv)
        g_mat = re + _dot(arb, p)
        y0 = _dot(arb, q) + _dot(ark, ve)
        phi = jnp.where(eye, jnp.broadcast_to(wc, (n, n)), 0.0) + _dot_tn(bhe, p)
        psi = _dot_tn(bhe, q) + _dot_tn(khe, ve)
        pre.append((g_mat, y0, phi, psi))

    state = state_ref[...]
    for c in range(nc):
        g_mat, y0, phi, psi = pre[c]
        y = _dot(g_mat, state) + y0
        state = _dot(phi, state) + psi
        mean = jnp.sum(y, axis=1, keepdims=True) * (1.0 / HEAD_DIM)
        yc = jnp.where(same_head, y - mean, 0.0)
        var = jnp.sum(yc * yc, axis=1, keepdims=True) * (1.0 / HEAD_DIM)
        yn = yc * lax.rsqrt(var + RWKV_GN_EPS)
        yw = yn[0:CHUNK] + yn[CHUNK:2 * CHUNK] + yn[2 * CHUNK:3 * CHUNK] + yn[3 * CHUNK:4 * CHUNK]
        sl = slice(c * CHUNK, (c + 1) * CHUNK)
        out = (yw * gain_ref[...] + bias_ref[...] + bonus_ref[sl, :]) * g_ref[sl, :]
        o_ref[sl, :] = out.astype(o_ref.dtype)
    state_ref[...] = state


def _rwkv_chunks(at, bt, kt, rt, v, wc, bonus, g, gn_gain, gn_bias, *, nc):
    s, w = at.shape
    tb = nc * CHUNK
    assert s % tb == 0 and w % GROUP_W == 0
    tile = pl.BlockSpec((tb, GROUP_W), lambda gi, ti: (ti, gi))
    rowspec = pl.BlockSpec((1, GROUP_W), lambda gi, ti: (0, gi))
    return pl.pallas_call(
        functools.partial(_rwkv_chunk_kernel, nc=nc),
        grid=(w // GROUP_W, s // tb),
        in_specs=[tile, tile, tile, tile, tile,
                  pl.BlockSpec((nc, 1, GROUP_W), lambda gi, ti: (ti, 0, gi)),
                  tile, tile, rowspec, rowspec],
        out_specs=tile,
        out_shape=jax.ShapeDtypeStruct((s, w), BF16),
        scratch_shapes=[pltpu.VMEM((GROUP_W, GROUP_W), F32)],
        compiler_params=_cparams(2),
        name="rwkv_chunks",
    )(at, bt, kt, rt, v, wc, bonus, g, gn_gain.reshape(1, -1).astype(F32),
      gn_bias.reshape(1, -1).astype(F32))


def _rope_kernel(pos_ref, invf_ref, p_ref, q_ref, k_ref, *, tb):
    ang = pos_ref[...].astype(F32) * invf_ref[...]
    lane = lax.broadcasted_iota(jnp.int32, (tb, LANES), 1)
    first_half = (lane & (HEAD_DIM - 1)) < (HEAD_DIM // 2)
    cos = jnp.cos(ang)
    sin = jnp.sin(ang)
    sin = jnp.where(first_half, -sin, sin)

    def rope(t):
        partner = jnp.where(first_half, pltpu.roll(t, LANES - HEAD_DIM // 2, axis=1),
                            pltpu.roll(t, HEAD_DIM // 2, axis=1))
        return t * cos + partner * sin

    scale = HEAD_DIM ** -0.5 * LOG2_E
    for j in range(DIFF_WIDTH // LANES):
        sl = slice(j * LANES, (j + 1) * LANES)
        q_ref[:, sl] = (rope(p_ref[:, sl]) * scale).astype(BF16)
        ks = slice(DIFF_WIDTH + j * LANES, DIFF_WIDTH + (j + 1) * LANES)
        k_ref[:, sl] = rope(p_ref[:, ks]).astype(BF16)


def _rope(proj_qk, positions, *, tb):
    s = proj_qk.shape[0]
    half = jnp.arange(0, HEAD_DIM, 2, dtype=F32)
    inv_freq = ROPE_THETA ** (-half / HEAD_DIM)
    invf = jnp.tile(inv_freq, 2 * LANES // HEAD_DIM).reshape(1, LANES)
    tile = pl.BlockSpec((tb, DIFF_WIDTH), lambda i: (i, 0))
    return pl.pallas_call(
        functools.partial(_rope_kernel, tb=tb),
        grid=(s // tb,),
        in_specs=[pl.BlockSpec((tb, 1), lambda i: (i, 0)),
                  pl.BlockSpec((1, LANES), lambda i: (0, 0)),
                  pl.BlockSpec((tb, 2 * DIFF_WIDTH), lambda i: (i, 0))],
        out_specs=[tile, tile],
        out_shape=[jax.ShapeDtypeStruct((s, DIFF_WIDTH), BF16)] * 2,
        compiler_params=_cparams(1),
        name="rope",
    )(positions.reshape(s, 1), invf, proj_qk)


ATT_STRIP = 128


def _attn_kernel(lam_ref, gain_ref, q_ref, k_ref, vt_ref, o_ref, acc_ref, *, bq):
    i = pl.program_id(1)
    nq = 2 * bq
    q = q_ref[...]
    lane = lax.broadcasted_iota(jnp.int32, q.shape, 1)
    zero = jnp.zeros_like(q)
    q_all = jnp.concatenate([jnp.where(lane < HEAD_DIM, q, zero),
                             jnp.where(lane >= HEAD_DIM, q, zero)], axis=0)

    def process(j, m, l, masked):
        start = pl.multiple_of(j * bq, bq)
        kb = k_ref[pl.ds(start, bq), :]
        vtb = vt_ref[:, pl.ds(start, bq)]
        s_all = lax.dot_general(kb, q_all, (((1,), (1,)), ((), ())),
                                preferred_element_type=F32)
        p_strips, m_new, l_new, alphas = [], [], [], []
        for c in range(nq // ATT_STRIP):
            sl = slice(c * ATT_STRIP, (c + 1) * ATT_STRIP)
            s = s_all[:, sl]
            if masked:
                key = lax.broadcasted_iota(jnp.int32, s.shape, 0)
                qry = lax.broadcasted_iota(jnp.int32, s.shape, 1) + (c * ATT_STRIP) % bq
                s = jnp.where(key <= qry, s, -1e30)
            mn = jnp.maximum(m[:, sl], jnp.max(s, axis=0, keepdims=True))
            p = jnp.exp2(s - mn)
            alpha = jnp.exp2(m[:, sl] - mn)
            m_new.append(mn)
            l_new.append(alpha * l[:, sl] + jnp.sum(p, axis=0, keepdims=True))
            alphas.append(alpha)
            p_strips.append(p.astype(BF16))
        p_all = jnp.concatenate(p_strips, axis=1)
        alpha = jnp.concatenate(alphas, axis=1)
        acc_ref[...] = alpha * acc_ref[...] + jnp.dot(vtb, p_all, preferred_element_type=F32)
        return jnp.concatenate(m_new, axis=1), jnp.concatenate(l_new, axis=1)

    acc_ref[...] = jnp.zeros_like(acc_ref)
    init = (jnp.full((1, nq), -1e30, F32), jnp.zeros((1, nq), F32))
    m, l = lax.fori_loop(0, i, lambda j, c: process(j, c[0], c[1], False), init)
    m, l = process(i, m, l, True)

    lam_v = lam_ref[...]
    lam = (jnp.exp(jnp.sum(lam_v[0:1] * lam_v[1:2], axis=1, keepdims=True))
           - jnp.exp(jnp.sum(lam_v[2:3] * lam_v[3:4], axis=1, keepdims=True)) + LAMBDA_INIT)
    o = acc_ref[...] / l
    out = o[:, :bq] - lam * o[:, bq:]
    out = out * lax.rsqrt(jnp.mean(jnp.square(out), axis=0, keepdims=True) + SUBLN_EPS)
    out = out * gain_ref[...] * (1.0 - LAMBDA_INIT)
    o_ref[...] = out.T.astype(o_ref.dtype)


def _attention(q, k, vt, lam_vecs, subln_gain, *, bq):
    s = q.shape[0]
    assert s % bq == 0 and bq % ATT_STRIP == 0
    return pl.pallas_call(
        functools.partial(_attn_kernel, bq=bq),
        grid=(DIFF_HEADS, s // bq),
        in_specs=[pl.BlockSpec((4, HEAD_DIM), lambda h, i: (0, 0)),
                  pl.BlockSpec((DIFF_V_DIM, 1), lambda h, i: (0, 0)),
                  pl.BlockSpec((bq, DIFF_V_DIM), lambda h, i: (i, h)),
                  pl.BlockSpec((s, DIFF_V_DIM), lambda h, i: (0, h)),
                  pl.BlockSpec((DIFF_V_DIM, s), lambda h, i: (h, 0))],
        out_specs=pl.BlockSpec((bq, DIFF_V_DIM), lambda h, i: (i, h)),
        out_shape=jax.ShapeDtypeStruct((s, DIFF_WIDTH), BF16),
        scratch_shapes=[pltpu.VMEM((DIFF_V_DIM, 2 * bq), F32)],
        compiler_params=_cparams(2),
        name="diff_attn",
    )(lam_vecs, subln_gain.reshape(-1, 1).astype(F32), q, k, vt)


def _layer_norm(z, gain, bias):
    mean = jnp.mean(z, axis=-1, keepdims=True)
    zc = z - mean
    var = jnp.mean(zc * zc, axis=-1, keepdims=True)
    return zc * lax.rsqrt(var + LN_EPS) * gain + bias


def _outproj_kernel(yr_ref, yd_ref, wr_ref, wd_ref, x_ref, gain_ref, bias_ref, o32_ref, o16_ref):
    mix = (jnp.dot(yr_ref[...], wr_ref[...], preferred_element_type=F32)
           + jnp.dot(yd_ref[...], wd_ref[...], preferred_element_type=F32))
    y = _layer_norm(DEEPNORM_ALPHA * x_ref[...] + mix, gain_ref[...], bias_ref[...])
    o32_ref[...] = y
    o16_ref[...] = y.astype(BF16)


def _outproj(y_rwkv, y_diff, w_r, w_d, x, gain, bias, *, bm):
    s = x.shape[0]
    const = lambda shape: pl.BlockSpec(shape, lambda i: (0, 0))
    return pl.pallas_call(
        _outproj_kernel,
        grid=(s // bm,),
        in_specs=[pl.BlockSpec((bm, RWKV_WIDTH), lambda i: (i, 0)),
                  pl.BlockSpec((bm, DIFF_WIDTH), lambda i: (i, 0)),
                  const((RWKV_WIDTH, D_MODEL)), const((DIFF_WIDTH, D_MODEL)),
                  pl.BlockSpec((bm, D_MODEL), lambda i: (i, 0)),
                  const((1, D_MODEL)), const((1, D_MODEL))],
        out_specs=[pl.BlockSpec((bm, D_MODEL), lambda i: (i, 0))] * 2,
        out_shape=[jax.ShapeDtypeStruct((s, D_MODEL), F32), jax.ShapeDtypeStruct((s, D_MODEL), BF16)],
        compiler_params=_cparams(1),
        name="outproj_ln",
    )(y_rwkv, y_diff, w_r, w_d, x, gain.reshape(1, -1), bias.reshape(1, -1))


def _ff2_kernel(h_ref, w_ref, x_ref, gain_ref, bias_ref, o_ref, acc_ref):
    kk = pl.program_id(1)

    @pl.when(kk == 0)
    def _():
        acc_ref[...] = jnp.zeros_like(acc_ref)

    acc_ref[...] += jnp.dot(h_ref[...], w_ref[...], preferred_element_type=F32)

    @pl.when(kk == pl.num_programs(1) - 1)
    def _():
        o_ref[...] = _layer_norm(DEEPNORM_ALPHA * x_ref[...] + acc_ref[...],
                                 gain_ref[...], bias_ref[...])


def _ff2(h, w, x, gain, bias, *, bm, bk):
    s, kdim = h.shape
    const = lambda shape: pl.BlockSpec(shape, lambda i, k: (0, 0))
    return pl.pallas_call(
        _ff2_kernel,
        grid=(s // bm, kdim // bk),
        in_specs=[pl.BlockSpec((bm, bk), lambda i, k: (i, k)),
                  pl.BlockSpec((bk, D_MODEL), lambda i, k: (k, 0)),
                  pl.BlockSpec((bm, D_MODEL), lambda i, k: (i, 0)),
                  const((1, D_MODEL)), const((1, D_MODEL))],
        out_specs=pl.BlockSpec((bm, D_MODEL), lambda i, k: (i, 0)),
        out_shape=jax.ShapeDtypeStruct((s, D_MODEL), F32),
        scratch_shapes=[pltpu.VMEM((bm, D_MODEL), F32)],
        compiler_params=_cparams(2),
        name="ff2_ln",
    )(h, w, x, gain.reshape(1, -1), bias.reshape(1, -1))


def _pad_rows(w, rows):
    return jnp.pad(w, ((0, rows - w.shape[0]), (0, 0)))


def _layer(x, positions, w_in, mu_shift, w0, w_up, a0, a_up, g_up, k_k, k_a, r_k, gn_gain, gn_bias,
           lq1, lk1, lq2, lk2, subln_gain, w_out, ln1_gain, ln1_bias, w_ff1, w_ff2, ln2_gain, ln2_bias,
           *, bm, prep_tb, nc, rope_tb, bq):
    w3 = 3 * RWKV_WIDTH
    rwkv_cols = w3 + DECAY_LORA + AAA_LORA + GATE_LORA

    def pad_cols(a, lo, hi, width):
        return jnp.pad(a[..., lo:hi], [(0, 0)] * (a.ndim - 1) + [(0, width - (hi - lo))])

    def rwkv_layout(a):
        return jnp.concatenate([
            a[..., :w3],
            pad_cols(a, w3, w3 + DECAY_LORA, DECAY_PAD),
            pad_cols(a, w3 + DECAY_LORA, w3 + DECAY_LORA + AAA_LORA, AAA_PAD),
            pad_cols(a, w3 + DECAY_LORA + AAA_LORA, rwkv_cols, GATE_PAD)], axis=-1)

    xb = x.astype(BF16)
    w_r = rwkv_layout(w_in[:, :rwkv_cols]).astype(BF16)
    w_qk = w_in[:, rwkv_cols:rwkv_cols + 2 * DIFF_WIDTH].astype(BF16)
    w_vt = w_in[:, rwkv_cols + 2 * DIFF_WIDTH:].T.astype(BF16)
    proj_r = _matmul(xb, w_r, bm=bm, bn=512, out_dtype=F32, name="proj_rwkv")
    proj_qk = _matmul(xb, w_qk, bm=bm, bn=512, out_dtype=F32, name="proj_qk")
    vt = _matmul_nt(w_vt, xb, bn=512, out_dtype=BF16, name="proj_vt")

    at, bt, kt, rt, v_r, wc, bonus, gate = _rwkv_prep(
        proj_r, rwkv_layout(mu_shift), w0,
        _pad_rows(w_up, DECAY_PAD).astype(BF16), a0, _pad_rows(a_up, AAA_PAD).astype(BF16),
        _pad_rows(g_up, GATE_PAD).astype(BF16), k_k, k_a, r_k.reshape(-1), tb=prep_tb)
    y_rwkv = _rwkv_chunks(at, bt, kt, rt, v_r, wc, bonus, gate, gn_gain, gn_bias, nc=nc)

    q, k = _rope(proj_qk, positions, tb=rope_tb)
    lam_vecs = jnp.stack([lq1, lk1, lq2, lk2]).astype(F32)
    y_diff = _attention(q, k, vt, lam_vecs, subln_gain, bq=bq)

    wo = w_out.astype(BF16)
    x1, x1b = _outproj(y_rwkv, y_diff, wo[:RWKV_WIDTH], wo[RWKV_WIDTH:], x, ln1_gain, ln1_bias, bm=256)
    h = _matmul(x1b, w_ff1.astype(BF16), bm=bm, bn=512, out_dtype=BF16, act="relu2", name="ff1")
    return _ff2(h, w_ff2.astype(BF16), x1, ln2_gain, ln2_bias, bm=512, bk=1024)


def kernel(x, positions, w_in, mu_shift, rwkv_w0, rwkv_w_up, rwkv_a0, rwkv_a_up, rwkv_g_up, rwkv_k_k,
           rwkv_k_a, rwkv_r_k, rwkv_gn_gain, rwkv_gn_bias, diff_lambda_q1, diff_lambda_k1,
           diff_lambda_q2, diff_lambda_k2, diff_subln_gain, w_out, ln1_gain, ln1_bias, w_ff1, w_ff2,
           ln2_gain, ln2_bias):
    batch, seq, _ = x.shape
    assert batch == 1 and w_in.shape[0] == 1
    out = _layer(
        x[0], positions[0], w_in[0], mu_shift[0], rwkv_w0[0], rwkv_w_up[0], rwkv_a0[0], rwkv_a_up[0],
        rwkv_g_up[0], rwkv_k_k[0], rwkv_k_a[0], rwkv_r_k[0], rwkv_gn_gain[0], rwkv_gn_bias[0],
        diff_lambda_q1[0], diff_lambda_k1[0], diff_lambda_q2[0], diff_lambda_k2[0],
        diff_subln_gain[0], w_out[0], ln1_gain[0], ln1_bias[0], w_ff1[0], w_ff2[0], ln2_gain[0],
        ln2_bias[0], bm=1024, prep_tb=256, nc=2, rope_tb=512, bq=512)
    return out[None]
```

```python
import functools
import math

import jax
import jax.numpy as jnp
from jax import lax
from jax.experimental import pallas as pl
from jax.experimental.pallas import tpu as pltpu

F32 = jnp.float32
BF16 = jnp.bfloat16

D_MODEL = 2048
RWKV_WIDTH = 1024
HEAD_DIM = 64
DECAY_LORA = 64
AAA_LORA = 64
GATE_LORA = 160
DIFF_WIDTH = 1024
DIFF_V_DIM = 128
DIFF_HEADS = 8
D_FF = 8192
ROPE_THETA = 10000.0
LN_EPS = 1e-5
RWKV_GN_EPS = 64e-5
SUBLN_EPS = 1e-5
DEEPNORM_ALPHA = 2.0 ** 0.25
LAMBDA_INIT = 0.8 - 0.6 * math.exp(0.0)
LOG2_E = math.log2(math.e)

LANES = 128
SUBLANES = 8
VMEM_LIMIT_BYTES = 56 * 1024 * 1024

DECAY_PAD = 128
AAA_PAD = 128
GATE_PAD = 256
RWKV_PCOLS = 3 * RWKV_WIDTH + DECAY_PAD + AAA_PAD + GATE_PAD
CHUNK = 64
GROUP_HEADS = 4
GROUP_W = GROUP_HEADS * HEAD_DIM


def _cparams(n_axes):
    return pltpu.CompilerParams(
        dimension_semantics=("arbitrary",) * n_axes,
        vmem_limit_bytes=VMEM_LIMIT_BYTES,
    )


def _dot(a, b):
    return jnp.dot(a.astype(BF16), b.astype(BF16), preferred_element_type=F32)


def _dot_nt(a, b):
    return lax.dot_general(a.astype(BF16), b.astype(BF16), (((1,), (1,)), ((), ())),
                           preferred_element_type=F32)


def _dot_tn(a, b):
    return lax.dot_general(a.astype(BF16), b.astype(BF16), (((0,), (0,)), ((), ())),
                           preferred_element_type=F32)


def _split3(x):
    h1 = x.astype(BF16)
    r1 = x - h1.astype(F32)
    h2 = r1.astype(BF16)
    h3 = (r1 - h2.astype(F32)).astype(BF16)
    return h1, h2, h3


def _split2(x):
    h1 = x.astype(BF16)
    h2 = (x - h1.astype(F32)).astype(BF16)
    return h1, h2


def _mm_kernel(a_ref, b_ref, o_ref, *, act):
    acc = jnp.dot(a_ref[...], b_ref[...], preferred_element_type=F32)
    if act == "relu2":
        acc = jnp.square(jnp.maximum(acc, 0.0))
    o_ref[...] = acc.astype(o_ref.dtype)


def _matmul(a, b, *, bm, bn, out_dtype, act=None, name):
    m, k = a.shape
    _, n = b.shape
    assert m % bm == 0 and n % bn == 0
    return pl.pallas_call(
        functools.partial(_mm_kernel, act=act),
        grid=(m // bm, n // bn),
        in_specs=[pl.BlockSpec((bm, k), lambda i, j: (i, 0)),
                  pl.BlockSpec((k, bn), lambda i, j: (0, j))],
        out_specs=pl.BlockSpec((bm, bn), lambda i, j: (i, j)),
        out_shape=jax.ShapeDtypeStruct((m, n), out_dtype),
        compiler_params=_cparams(2),
        name=name,
    )(a, b)


def _mm_nt_kernel(a_ref, b_ref, o_ref):
    o_ref[...] = lax.dot_general(a_ref[...], b_ref[...], (((1,), (1,)), ((), ())),
                                 preferred_element_type=F32).astype(o_ref.dtype)


def _matmul_nt(a, b, *, bn, out_dtype, name):
    m, k = a.shape
    n, _ = b.shape
    assert n % bn == 0
    return pl.pallas_call(
        _mm_nt_kernel,
        grid=(n // bn,),
        in_specs=[pl.BlockSpec((m, k), lambda j: (0, 0)),
                  pl.BlockSpec((bn, k), lambda j: (j, 0))],
        out_specs=pl.BlockSpec((m, bn), lambda j: (0, j)),
        out_shape=jax.ShapeDtypeStruct((m, n), out_dtype),
        compiler_params=_cparams(1),
        name=name,
    )(a, b)


def _rwkv_prep_kernel(p_ref, mu_ref, w0_ref, wup_ref, a0_ref, aup_ref, gup_ref, kk_ref, ka_ref,
                      rk_ref, seg_ref, tri_ref,
                      at_ref, bt_ref, kt_ref, rt_ref, v_ref, wc_ref, bonus_ref, g_ref,
                      carry_ref, *, tb):
    i = pl.program_id(0)

    @pl.when(i == 0)
    def _():
        carry_ref[...] = jnp.zeros_like(carry_ref)

    def mixed(lo, hi):
        p = p_ref[:, lo:hi]
        prev_last = carry_ref[0:1, lo:hi]
        shifted = pltpu.roll(p, 1, axis=0)
        row = lax.broadcasted_iota(jnp.int32, p.shape, 0)
        prev = jnp.where(row == 0, prev_last, shifted)
        carry_ref[0:1, lo:hi] = p[tb - 1:tb, :]
        return p + (prev - p) * mu_ref[:, lo:hi]

    w3 = 3 * RWKV_WIDTH
    r = mixed(0, RWKV_WIDTH)
    k = mixed(RWKV_WIDTH, 2 * RWKV_WIDTH)
    v = mixed(2 * RWKV_WIDTH, w3)
    w_d = mixed(w3, w3 + DECAY_PAD)
    a_d = mixed(w3 + DECAY_PAD, w3 + DECAY_PAD + AAA_PAD)
    g_d = mixed(w3 + DECAY_PAD + AAA_PAD, RWKV_PCOLS)

    z = w0_ref[...] + _dot(jnp.tanh(w_d), wup_ref[...])
    nz = -z
    softplus = jnp.maximum(nz, 0.0) + jnp.log1p(jnp.exp(-jnp.abs(nz)))
    lw = -jnp.exp(-softplus - 0.5)
    a = jax.nn.sigmoid(a0_ref[...] + _dot(a_d, aup_ref[...]))
    g_ref[...] = _dot(jax.nn.sigmoid(g_d), gup_ref[...])

    seg = seg_ref[...]

    def segsum(x):
        h1, h2 = _split2(x)
        return (jnp.dot(h1, seg, preferred_element_type=F32)
                + jnp.dot(h2, seg, preferred_element_type=F32))

    kk0 = k * kk_ref[...]
    kk = kk0 / jnp.maximum(jnp.sqrt(segsum(kk0 * kk0)), 1e-12)
    k2 = k * (1.0 + (a - 1.0) * ka_ref[...])
    bonus_ref[...] = segsum(r * k2 * rk_ref[...]) * v

    tri = tri_ref[...]
    l1, l2, l3 = _split3(lw)
    cs = (jnp.dot(tri, l1, preferred_element_type=F32)
          + jnp.dot(tri, l2, preferred_element_type=F32)
          + jnp.dot(tri, l3, preferred_element_type=F32))
    cl = cs[:tb]
    tot = cs[tb:]
    e_neg = jnp.exp(-cl)
    at_ref[...] = (-kk * jnp.exp(cl - lw)).astype(BF16)
    bt_ref[...] = (kk * a * e_neg).astype(BF16)
    kt_ref[...] = (k2 * e_neg).astype(BF16)
    rt_ref[...] = (r * jnp.exp(cl)).astype(BF16)
    v_ref[...] = v.astype(BF16)
    for c in range(tb // CHUNK):
        wc_ref[c] = jnp.exp(tot[c * CHUNK:c * CHUNK + 1, :])


def _rwkv_prep(proj_r, mu, w0, wup, a0, aup, gup, k_k, k_a, r_k, *, tb):
    s = proj_r.shape[0]
    assert s % tb == 0 and tb % CHUNK == 0
    w = RWKV_WIDTH
    lane = jnp.arange(w)
    seg = (lane[:, None] // HEAD_DIM == lane[None, :] // HEAD_DIM).astype(BF16)
    t = jnp.arange(tb)
    same = t[:, None] // CHUNK == t[None, :] // CHUNK
    tri = jnp.concatenate([(same & (t[:, None] >= t[None, :])), same], axis=0).astype(BF16)
    row = lambda x: x.reshape(1, -1).astype(F32)
    const = lambda shape: pl.BlockSpec(shape, lambda i: (0,) * len(shape))
    tile = lambda width: pl.BlockSpec((tb, width), lambda i: (i, 0))
    outs = pl.pallas_call(
        functools.partial(_rwkv_prep_kernel, tb=tb),
        grid=(s // tb,),
        in_specs=[tile(RWKV_PCOLS), const((1, RWKV_PCOLS)), const((1, w)), const((DECAY_PAD, w)),
                  const((1, w)), const((AAA_PAD, w)), const((GATE_PAD, w)), const((1, w)),
                  const((1, w)), const((1, w)), const((w, w)), const((2 * tb, tb))],
        out_specs=[tile(w), tile(w), tile(w), tile(w), tile(w),
                   pl.BlockSpec((tb // CHUNK, 1, w), lambda i: (i, 0, 0)),
                   tile(w), tile(w)],
        out_shape=[jax.ShapeDtypeStruct((s, w), BF16)] * 5
        + [jax.ShapeDtypeStruct((s // CHUNK, 1, w), F32),
           jax.ShapeDtypeStruct((s, w), F32), jax.ShapeDtypeStruct((s, w), F32)],
        scratch_shapes=[pltpu.VMEM((SUBLANES, RWKV_PCOLS), F32)],
        compiler_params=_cparams(1),
        name="rwkv_prep",
    )(proj_r, row(mu), row(w0), wup, row(a0), aup, gup, row(k_k), row(k_a), row(r_k), seg, tri)
    return outs


def _bdot(a, b):
    return jnp.dot(a, b, preferred_element_type=F32)


def _bdot_nt(a, b):
    return lax.dot_general(a, b, (((1,), (1,)), ((), ())), preferred_element_type=F32)


def _bdot_tn(a, b):
    return lax.dot_general(a, b, (((0,), (0,)), ((), ())), preferred_element_type=F32)


(MASK_STRICT, MASK_INCL, MASK_EYE, MASK_BLK8, MASK_OFF16, MASK_OFF32, MASK_OFF64,
 MASK_HEAD) = range(8)


def _chunk_masks():
    n = GROUP_W
    r = jnp.arange(n)[:, None]
    c = jnp.arange(n)[None, :]
    head = (r // CHUNK) == (c // CHUNK)
    t_r, t_c = r % CHUNK, c % CHUNK
    blk = lambda size: (r // size) == (c // size)
    masks = [head & (t_r > t_c), head & (t_r >= t_c), r == c, blk(8),
             blk(16) & ~blk(8), blk(32) & ~blk(16), blk(64) & ~blk(32), head]
    return jnp.stack(masks).astype(F32)


def _rwkv_chunk_kernel(at_ref, bt_ref, kt_ref, rt_ref, v_ref, wc_ref, bonus_ref, g_ref,
                       gain_ref, bias_ref, mask_ref, o_ref, state_ref, *, nc):
    @pl.when(pl.program_id(0) == 0)
    def _():
        state_ref[...] = jnp.zeros_like(state_ref)

    n = GROUP_W
    ng = RWKV_WIDTH // GROUP_W
    items = [(c, g) for c in range(nc) for g in range(ng)]
    lane_head = lax.broadcasted_iota(jnp.int32, (CHUNK, n), 1) >> 6

    def tile(ref, c, g):
        return ref[c * CHUNK:(c + 1) * CHUNK, g * n:(g + 1) * n]

    def expand(x):
        zero = jnp.zeros_like(x)
        return jnp.concatenate([jnp.where(lane_head == h, x, zero) for h in range(GROUP_HEADS)],
                               axis=0)

    def repeat(x):
        return jnp.concatenate([x] * GROUP_HEADS, axis=0)

    mask = lambda k: mask_ref[k]
    bf = lambda x: x.astype(BF16)
    each = lambda f: [f(*it) for it in items]
    zipped = lambda f, *lists: [f(*args) for args in zip(*lists)]

    at = each(lambda c, g: tile(at_ref, c, g))
    bt = each(lambda c, g: tile(bt_ref, c, g))
    kt = each(lambda c, g: tile(kt_ref, c, g))
    wc = each(lambda c, g: wc_ref[c][:, g * n:(g + 1) * n])
    ae = [expand(x) for x in at]
    be = [expand(x) for x in bt]
    ke = [expand(x) for x in kt]
    ve = each(lambda c, g: expand(tile(v_ref, c, g)))
    at4 = [repeat(x) for x in at]
    rt4 = each(lambda c, g: repeat(tile(rt_ref, c, g)))
    bhe = zipped(lambda x, w: expand(bf(x.astype(F32) * w)), bt, wc)
    khe = zipped(lambda x, w: expand(bf(x.astype(F32) * w)), kt, wc)

    lab = zipped(lambda a, b: _bdot_nt(a, b) * mask(MASK_STRICT), at4, be)
    lak = zipped(lambda a, b: bf(_bdot_nt(a, b) * mask(MASK_STRICT)), at4, ke)
    arb = zipped(lambda a, b: bf(_bdot_nt(a, b) * mask(MASK_INCL)), rt4, be)
    ark = zipped(lambda a, b: bf(_bdot_nt(a, b) * mask(MASK_INCL)), rt4, ke)
    lv = zipped(lambda a, b: bf(_bdot(a, b)), lak, ve)

    l8 = [x * mask(MASK_BLK8) for x in lab]
    minv = [mask(MASK_EYE) + x for x in l8]
    pw = [bf(x) for x in l8]
    for _ in range(2):
        pw2 = [_bdot(x, x) for x in pw]
        pw = [bf(x) for x in pw2]
        minv = zipped(lambda p, m: m + _bdot(p, bf(m)), pw, minv)
    for k in (MASK_OFF16, MASK_OFF32, MASK_OFF64):
        off = [bf(x * mask(k)) for x in lab]
        mb = [bf(m) for m in minv]
        tmp = zipped(lambda m, o: bf(_bdot(m, o)), mb, off)
        minv = zipped(lambda m, t, m16: m + _bdot(t, m16), minv, tmp, mb)
    mb = [bf(m) for m in minv]

    p = zipped(lambda m, a: bf(_bdot(m, a)), mb, ae)
    q = zipped(lambda m, x: bf(_bdot(m, x)), mb, lv)
    g_mat = zipped(lambda r4, a, x: bf(r4.astype(F32) * mask(MASK_HEAD) + _bdot(a, x)), rt4, arb, p)
    y0 = zipped(lambda a, x, b, v: _bdot(a, x) + _bdot(b, v), arb, q, ark, ve)
    phi = zipped(lambda w, b, x: bf(mask(MASK_EYE) * w + _bdot_tn(b, x)), wc, bhe, p)
    psi = zipped(lambda b, x, k, v: _bdot_tn(b, x) + _bdot_tn(k, v), bhe, q, khe, ve)

    state = [state_ref[g] for g in range(ng)]
    for c in range(nc):
        idx = [c * ng + g for g in range(ng)]
        sb = [bf(s) for s in state]
        ys = [_bdot(g_mat[i], sb[g]) + y0[i] for g, i in enumerate(idx)]
        state = [_bdot(phi[i], sb[g]) + psi[i] for g, i in enumerate(idx)]
        for g, y in enumerate(ys):
            mean = jnp.sum(y, axis=1, keepdims=True) * (1.0 / HEAD_DIM)
            yc = (y - mean) * mask(MASK_HEAD)
            var = jnp.sum(yc * yc, axis=1, keepdims=True) * (1.0 / HEAD_DIM)
            yn = yc * lax.rsqrt(var + RWKV_GN_EPS)
            yw = (yn[0:CHUNK] + yn[CHUNK:2 * CHUNK] + yn[2 * CHUNK:3 * CHUNK]
                  + yn[3 * CHUNK:4 * CHUNK])
            rows = slice(c * CHUNK, (c + 1) * CHUNK)
            cols = slice(g * n, (g + 1) * n)
            out = (yw * gain_ref[:, cols] + bias_ref[:, cols] + bonus_ref[rows, cols]) * g_ref[rows, cols]
            o_ref[rows, cols] = out.astype(o_ref.dtype)
    for g in range(ng):
        state_ref[g] = state[g]


def _rwkv_chunks(at, bt, kt, rt, v, wc, bonus, g, gn_gain, gn_bias, *, nc):
    s, w = at.shape
    tb = nc * CHUNK
    assert s % tb == 0 and w % GROUP_W == 0
    tile = pl.BlockSpec((tb, w), lambda ti: (ti, 0))
    rowspec = pl.BlockSpec((1, w), lambda ti: (0, 0))
    masks = _chunk_masks()
    return pl.pallas_call(
        functools.partial(_rwkv_chunk_kernel, nc=nc),
        grid=(s // tb,),
        in_specs=[tile, tile, tile, tile, tile,
                  pl.BlockSpec((nc, 1, w), lambda ti: (ti, 0, 0)),
                  tile, tile, rowspec, rowspec,
                  pl.BlockSpec(masks.shape, lambda ti: (0, 0, 0))],
        out_specs=tile,
        out_shape=jax.ShapeDtypeStruct((s, w), BF16),
        scratch_shapes=[pltpu.VMEM((w // GROUP_W, GROUP_W, GROUP_W), F32)],
        compiler_params=_cparams(1),
        name="rwkv_chunks",
    )(at, bt, kt, rt, v, wc, bonus, g, gn_gain.reshape(1, -1).astype(F32),
      gn_bias.reshape(1, -1).astype(F32), masks)


def _rope_kernel(pos_ref, invf_ref, p_ref, q_ref, k_ref, *, tb):
    ang = pos_ref[...].astype(F32) * invf_ref[...]
    lane = lax.broadcasted_iota(jnp.int32, (tb, LANES), 1)
    first_half = (lane & (HEAD_DIM - 1)) < (HEAD_DIM // 2)
    cos = jnp.cos(ang)
    sin = jnp.sin(ang)
    sin = jnp.where(first_half, -sin, sin)

    def rope(t):
        partner = jnp.where(first_half, pltpu.roll(t, LANES - HEAD_DIM // 2, axis=1),
                            pltpu.roll(t, HEAD_DIM // 2, axis=1))
        return t * cos + partner * sin

    scale = HEAD_DIM ** -0.5 * LOG2_E
    for j in range(DIFF_WIDTH // LANES):
        sl = slice(j * LANES, (j + 1) * LANES)
        q_ref[:, sl] = (rope(p_ref[:, sl]) * scale).astype(BF16)
        ks = slice(DIFF_WIDTH + j * LANES, DIFF_WIDTH + (j + 1) * LANES)
        k_ref[:, sl] = rope(p_ref[:, ks]).astype(BF16)


def _rope(proj_qk, positions, *, tb):
    s = proj_qk.shape[0]
    half = jnp.arange(0, HEAD_DIM, 2, dtype=F32)
    inv_freq = ROPE_THETA ** (-half / HEAD_DIM)
    invf = jnp.tile(inv_freq, 2 * LANES // HEAD_DIM).reshape(1, LANES)
    tile = pl.BlockSpec((tb, DIFF_WIDTH), lambda i: (i, 0))
    return pl.pallas_call(
        functools.partial(_rope_kernel, tb=tb),
        grid=(s // tb,),
        in_specs=[pl.BlockSpec((tb, 1), lambda i: (i, 0)),
                  pl.BlockSpec((1, LANES), lambda i: (0, 0)),
                  pl.BlockSpec((tb, 2 * DIFF_WIDTH), lambda i: (i, 0))],
        out_specs=[tile, tile],
        out_shape=[jax.ShapeDtypeStruct((s, DIFF_WIDTH), BF16)] * 2,
        compiler_params=_cparams(1),
        name="rope",
    )(positions.reshape(s, 1), invf, proj_qk)


ATT_STRIP = 128


def _attn_kernel(lam_ref, gain_ref, q_ref, k_ref, vt_ref, o_ref, acc_ref, *, bq):
    i = pl.program_id(1)
    nq = 2 * bq
    q = q_ref[...]
    lane = lax.broadcasted_iota(jnp.int32, q.shape, 1)
    zero = jnp.zeros_like(q)
    q_all = jnp.concatenate([jnp.where(lane < HEAD_DIM, q, zero),
                             jnp.where(lane >= HEAD_DIM, q, zero)], axis=0)

    def process(j, m, l, masked):
        start = pl.multiple_of(j * bq, bq)
        kb = k_ref[pl.ds(start, bq), :]
        vtb = vt_ref[:, pl.ds(start, bq)]
        s_all = lax.dot_general(kb, q_all, (((1,), (1,)), ((), ())),
                                preferred_element_type=F32)
        p_strips, m_new, l_new, alphas = [], [], [], []
        for c in range(nq // ATT_STRIP):
            sl = slice(c * ATT_STRIP, (c + 1) * ATT_STRIP)
            s = s_all[:, sl]
            if masked:
                key = lax.broadcasted_iota(jnp.int32, s.shape, 0)
                qry = lax.broadcasted_iota(jnp.int32, s.shape, 1) + (c * ATT_STRIP) % bq
                s = jnp.where(key <= qry, s, -1e30)
            mn = jnp.maximum(m[:, sl], jnp.max(s, axis=0, keepdims=True))
            p = jnp.exp2(s - mn)
            alpha = jnp.exp2(m[:, sl] - mn)
            m_new.append(mn)
            l_new.append(alpha * l[:, sl] + jnp.sum(p, axis=0, keepdims=True))
            alphas.append(alpha)
            p_strips.append(p.astype(BF16))
        p_all = jnp.concatenate(p_strips, axis=1)
        alpha = jnp.concatenate(alphas, axis=1)
        acc_ref[...] = alpha * acc_ref[...] + jnp.dot(vtb, p_all, preferred_element_type=F32)
        return jnp.concatenate(m_new, axis=1), jnp.concatenate(l_new, axis=1)

    acc_ref[...] = jnp.zeros_like(acc_ref)
    init = (jnp.full((1, nq), -1e30, F32), jnp.zeros((1, nq), F32))
    m, l = lax.fori_loop(0, i, lambda j, c: process(j, c[0], c[1], False), init)
    m, l = process(i, m, l, True)

    lam_v = lam_ref[...]
    lam = (jnp.exp(jnp.sum(lam_v[0:1] * lam_v[1:2], axis=1, keepdims=True))
           - jnp.exp(jnp.sum(lam_v[2:3] * lam_v[3:4], axis=1, keepdims=True)) + LAMBDA_INIT)
    o = acc_ref[...] / l
    out = o[:, :bq] - lam * o[:, bq:]
    out = out * lax.rsqrt(jnp.mean(jnp.square(out), axis=0, keepdims=True) + SUBLN_EPS)
    out = out * gain_ref[...] * (1.0 - LAMBDA_INIT)
    o_ref[...] = out.T.astype(o_ref.dtype)


def _attention(q, k, vt, lam_vecs, subln_gain, *, bq):
    s = q.shape[0]
    assert s % bq == 0 and bq % ATT_STRIP == 0
    return pl.pallas_call(
        functools.partial(_attn_kernel, bq=bq),
        grid=(DIFF_HEADS, s // bq),
        in_specs=[pl.BlockSpec((4, HEAD_DIM), lambda h, i: (0, 0)),
                  pl.BlockSpec((DIFF_V_DIM, 1), lambda h, i: (0, 0)),
                  pl.BlockSpec((bq, DIFF_V_DIM), lambda h, i: (i, h)),
                  pl.BlockSpec((s, DIFF_V_DIM), lambda h, i: (0, h)),
                  pl.BlockSpec((DIFF_V_DIM, s), lambda h, i: (h, 0))],
        out_specs=pl.BlockSpec((bq, DIFF_V_DIM), lambda h, i: (i, h)),
        out_shape=jax.ShapeDtypeStruct((s, DIFF_WIDTH), BF16),
        scratch_shapes=[pltpu.VMEM((DIFF_V_DIM, 2 * bq), F32)],
        compiler_params=_cparams(2),
        name="diff_attn",
    )(lam_vecs, subln_gain.reshape(-1, 1).astype(F32), q, k, vt)


def _layer_norm(z, gain, bias):
    mean = jnp.mean(z, axis=-1, keepdims=True)
    zc = z - mean
    var = jnp.mean(zc * zc, axis=-1, keepdims=True)
    return zc * lax.rsqrt(var + LN_EPS) * gain + bias


def _outproj_kernel(yr_ref, yd_ref, wr_ref, wd_ref, x_ref, gain_ref, bias_ref, o32_ref, o16_ref):
    mix = (jnp.dot(yr_ref[...], wr_ref[...], preferred_element_type=F32)
           + jnp.dot(yd_ref[...], wd_ref[...], preferred_element_type=F32))
    y = _layer_norm(DEEPNORM_ALPHA * x_ref[...] + mix, gain_ref[...], bias_ref[...])
    o32_ref[...] = y
    o16_ref[...] = y.astype(BF16)


def _outproj(y_rwkv, y_diff, w_r, w_d, x, gain, bias, *, bm):
    s = x.shape[0]
    const = lambda shape: pl.BlockSpec(shape, lambda i: (0, 0))
    return pl.pallas_call(
        _outproj_kernel,
        grid=(s // bm,),
        in_specs=[pl.BlockSpec((bm, RWKV_WIDTH), lambda i: (i, 0)),
                  pl.BlockSpec((bm, DIFF_WIDTH), lambda i: (i, 0)),
                  const((RWKV_WIDTH, D_MODEL)), const((DIFF_WIDTH, D_MODEL)),
                  pl.BlockSpec((bm, D_MODEL), lambda i: (i, 0)),
                  const((1, D_MODEL)), const((1, D_MODEL))],
        out_specs=[pl.BlockSpec((bm, D_MODEL), lambda i: (i, 0))] * 2,
        out_shape=[jax.ShapeDtypeStruct((s, D_MODEL), F32), jax.ShapeDtypeStruct((s, D_MODEL), BF16)],
        compiler_params=_cparams(1),
        name="outproj_ln",
    )(y_rwkv, y_diff, w_r, w_d, x, gain.reshape(1, -1), bias.reshape(1, -1))


def _ff2_kernel(h_ref, w_ref, x_ref, gain_ref, bias_ref, o_ref, acc_ref):
    kk = pl.program_id(1)

    @pl.when(kk == 0)
    def _():
        acc_ref[...] = jnp.zeros_like(acc_ref)

    acc_ref[...] += jnp.dot(h_ref[...], w_ref[...], preferred_element_type=F32)

    @pl.when(kk == pl.num_programs(1) - 1)
    def _():
        o_ref[...] = _layer_norm(DEEPNORM_ALPHA * x_ref[...] + acc_ref[...],
                                 gain_ref[...], bias_ref[...])


def _ff2(h, w, x, gain, bias, *, bm, bk):
    s, kdim = h.shape
    const = lambda shape: pl.BlockSpec(shape, lambda i, k: (0, 0))
    return pl.pallas_call(
        _ff2_kernel,
        grid=(s // bm, kdim // bk),
        in_specs=[pl.BlockSpec((bm, bk), lambda i, k: (i, k)),
                  pl.BlockSpec((bk, D_MODEL), lambda i, k: (k, 0)),
                  pl.BlockSpec((bm, D_MODEL), lambda i, k: (i, 0)),
                  const((1, D_MODEL)), const((1, D_MODEL))],
        out_specs=pl.BlockSpec((bm, D_MODEL), lambda i, k: (i, 0)),
        out_shape=jax.ShapeDtypeStruct((s, D_MODEL), F32),
        scratch_shapes=[pltpu.VMEM((bm, D_MODEL), F32)],
        compiler_params=_cparams(2),
        name="ff2_ln",
    )(h, w, x, gain.reshape(1, -1), bias.reshape(1, -1))


def _pad_rows(w, rows):
    return jnp.pad(w, ((0, rows - w.shape[0]), (0, 0)))


def _layer(x, positions, w_in, mu_shift, w0, w_up, a0, a_up, g_up, k_k, k_a, r_k, gn_gain, gn_bias,
           lq1, lk1, lq2, lk2, subln_gain, w_out, ln1_gain, ln1_bias, w_ff1, w_ff2, ln2_gain, ln2_bias,
           *, bm, prep_tb, nc, rope_tb, bq):
    w3 = 3 * RWKV_WIDTH
    rwkv_cols = w3 + DECAY_LORA + AAA_LORA + GATE_LORA

    def pad_cols(a, lo, hi, width):
        return jnp.pad(a[..., lo:hi], [(0, 0)] * (a.ndim - 1) + [(0, width - (hi - lo))])

    def rwkv_layout(a):
        return jnp.concatenate([
            a[..., :w3],
            pad_cols(a, w3, w3 + DECAY_LORA, DECAY_PAD),
            pad_cols(a, w3 + DECAY_LORA, w3 + DECAY_LORA + AAA_LORA, AAA_PAD),
            pad_cols(a, w3 + DECAY_LORA + AAA_LORA, rwkv_cols, GATE_PAD)], axis=-1)

    xb = x.astype(BF16)
    w_r = rwkv_layout(w_in[:, :rwkv_cols]).astype(BF16)
    w_qk = w_in[:, rwkv_cols:rwkv_cols + 2 * DIFF_WIDTH].astype(BF16)
    w_vt = w_in[:, rwkv_cols + 2 * DIFF_WIDTH:].T.astype(BF16)
    proj_r = _matmul(xb, w_r, bm=bm, bn=512, out_dtype=F32, name="proj_rwkv")
    proj_qk = _matmul(xb, w_qk, bm=bm, bn=512, out_dtype=F32, name="proj_qk")
    vt = _matmul_nt(w_vt, xb, bn=512, out_dtype=BF16, name="proj_vt")

    at, bt, kt, rt, v_r, wc, bonus, gate = _rwkv_prep(
        proj_r, rwkv_layout(mu_shift), w0,
        _pad_rows(w_up, DECAY_PAD).astype(BF16), a0, _pad_rows(a_up, AAA_PAD).astype(BF16),
        _pad_rows(g_up, GATE_PAD).astype(BF16), k_k, k_a, r_k.reshape(-1), tb=prep_tb)
    y_rwkv = _rwkv_chunks(at, bt, kt, rt, v_r, wc, bonus, gate, gn_gain, gn_bias, nc=nc)

    q, k = _rope(proj_qk, positions, tb=rope_tb)
    lam_vecs = jnp.stack([lq1, lk1, lq2, lk2]).astype(F32)
    y_diff = _attention(q, k, vt, lam_vecs, subln_gain, bq=bq)

    wo = w_out.astype(BF16)
    x1, x1b = _outproj(y_rwkv, y_diff, wo[:RWKV_WIDTH], wo[RWKV_WIDTH:], x, ln1_gain, ln1_bias, bm=256)
    h = _matmul(x1b, w_ff1.astype(BF16), bm=bm, bn=512, out_dtype=BF16, act="relu2", name="ff1")
    return _ff2(h, w_ff2.astype(BF16), x1, ln2_gain, ln2_bias, bm=512, bk=1024)


def kernel(x, positions, w_in, mu_shift, rwkv_w0, rwkv_w_up, rwkv_a0, rwkv_a_up, rwkv_g_up, rwkv_k_k,
           rwkv_k_a, rwkv_r_k, rwkv_gn_gain, rwkv_gn_bias, diff_lambda_q1, diff_lambda_k1,
           diff_lambda_q2, diff_lambda_k2, diff_subln_gain, w_out, ln1_gain, ln1_bias, w_ff1, w_ff2,
           ln2_gain, ln2_bias):
    batch, seq, _ = x.shape
    assert batch == 1 and w_in.shape[0] == 1
    out = _layer(
        x[0], positions[0], w_in[0], mu_shift[0], rwkv_w0[0], rwkv_w_up[0], rwkv_a0[0], rwkv_a_up[0],
        rwkv_g_up[0], rwkv_k_k[0], rwkv_k_a[0], rwkv_r_k[0], rwkv_gn_gain[0], rwkv_gn_bias[0],
        diff_lambda_q1[0], diff_lambda_k1[0], diff_lambda_q2[0], diff_lambda_k2[0],
        diff_subln_gain[0], w_out[0], ln1_gain[0], ln1_bias[0], w_ff1[0], w_ff2[0], ln2_gain[0],
        ln2_bias[0], bm=1024, prep_tb=256, nc=2, rope_tb=512, bq=512)
    return out[None]
```

```python
import functools
import math

import jax
import jax.numpy as jnp
from jax import lax
from jax.experimental import pallas as pl
from jax.experimental.pallas import tpu as pltpu

F32 = jnp.float32
BF16 = jnp.bfloat16

D_MODEL = 2048
RWKV_WIDTH = 1024
HEAD_DIM = 64
DECAY_LORA = 64
AAA_LORA = 64
GATE_LORA = 160
DIFF_WIDTH = 1024
DIFF_V_DIM = 128
DIFF_HEADS = 8
D_FF = 8192
ROPE_THETA = 10000.0
LN_EPS = 1e-5
RWKV_GN_EPS = 64e-5
SUBLN_EPS = 1e-5
DEEPNORM_ALPHA = 2.0 ** 0.25
LAMBDA_INIT = 0.8 - 0.6 * math.exp(0.0)
LOG2_E = math.log2(math.e)

LANES = 128
SUBLANES = 8
VMEM_LIMIT_BYTES = 56 * 1024 * 1024

DECAY_PAD = 128
AAA_PAD = 128
GATE_PAD = 256
RWKV_PCOLS = 3 * RWKV_WIDTH + DECAY_PAD + AAA_PAD + GATE_PAD
CHUNK = 64
GROUP_HEADS = 4
GROUP_W = GROUP_HEADS * HEAD_DIM


def _cparams(n_axes):
    return pltpu.CompilerParams(
        dimension_semantics=("arbitrary",) * n_axes,
        vmem_limit_bytes=VMEM_LIMIT_BYTES,
    )


def _dot(a, b):
    return jnp.dot(a.astype(BF16), b.astype(BF16), preferred_element_type=F32)


def _dot_nt(a, b):
    return lax.dot_general(a.astype(BF16), b.astype(BF16), (((1,), (1,)), ((), ())),
                           preferred_element_type=F32)


def _dot_tn(a, b):
    return lax.dot_general(a.astype(BF16), b.astype(BF16), (((0,), (0,)), ((), ())),
                           preferred_element_type=F32)


def _split3(x):
    h1 = x.astype(BF16)
    r1 = x - h1.astype(F32)
    h2 = r1.astype(BF16)
    h3 = (r1 - h2.astype(F32)).astype(BF16)
    return h1, h2, h3


def _split2(x):
    h1 = x.astype(BF16)
    h2 = (x - h1.astype(F32)).astype(BF16)
    return h1, h2


def _mm_kernel(a_ref, b_ref, o_ref, *, act):
    acc = jnp.dot(a_ref[...], b_ref[...], preferred_element_type=F32)
    if act == "relu2":
        acc = jnp.square(jnp.maximum(acc, 0.0))
    o_ref[...] = acc.astype(o_ref.dtype)


def _matmul(a, b, *, bm, bn, out_dtype, act=None, name):
    m, k = a.shape
    _, n = b.shape
    assert m % bm == 0 and n % bn == 0
    return pl.pallas_call(
        functools.partial(_mm_kernel, act=act),
        grid=(m // bm, n // bn),
        in_specs=[pl.BlockSpec((bm, k), lambda i, j: (i, 0)),
                  pl.BlockSpec((k, bn), lambda i, j: (0, j))],
        out_specs=pl.BlockSpec((bm, bn), lambda i, j: (i, j)),
        out_shape=jax.ShapeDtypeStruct((m, n), out_dtype),
        compiler_params=_cparams(2),
        name=name,
    )(a, b)


def _mm_nt_kernel(a_ref, b_ref, o_ref):
    o_ref[...] = lax.dot_general(a_ref[...], b_ref[...], (((1,), (1,)), ((), ())),
                                 preferred_element_type=F32).astype(o_ref.dtype)


def _matmul_nt(a, b, *, bn, out_dtype, name):
    m, k = a.shape
    n, _ = b.shape
    assert n % bn == 0
    return pl.pallas_call(
        _mm_nt_kernel,
        grid=(n // bn,),
        in_specs=[pl.BlockSpec((m, k), lambda j: (0, 0)),
                  pl.BlockSpec((bn, k), lambda j: (j, 0))],
        out_specs=pl.BlockSpec((m, bn), lambda j: (0, j)),
        out_shape=jax.ShapeDtypeStruct((m, n), out_dtype),
        compiler_params=_cparams(1),
        name=name,
    )(a, b)


def _rwkv_prep_kernel(p_ref, mu_ref, w0_ref, wup_ref, a0_ref, aup_ref, gup_ref, kk_ref, ka_ref,
                      rk_ref, segin_ref, segout_ref, tri_ref,
                      at_ref, bt_ref, kt_ref, rt_ref, v_ref, wc_ref, bonus_ref, g_ref,
                      carry_ref, *, tb):
    i = pl.program_id(0)

    @pl.when(i == 0)
    def _():
        carry_ref[...] = jnp.zeros_like(carry_ref)

    def mixed(lo, hi):
        p = p_ref[:, lo:hi]
        prev_last = carry_ref[0:1, lo:hi]
        shifted = pltpu.roll(p, 1, axis=0)
        row = lax.broadcasted_iota(jnp.int32, p.shape, 0)
        prev = jnp.where(row == 0, prev_last, shifted)
        carry_ref[0:1, lo:hi] = p[tb - 1:tb, :]
        return p + (prev - p) * mu_ref[:, lo:hi]

    w3 = 3 * RWKV_WIDTH
    r = mixed(0, RWKV_WIDTH)
    k = mixed(RWKV_WIDTH, 2 * RWKV_WIDTH)
    v = mixed(2 * RWKV_WIDTH, w3)
    w_d = mixed(w3, w3 + DECAY_PAD)
    a_d = mixed(w3 + DECAY_PAD, w3 + DECAY_PAD + AAA_PAD)
    g_d = mixed(w3 + DECAY_PAD + AAA_PAD, RWKV_PCOLS)

    z = w0_ref[...] + _dot(jnp.tanh(w_d), wup_ref[...])
    nz = -z
    softplus = jnp.maximum(nz, 0.0) + jnp.log1p(jnp.exp(-jnp.abs(nz)))
    lw = -jnp.exp(-softplus - 0.5)
    a = jax.nn.sigmoid(a0_ref[...] + _dot(a_d, aup_ref[...]))
    g_ref[...] = _dot(jax.nn.sigmoid(g_d), gup_ref[...])

    def segsum(x):
        h1, h2 = _split2(jnp.dot(x.astype(BF16), segin_ref[...], preferred_element_type=F32))
        return (jnp.dot(h1, segout_ref[...], preferred_element_type=F32)
                + jnp.dot(h2, segout_ref[...], preferred_element_type=F32))

    kk0 = k * kk_ref[...]
    kk = kk0 / jnp.maximum(jnp.sqrt(segsum(kk0 * kk0)), 1e-12)
    k2 = k * (1.0 + (a - 1.0) * ka_ref[...])
    bonus_ref[...] = segsum(r * k2 * rk_ref[...]) * v

    tri = tri_ref[...]
    l1, l2, l3 = _split3(lw)
    cs = (jnp.dot(tri, l1, preferred_element_type=F32)
          + jnp.dot(tri, l2, preferred_element_type=F32)
          + jnp.dot(tri, l3, preferred_element_type=F32))
    cl = cs[:tb]
    tot = cs[tb:]
    e_neg = jnp.exp(-cl)
    at_ref[...] = (-kk * jnp.exp(cl - lw)).astype(BF16)
    bt_ref[...] = (kk * a * e_neg).astype(BF16)
    kt_ref[...] = (k2 * e_neg).astype(BF16)
    rt_ref[...] = (r * jnp.exp(cl)).astype(BF16)
    v_ref[...] = v.astype(BF16)
    for c in range(tb // CHUNK):
        wc_ref[c] = jnp.exp(tot[c * CHUNK:c * CHUNK + 1, :])


def _rwkv_prep(proj_r, mu, w0, wup, a0, aup, gup, k_k, k_a, r_k, *, tb):
    s = proj_r.shape[0]
    assert s % tb == 0 and tb % CHUNK == 0
    w = RWKV_WIDTH
    lane = jnp.arange(w)
    seg_in = (lane[:, None] // HEAD_DIM == jnp.arange(LANES)[None, :]).astype(BF16)
    t = jnp.arange(tb)
    same = t[:, None] // CHUNK == t[None, :] // CHUNK
    tri = jnp.concatenate([(same & (t[:, None] >= t[None, :])), same], axis=0).astype(BF16)
    row = lambda x: x.reshape(1, -1).astype(F32)
    const = lambda shape: pl.BlockSpec(shape, lambda i: (0,) * len(shape))
    tile = lambda width: pl.BlockSpec((tb, width), lambda i: (i, 0))
    outs = pl.pallas_call(
        functools.partial(_rwkv_prep_kernel, tb=tb),
        grid=(s // tb,),
        in_specs=[tile(RWKV_PCOLS), const((1, RWKV_PCOLS)), const((1, w)), const((DECAY_PAD, w)),
                  const((1, w)), const((AAA_PAD, w)), const((GATE_PAD, w)), const((1, w)),
                  const((1, w)), const((1, w)), const((w, LANES)), const((LANES, w)),
                  const((2 * tb, tb))],
        out_specs=[tile(w), tile(w), tile(w), tile(w), tile(w),
                   pl.BlockSpec((tb // CHUNK, 1, w), lambda i: (i, 0, 0)),
                   tile(w), tile(w)],
        out_shape=[jax.ShapeDtypeStruct((s, w), BF16)] * 5
        + [jax.ShapeDtypeStruct((s // CHUNK, 1, w), F32),
           jax.ShapeDtypeStruct((s, w), F32), jax.ShapeDtypeStruct((s, w), F32)],
        scratch_shapes=[pltpu.VMEM((SUBLANES, RWKV_PCOLS), F32)],
        compiler_params=_cparams(1),
        name="rwkv_prep",
    )(proj_r, row(mu), row(w0), wup, row(a0), aup, gup, row(k_k), row(k_a), row(r_k), seg_in, seg_in.T, tri)
    return outs


def _bdot(a, b):
    return jnp.dot(a, b, preferred_element_type=F32)


def _bdot_nt(a, b):
    return lax.dot_general(a, b, (((1,), (1,)), ((), ())), preferred_element_type=F32)


def _bdot_tn(a, b):
    return lax.dot_general(a, b, (((0,), (0,)), ((), ())), preferred_element_type=F32)


(MASK_STRICT, MASK_INCL, MASK_EYE, MASK_BLK8, MASK_OFF16, MASK_OFF32, MASK_OFF64,
 MASK_HEAD) = range(8)


def _chunk_masks():
    n = GROUP_W
    r = jnp.arange(n)[:, None]
    c = jnp.arange(n)[None, :]
    head = (r // CHUNK) == (c // CHUNK)
    t_r, t_c = r % CHUNK, c % CHUNK
    blk = lambda size: (r // size) == (c // size)
    masks = [head & (t_r > t_c), head & (t_r >= t_c), r == c, blk(8),
             blk(16) & ~blk(8), blk(32) & ~blk(16), blk(64) & ~blk(32), head]
    return jnp.stack(masks).astype(F32)


def _rwkv_chunk_kernel(at_ref, bt_ref, kt_ref, rt_ref, v_ref, wc_ref, bonus_ref, g_ref,
                       gain_ref, bias_ref, mask_ref, o_ref, state_ref, *, nc):
    @pl.when(pl.program_id(0) == 0)
    def _():
        state_ref[...] = jnp.zeros_like(state_ref)

    n = GROUP_W
    ng = RWKV_WIDTH // GROUP_W
    items = [(c, g) for c in range(nc) for g in range(ng)]
    lane_head = lax.broadcasted_iota(jnp.int32, (CHUNK, n), 1) >> 6

    def tile(ref, c, g):
        return ref[c * CHUNK:(c + 1) * CHUNK, g * n:(g + 1) * n]

    def expand(x):
        zero = jnp.zeros_like(x)
        return jnp.concatenate([jnp.where(lane_head == h, x, zero) for h in range(GROUP_HEADS)],
                               axis=0)

    def repeat(x):
        return jnp.concatenate([x] * GROUP_HEADS, axis=0)

    mask = lambda k: mask_ref[k]
    bf = lambda x: x.astype(BF16)
    each = lambda f: [f(*it) for it in items]
    zipped = lambda f, *lists: [f(*args) for args in zip(*lists)]

    at = each(lambda c, g: tile(at_ref, c, g))
    bt = each(lambda c, g: tile(bt_ref, c, g))
    kt = each(lambda c, g: tile(kt_ref, c, g))
    wc = each(lambda c, g: wc_ref[c][:, g * n:(g + 1) * n])
    ae = [expand(x) for x in at]
    be = [expand(x) for x in bt]
    ke = [expand(x) for x in kt]
    ve = each(lambda c, g: expand(tile(v_ref, c, g)))
    at4 = [repeat(x) for x in at]
    rt4 = each(lambda c, g: repeat(tile(rt_ref, c, g)))
    bhe = zipped(lambda x, w: expand(bf(x.astype(F32) * w)), bt, wc)
    khe = zipped(lambda x, w: expand(bf(x.astype(F32) * w)), kt, wc)

    lab = zipped(lambda a, b: _bdot_nt(a, b) * mask(MASK_STRICT), at4, be)
    lak = zipped(lambda a, b: bf(_bdot_nt(a, b) * mask(MASK_STRICT)), at4, ke)
    arb = zipped(lambda a, b: bf(_bdot_nt(a, b) * mask(MASK_INCL)), rt4, be)
    ark = zipped(lambda a, b: bf(_bdot_nt(a, b) * mask(MASK_INCL)), rt4, ke)
    lv = zipped(lambda a, b: bf(_bdot(a, b)), lak, ve)

    l8 = [x * mask(MASK_BLK8) for x in lab]
    minv = [mask(MASK_EYE) + x for x in l8]
    pw = [bf(x) for x in l8]
    for _ in range(2):
        pw2 = [_bdot(x, x) for x in pw]
        pw = [bf(x) for x in pw2]
        minv = zipped(lambda p, m: m + _bdot(p, bf(m)), pw, minv)
    for k in (MASK_OFF16, MASK_OFF32, MASK_OFF64):
        off = [bf(x * mask(k)) for x in lab]
        mb = [bf(m) for m in minv]
        tmp = zipped(lambda m, o: bf(_bdot(m, o)), mb, off)
        minv = zipped(lambda m, t, m16: m + _bdot(t, m16), minv, tmp, mb)
    mb = [bf(m) for m in minv]

    p = zipped(lambda m, a: bf(_bdot(m, a)), mb, ae)
    q = zipped(lambda m, x: bf(_bdot(m, x)), mb, lv)
    g_mat = zipped(lambda r4, a, x: bf(r4.astype(F32) * mask(MASK_HEAD) + _bdot(a, x)), rt4, arb, p)
    y0 = zipped(lambda a, x, b, v: _bdot(a, x) + _bdot(b, v), arb, q, ark, ve)
    phi = zipped(lambda w, b, x: bf(mask(MASK_EYE) * w + _bdot_tn(b, x)), wc, bhe, p)
    psi = zipped(lambda b, x, k, v: _bdot_tn(b, x) + _bdot_tn(k, v), bhe, q, khe, ve)

    state = [state_ref[g] for g in range(ng)]
    for c in range(nc):
        idx = [c * ng + g for g in range(ng)]
        sb = [bf(s) for s in state]
        ys = [_bdot(g_mat[i], sb[g]) + y0[i] for g, i in enumerate(idx)]
        state = [_bdot(phi[i], sb[g]) + psi[i] for g, i in enumerate(idx)]
        for g, y in enumerate(ys):
            mean = jnp.sum(y, axis=1, keepdims=True) * (1.0 / HEAD_DIM)
            yc = (y - mean) * mask(MASK_HEAD)
            var = jnp.sum(yc * yc, axis=1, keepdims=True) * (1.0 / HEAD_DIM)
            yn = yc * lax.rsqrt(var + RWKV_GN_EPS)
            yw = (yn[0:CHUNK] + yn[CHUNK:2 * CHUNK] + yn[2 * CHUNK:3 * CHUNK]
                  + yn[3 * CHUNK:4 * CHUNK])
            rows = slice(c * CHUNK, (c + 1) * CHUNK)
            cols = slice(g * n, (g + 1) * n)
            out = (yw * gain_ref[:, cols] + bias_ref[:, cols] + bonus_ref[rows, cols]) * g_ref[rows, cols]
            o_ref[rows, cols] = out.astype(o_ref.dtype)
    for g in range(ng):
        state_ref[g] = state[g]


def _rwkv_chunks(at, bt, kt, rt, v, wc, bonus, g, gn_gain, gn_bias, *, nc):
    s, w = at.shape
    tb = nc * CHUNK
    assert s % tb == 0 and w % GROUP_W == 0
    tile = pl.BlockSpec((tb, w), lambda ti: (ti, 0))
    rowspec = pl.BlockSpec((1, w), lambda ti: (0, 0))
    masks = _chunk_masks()
    return pl.pallas_call(
        functools.partial(_rwkv_chunk_kernel, nc=nc),
        grid=(s // tb,),
        in_specs=[tile, tile, tile, tile, tile,
                  pl.BlockSpec((nc, 1, w), lambda ti: (ti, 0, 0)),
                  tile, tile, rowspec, rowspec,
                  pl.BlockSpec(masks.shape, lambda ti: (0, 0, 0))],
        out_specs=tile,
        out_shape=jax.ShapeDtypeStruct((s, w), BF16),
        scratch_shapes=[pltpu.VMEM((w // GROUP_W, GROUP_W, GROUP_W), F32)],
        compiler_params=_cparams(1),
        name="rwkv_chunks",
    )(at, bt, kt, rt, v, wc, bonus, g, gn_gain.reshape(1, -1).astype(F32),
      gn_bias.reshape(1, -1).astype(F32), masks)


def _proj_rope_kernel(pos_ref, invf_ref, x_ref, w_ref, o_ref, cos_ref, sin_ref, *, bm, bn):
    lane = lax.broadcasted_iota(jnp.int32, (bm, LANES), 1)
    first_half = (lane & (HEAD_DIM - 1)) < (HEAD_DIM // 2)

    @pl.when(pl.program_id(1) == 0)
    def _():
        ang = pos_ref[...].astype(F32) * invf_ref[...]
        cos_ref[...] = jnp.cos(ang)
        sin = jnp.sin(ang)
        sin_ref[...] = jnp.where(first_half, -sin, sin)

    acc = jnp.dot(x_ref[...], w_ref[...], preferred_element_type=F32)
    cos = cos_ref[...]
    sin = sin_ref[...]
    for c in range(bn // LANES):
        t = acc[:, c * LANES:(c + 1) * LANES]
        partner = jnp.where(first_half, pltpu.roll(t, LANES - HEAD_DIM // 2, axis=1),
                            pltpu.roll(t, HEAD_DIM // 2, axis=1))
        o_ref[:, c * LANES:(c + 1) * LANES] = (t * cos + partner * sin).astype(o_ref.dtype)


def _proj_rope(xb, w_qk, positions, *, bm, bn):
    s, k = xb.shape
    n = w_qk.shape[1]
    half = jnp.arange(0, HEAD_DIM, 2, dtype=F32)
    inv_freq = ROPE_THETA ** (-half / HEAD_DIM)
    invf = jnp.tile(inv_freq, 2 * LANES // HEAD_DIM).reshape(1, LANES)
    return pl.pallas_call(
        functools.partial(_proj_rope_kernel, bm=bm, bn=bn),
        grid=(s // bm, n // bn),
        in_specs=[pl.BlockSpec((bm, 1), lambda i, j: (i, 0)),
                  pl.BlockSpec((1, LANES), lambda i, j: (0, 0)),
                  pl.BlockSpec((bm, k), lambda i, j: (i, 0)),
                  pl.BlockSpec((k, bn), lambda i, j: (0, j))],
        out_specs=pl.BlockSpec((bm, bn), lambda i, j: (i, j)),
        out_shape=jax.ShapeDtypeStruct((s, n), BF16),
        scratch_shapes=[pltpu.VMEM((bm, LANES), F32), pltpu.VMEM((bm, LANES), F32)],
        compiler_params=_cparams(2),
        name="proj_qk_rope",
    )(positions.reshape(s, 1), invf, xb, w_qk)


ATT_STRIP = 128


def _attn_kernel(lam_ref, gain_ref, q_ref, k_ref, vt_ref, o_ref, acc_ref, s0_ref, s1_ref, *, bq):
    i = pl.program_id(1)
    nq = 2 * bq
    q = q_ref[...]
    lane = lax.broadcasted_iota(jnp.int32, q.shape, 1)
    zero = jnp.zeros_like(q)
    q_all = jnp.concatenate([jnp.where(lane < HEAD_DIM, q, zero),
                             jnp.where(lane >= HEAD_DIM, q, zero)], axis=0)

    s_slots = (s0_ref, s1_ref)

    def scores(j, slot):
        start = pl.multiple_of(j * bq, bq)
        s_slots[slot][...] = lax.dot_general(k_ref[pl.ds(start, bq), :], q_all,
                                             (((1,), (1,)), ((), ())), preferred_element_type=F32)

    def consume(j, slot, m, l, masked):
        start = pl.multiple_of(j * bq, bq)
        vtb = vt_ref[:, pl.ds(start, bq)]
        p_strips, m_new, l_new, alphas = [], [], [], []
        for c in range(nq // ATT_STRIP):
            sl = slice(c * ATT_STRIP, (c + 1) * ATT_STRIP)
            s = s_slots[slot][:, sl]
            if masked:
                key = lax.broadcasted_iota(jnp.int32, s.shape, 0)
                qry = lax.broadcasted_iota(jnp.int32, s.shape, 1) + (c * ATT_STRIP) % bq
                s = jnp.where(key <= qry, s, -1e30)
            mn = jnp.maximum(m[:, sl], jnp.max(s, axis=0, keepdims=True))
            p = jnp.exp2(s - mn)
            alpha = jnp.exp2(m[:, sl] - mn)
            m_new.append(mn)
            l_new.append(alpha * l[:, sl] + jnp.sum(p, axis=0, keepdims=True))
            alphas.append(alpha)
            p_strips.append(p.astype(BF16))
        p_all = jnp.concatenate(p_strips, axis=1)
        alpha = jnp.concatenate(alphas, axis=1)
        acc_ref[...] = alpha * acc_ref[...] + jnp.dot(vtb, p_all, preferred_element_type=F32)
        return jnp.concatenate(m_new, axis=1), jnp.concatenate(l_new, axis=1)

    def pair(jj, carry):
        j = 2 * jj
        scores(j + 1, 1)
        m, l = consume(j, 0, carry[0], carry[1], False)
        scores(j + 2, 0)
        return consume(j + 1, 1, m, l, False)

    def tail_even(carry):
        return consume(i, 0, carry[0], carry[1], True)

    def tail_odd(carry):
        scores(i, 1)
        m, l = consume(i - 1, 0, carry[0], carry[1], False)
        return consume(i, 1, m, l, True)

    acc_ref[...] = jnp.zeros_like(acc_ref)
    scores(0, 0)
    init = (jnp.full((1, nq), -1e30, F32), jnp.zeros((1, nq), F32))
    carry = lax.fori_loop(0, i // 2, pair, init)
    m, l = lax.cond((i & 1) == 0, tail_even, tail_odd, carry)

    lam_v = lam_ref[...]
    lam = (jnp.exp(jnp.sum(lam_v[0:1] * lam_v[1:2], axis=1, keepdims=True))
           - jnp.exp(jnp.sum(lam_v[2:3] * lam_v[3:4], axis=1, keepdims=True)) + LAMBDA_INIT)
    o = acc_ref[...] / l
    out = o[:, :bq] - lam * o[:, bq:]
    out = out * lax.rsqrt(jnp.mean(jnp.square(out), axis=0, keepdims=True) + SUBLN_EPS)
    out = out * gain_ref[...] * (1.0 - LAMBDA_INIT)
    o_ref[...] = out.T.astype(o_ref.dtype)


def _attention(qk, vt, lam_vecs, subln_gain, *, bq):
    s = qk.shape[0]
    assert s % bq == 0 and bq % ATT_STRIP == 0
    return pl.pallas_call(
        functools.partial(_attn_kernel, bq=bq),
        grid=(DIFF_HEADS, s // bq),
        in_specs=[pl.BlockSpec((4, HEAD_DIM), lambda h, i: (0, 0)),
                  pl.BlockSpec((DIFF_V_DIM, 1), lambda h, i: (0, 0)),
                  pl.BlockSpec((bq, DIFF_V_DIM), lambda h, i: (i, h)),
                  pl.BlockSpec((s, DIFF_V_DIM), lambda h, i: (0, DIFF_HEADS + h)),
                  pl.BlockSpec((DIFF_V_DIM, s), lambda h, i: (h, 0))],
        out_specs=pl.BlockSpec((bq, DIFF_V_DIM), lambda h, i: (i, h)),
        out_shape=jax.ShapeDtypeStruct((s, DIFF_WIDTH), BF16),
        scratch_shapes=[pltpu.VMEM((DIFF_V_DIM, 2 * bq), F32), pltpu.VMEM((bq, 2 * bq), F32),
                        pltpu.VMEM((bq, 2 * bq), F32)],
        compiler_params=_cparams(2),
        name="diff_attn",
    )(lam_vecs, subln_gain.reshape(-1, 1).astype(F32), qk, qk, vt)


def _layer_norm(z, gain, bias):
    mean = jnp.mean(z, axis=-1, keepdims=True)
    zc = z - mean
    var = jnp.mean(zc * zc, axis=-1, keepdims=True)
    return zc * lax.rsqrt(var + LN_EPS) * gain + bias


def _outproj_kernel(yr_ref, yd_ref, wr_ref, wd_ref, x_ref, gain_ref, bias_ref, o32_ref, o16_ref):
    mix = (jnp.dot(yr_ref[...], wr_ref[...], preferred_element_type=F32)
           + jnp.dot(yd_ref[...], wd_ref[...], preferred_element_type=F32))
    y = _layer_norm(DEEPNORM_ALPHA * x_ref[...] + mix, gain_ref[...], bias_ref[...])
    o32_ref[...] = y
    o16_ref[...] = y.astype(BF16)


def _outproj(y_rwkv, y_diff, w_r, w_d, x, gain, bias, *, bm):
    s = x.shape[0]
    const = lambda shape: pl.BlockSpec(shape, lambda i: (0, 0))
    return pl.pallas_call(
        _outproj_kernel,
        grid=(s // bm,),
        in_specs=[pl.BlockSpec((bm, RWKV_WIDTH), lambda i: (i, 0)),
                  pl.BlockSpec((bm, DIFF_WIDTH), lambda i: (i, 0)),
                  const((RWKV_WIDTH, D_MODEL)), const((DIFF_WIDTH, D_MODEL)),
                  pl.BlockSpec((bm, D_MODEL), lambda i: (i, 0)),
                  const((1, D_MODEL)), const((1, D_MODEL))],
        out_specs=[pl.BlockSpec((bm, D_MODEL), lambda i: (i, 0))] * 2,
        out_shape=[jax.ShapeDtypeStruct((s, D_MODEL), F32), jax.ShapeDtypeStruct((s, D_MODEL), BF16)],
        compiler_params=_cparams(1),
        name="outproj_ln",
    )(y_rwkv, y_diff, w_r, w_d, x, gain.reshape(1, -1), bias.reshape(1, -1))


def _ff2_kernel(h_ref, w_ref, x_ref, gain_ref, bias_ref, o_ref, acc_ref):
    kk = pl.program_id(1)

    @pl.when(kk == 0)
    def _():
        acc_ref[...] = jnp.zeros_like(acc_ref)

    acc_ref[...] += jnp.dot(h_ref[...], w_ref[...], preferred_element_type=F32)

    @pl.when(kk == pl.num_programs(1) - 1)
    def _():
        o_ref[...] = _layer_norm(DEEPNORM_ALPHA * x_ref[...] + acc_ref[...],
                                 gain_ref[...], bias_ref[...])


def _ff2(h, w, x, gain, bias, *, bm, bk):
    s, kdim = h.shape
    const = lambda shape: pl.BlockSpec(shape, lambda i, k: (0, 0))
    return pl.pallas_call(
        _ff2_kernel,
        grid=(s // bm, kdim // bk),
        in_specs=[pl.BlockSpec((bm, bk), lambda i, k: (i, k)),
                  pl.BlockSpec((bk, D_MODEL), lambda i, k: (k, 0)),
                  pl.BlockSpec((bm, D_MODEL), lambda i, k: (i, 0)),
                  const((1, D_MODEL)), const((1, D_MODEL))],
        out_specs=pl.BlockSpec((bm, D_MODEL), lambda i, k: (i, 0)),
        out_shape=jax.ShapeDtypeStruct((s, D_MODEL), F32),
        scratch_shapes=[pltpu.VMEM((bm, D_MODEL), F32)],
        compiler_params=_cparams(2),
        name="ff2_ln",
    )(h, w, x, gain.reshape(1, -1), bias.reshape(1, -1))


def _pad_rows(w, rows):
    return jnp.pad(w, ((0, rows - w.shape[0]), (0, 0)))


def _layer(x, positions, w_in, mu_shift, w0, w_up, a0, a_up, g_up, k_k, k_a, r_k, gn_gain, gn_bias,
           lq1, lk1, lq2, lk2, subln_gain, w_out, ln1_gain, ln1_bias, w_ff1, w_ff2, ln2_gain, ln2_bias,
           *, bm, prep_tb, nc, bq):
    w3 = 3 * RWKV_WIDTH
    rwkv_cols = w3 + DECAY_LORA + AAA_LORA + GATE_LORA

    def pad_cols(a, lo, hi, width):
        return jnp.pad(a[..., lo:hi], [(0, 0)] * (a.ndim - 1) + [(0, width - (hi - lo))])

    def rwkv_layout(a):
        return jnp.concatenate([
            a[..., :w3],
            pad_cols(a, w3, w3 + DECAY_LORA, DECAY_PAD),
            pad_cols(a, w3 + DECAY_LORA, w3 + DECAY_LORA + AAA_LORA, AAA_PAD),
            pad_cols(a, w3 + DECAY_LORA + AAA_LORA, rwkv_cols, GATE_PAD)], axis=-1)

    xb = x.astype(BF16)
    w_r = rwkv_layout(w_in[:, :rwkv_cols]).astype(BF16)
    q_scale = jnp.where(jnp.arange(2 * DIFF_WIDTH) < DIFF_WIDTH, HEAD_DIM ** -0.5 * LOG2_E, 1.0)
    w_qk = (w_in[:, rwkv_cols:rwkv_cols + 2 * DIFF_WIDTH] * q_scale).astype(BF16)
    w_vt = w_in[:, rwkv_cols + 2 * DIFF_WIDTH:].T.astype(BF16)
    proj_r = _matmul(xb, w_r, bm=bm, bn=512, out_dtype=F32, name="proj_rwkv")
    vt = _matmul_nt(w_vt, xb, bn=512, out_dtype=BF16, name="proj_vt")

    at, bt, kt, rt, v_r, wc, bonus, gate = _rwkv_prep(
        proj_r, rwkv_layout(mu_shift), w0,
        _pad_rows(w_up, DECAY_PAD).astype(BF16), a0, _pad_rows(a_up, AAA_PAD).astype(BF16),
        _pad_rows(g_up, GATE_PAD).astype(BF16), k_k, k_a, r_k.reshape(-1), tb=prep_tb)
    y_rwkv = _rwkv_chunks(at, bt, kt, rt, v_r, wc, bonus, gate, gn_gain, gn_bias, nc=nc)

    qk = _proj_rope(xb, w_qk, positions, bm=bm, bn=512)
    lam_vecs = jnp.stack([lq1, lk1, lq2, lk2]).astype(F32)
    y_diff = _attention(qk, vt, lam_vecs, subln_gain, bq=bq)

    wo = w_out.astype(BF16)
    x1, x1b = _outproj(y_rwkv, y_diff, wo[:RWKV_WIDTH], wo[RWKV_WIDTH:], x, ln1_gain, ln1_bias, bm=256)
    h = _matmul(x1b, w_ff1.astype(BF16), bm=bm, bn=512, out_dtype=BF16, act="relu2", name="ff1")
    return _ff2(h, w_ff2.astype(BF16), x1, ln2_gain, ln2_bias, bm=512, bk=1024)


def kernel(x, positions, w_in, mu_shift, rwkv_w0, rwkv_w_up, rwkv_a0, rwkv_a_up, rwkv_g_up, rwkv_k_k,
           rwkv_k_a, rwkv_r_k, rwkv_gn_gain, rwkv_gn_bias, diff_lambda_q1, diff_lambda_k1,
           diff_lambda_q2, diff_lambda_k2, diff_subln_gain, w_out, ln1_gain, ln1_bias, w_ff1, w_ff2,
           ln2_gain, ln2_bias):
    batch, seq, _ = x.shape
    assert batch == 1 and w_in.shape[0] == 1
    out = _layer(
        x[0], positions[0], w_in[0], mu_shift[0], rwkv_w0[0], rwkv_w_up[0], rwkv_a0[0], rwkv_a_up[0],
        rwkv_g_up[0], rwkv_k_k[0], rwkv_k_a[0], rwkv_r_k[0], rwkv_gn_gain[0], rwkv_gn_bias[0],
        diff_lambda_q1[0], diff_lambda_k1[0], diff_lambda_q2[0], diff_lambda_k2[0],
        diff_subln_gain[0], w_out[0], ln1_gain[0], ln1_bias[0], w_ff1[0], w_ff2[0], ln2_gain[0],
        ln2_bias[0], bm=1024, prep_tb=256, nc=2, bq=512)
    return out[None]
```

```python
import functools
import math

import jax
import jax.numpy as jnp
from jax import lax
from jax.experimental import pallas as pl
from jax.experimental.pallas import tpu as pltpu

F32 = jnp.float32
BF16 = jnp.bfloat16

D_MODEL = 2048
RWKV_WIDTH = 1024
HEAD_DIM = 64
DECAY_LORA = 64
AAA_LORA = 64
GATE_LORA = 160
DIFF_WIDTH = 1024
DIFF_V_DIM = 128
DIFF_HEADS = 8
D_FF = 8192
ROPE_THETA = 10000.0
LN_EPS = 1e-5
RWKV_GN_EPS = 64e-5
SUBLN_EPS = 1e-5
DEEPNORM_ALPHA = 2.0 ** 0.25
LAMBDA_INIT = 0.8 - 0.6 * math.exp(0.0)
LOG2_E = math.log2(math.e)

LANES = 128
SUBLANES = 8
VMEM_LIMIT_BYTES = 56 * 1024 * 1024

DECAY_PAD = 128
AAA_PAD = 128
GATE_PAD = 256
RWKV_PCOLS = 3 * RWKV_WIDTH + DECAY_PAD + AAA_PAD + GATE_PAD
CHUNK = 64
GROUP_HEADS = 4
GROUP_W = GROUP_HEADS * HEAD_DIM


def _cparams(n_axes):
    return pltpu.CompilerParams(
        dimension_semantics=("arbitrary",) * n_axes,
        vmem_limit_bytes=VMEM_LIMIT_BYTES,
    )


def _dot(a, b):
    return jnp.dot(a.astype(BF16), b.astype(BF16), preferred_element_type=F32)


def _dot_nt(a, b):
    return lax.dot_general(a.astype(BF16), b.astype(BF16), (((1,), (1,)), ((), ())),
                           preferred_element_type=F32)


def _dot_tn(a, b):
    return lax.dot_general(a.astype(BF16), b.astype(BF16), (((0,), (0,)), ((), ())),
                           preferred_element_type=F32)


def _split3(x):
    h1 = x.astype(BF16)
    r1 = x - h1.astype(F32)
    h2 = r1.astype(BF16)
    h3 = (r1 - h2.astype(F32)).astype(BF16)
    return h1, h2, h3


def _split2(x):
    h1 = x.astype(BF16)
    h2 = (x - h1.astype(F32)).astype(BF16)
    return h1, h2


def _mm_kernel(a_ref, b_ref, o_ref, *, transpose_out):
    acc = jnp.dot(a_ref[...], b_ref[...], preferred_element_type=F32)
    if transpose_out:
        acc = acc.T
    o_ref[...] = acc.astype(o_ref.dtype)


def _matmul(a, b, *, bm, bn, n, col0=0, out_dtype, transpose_out=False, name):
    m, k = a.shape
    assert m % bm == 0 and n % bn == 0 and col0 % bn == 0
    jb = col0 // bn
    if transpose_out:
        out_spec = pl.BlockSpec((bn, bm), lambda i, j: (j, i))
        out_shape = jax.ShapeDtypeStruct((n, m), out_dtype)
    else:
        out_spec = pl.BlockSpec((bm, bn), lambda i, j: (i, j))
        out_shape = jax.ShapeDtypeStruct((m, n), out_dtype)
    return pl.pallas_call(
        functools.partial(_mm_kernel, transpose_out=transpose_out),
        grid=(m // bm, n // bn),
        in_specs=[pl.BlockSpec((bm, k), lambda i, j: (i, 0)),
                  pl.BlockSpec((k, bn), lambda i, j: (0, jb + j))],
        out_specs=out_spec,
        out_shape=out_shape,
        compiler_params=_cparams(2),
        name=name,
    )(a, b)


def _ff1_kernel(x_ref, w_ref, o_ref, wb_ref):
    @pl.when(pl.program_id(1) == 0)
    def _():
        wb_ref[...] = w_ref[...].astype(BF16)

    acc = jnp.dot(x_ref[...], wb_ref[...], preferred_element_type=F32)
    o_ref[...] = jnp.square(jnp.maximum(acc, 0.0)).astype(o_ref.dtype)


def _ff1(x, w, *, bm, bn):
    m, k = x.shape
    _, n = w.shape
    assert m % bm == 0 and n % bn == 0
    return pl.pallas_call(
        _ff1_kernel,
        grid=(n // bn, m // bm),
        in_specs=[pl.BlockSpec((bm, k), lambda j, i: (i, 0)),
                  pl.BlockSpec((k, bn), lambda j, i: (0, j))],
        out_specs=pl.BlockSpec((bm, bn), lambda j, i: (i, j)),
        out_shape=jax.ShapeDtypeStruct((m, n), BF16),
        scratch_shapes=[pltpu.VMEM((k, bn), BF16)],
        compiler_params=_cparams(2),
        name="ff1",
    )(x, w)


def _rwkv_prep_kernel(p_ref, mu_ref, w0_ref, wup_ref, a0_ref, aup_ref, gup_ref, kk_ref, ka_ref,
                      rk_ref, segin_ref, segout_ref, tri_ref,
                      at_ref, bt_ref, kt_ref, rt_ref, v_ref, wc_ref, bonus_ref, g_ref,
                      carry_ref, *, tb):
    i = pl.program_id(0)

    @pl.when(i == 0)
    def _():
        carry_ref[...] = jnp.zeros_like(carry_ref)

    def mixed(lo, hi):
        p = p_ref[:, lo:hi]
        prev_last = carry_ref[0:1, lo:hi]
        shifted = pltpu.roll(p, 1, axis=0)
        row = lax.broadcasted_iota(jnp.int32, p.shape, 0)
        prev = jnp.where(row == 0, prev_last, shifted)
        carry_ref[0:1, lo:hi] = p[tb - 1:tb, :]
        return p + (prev - p) * mu_ref[:, lo:hi]

    w3 = 3 * RWKV_WIDTH
    r = mixed(0, RWKV_WIDTH)
    k = mixed(RWKV_WIDTH, 2 * RWKV_WIDTH)
    v = mixed(2 * RWKV_WIDTH, w3)
    w_d = mixed(w3, w3 + DECAY_PAD)
    a_d = mixed(w3 + DECAY_PAD, w3 + DECAY_PAD + AAA_PAD)
    g_d = mixed(w3 + DECAY_PAD + AAA_PAD, RWKV_PCOLS)

    z = w0_ref[...] + _dot(jnp.tanh(w_d), wup_ref[...])
    nz = -z
    softplus = jnp.maximum(nz, 0.0) + jnp.log1p(jnp.exp(-jnp.abs(nz)))
    lw = -jnp.exp(-softplus - 0.5)
    a = jax.nn.sigmoid(a0_ref[...] + _dot(a_d, aup_ref[...]))
    g_ref[...] = _dot(jax.nn.sigmoid(g_d), gup_ref[...])

    def segsum(x):
        h1, h2 = _split2(jnp.dot(x.astype(BF16), segin_ref[...], preferred_element_type=F32))
        return (jnp.dot(h1, segout_ref[...], preferred_element_type=F32)
                + jnp.dot(h2, segout_ref[...], preferred_element_type=F32))

    kk0 = k * kk_ref[...]
    kk = kk0 / jnp.maximum(jnp.sqrt(segsum(kk0 * kk0)), 1e-12)
    k2 = k * (1.0 + (a - 1.0) * ka_ref[...])
    bonus_ref[...] = segsum(r * k2 * rk_ref[...]) * v

    tri = tri_ref[...]
    l1, l2, l3 = _split3(lw)
    cs = (jnp.dot(tri, l1, preferred_element_type=F32)
          + jnp.dot(tri, l2, preferred_element_type=F32)
          + jnp.dot(tri, l3, preferred_element_type=F32))
    cl = cs[:tb]
    tot = cs[tb:]
    e_neg = jnp.exp(-cl)
    at_ref[...] = (-kk * jnp.exp(cl - lw)).astype(BF16)
    bt_ref[...] = (kk * a * e_neg).astype(BF16)
    kt_ref[...] = (k2 * e_neg).astype(BF16)
    rt_ref[...] = (r * jnp.exp(cl)).astype(BF16)
    v_ref[...] = v.astype(BF16)
    for c in range(tb // CHUNK):
        wc_ref[c] = jnp.exp(tot[c * CHUNK:c * CHUNK + 1, :])


def _rwkv_prep(proj_r, mu, w0, wup, a0, aup, gup, k_k, k_a, r_k, *, tb):
    s = proj_r.shape[0]
    assert s % tb == 0 and tb % CHUNK == 0
    w = RWKV_WIDTH
    lane = jnp.arange(w)
    seg_in = (lane[:, None] // HEAD_DIM == jnp.arange(LANES)[None, :]).astype(BF16)
    t = jnp.arange(tb)
    same = t[:, None] // CHUNK == t[None, :] // CHUNK
    tri = jnp.concatenate([(same & (t[:, None] >= t[None, :])), same], axis=0).astype(BF16)
    row = lambda x: x.reshape(1, -1).astype(F32)
    const = lambda shape: pl.BlockSpec(shape, lambda i: (0,) * len(shape))
    tile = lambda width: pl.BlockSpec((tb, width), lambda i: (i, 0))
    outs = pl.pallas_call(
        functools.partial(_rwkv_prep_kernel, tb=tb),
        grid=(s // tb,),
        in_specs=[tile(RWKV_PCOLS), const((1, RWKV_PCOLS)), const((1, w)), const((DECAY_PAD, w)),
                  const((1, w)), const((AAA_PAD, w)), const((GATE_PAD, w)), const((1, w)),
                  const((1, w)), const((1, w)), const((w, LANES)), const((LANES, w)),
                  const((2 * tb, tb))],
        out_specs=[tile(w), tile(w), tile(w), tile(w), tile(w),
                   pl.BlockSpec((tb // CHUNK, 1, w), lambda i: (i, 0, 0)),
                   tile(w), tile(w)],
        out_shape=[jax.ShapeDtypeStruct((s, w), BF16)] * 5
        + [jax.ShapeDtypeStruct((s // CHUNK, 1, w), F32),
           jax.ShapeDtypeStruct((s, w), F32), jax.ShapeDtypeStruct((s, w), F32)],
        scratch_shapes=[pltpu.VMEM((SUBLANES, RWKV_PCOLS), F32)],
        compiler_params=_cparams(1),
        name="rwkv_prep",
    )(proj_r, row(mu), row(w0), wup, row(a0), aup, gup, row(k_k), row(k_a), row(r_k), seg_in, seg_in.T, tri)
    return outs


def _bdot(a, b):
    return jnp.dot(a, b, preferred_element_type=F32)


def _bdot_nt(a, b):
    return lax.dot_general(a, b, (((1,), (1,)), ((), ())), preferred_element_type=F32)


def _bdot_tn(a, b):
    return lax.dot_general(a, b, (((0,), (0,)), ((), ())), preferred_element_type=F32)


(MASK_STRICT, MASK_INCL, MASK_EYE, MASK_BLK8, MASK_OFF16, MASK_OFF32, MASK_OFF64,
 MASK_HEAD) = range(8)


def _chunk_masks():
    n = GROUP_W
    r = jnp.arange(n)[:, None]
    c = jnp.arange(n)[None, :]
    head = (r // CHUNK) == (c // CHUNK)
    t_r, t_c = r % CHUNK, c % CHUNK
    blk = lambda size: (r // size) == (c // size)
    masks = [head & (t_r > t_c), head & (t_r >= t_c), r == c, blk(8),
             blk(16) & ~blk(8), blk(32) & ~blk(16), blk(64) & ~blk(32), head]
    return jnp.stack(masks).astype(F32)


def _rwkv_chunk_kernel(at_ref, bt_ref, kt_ref, rt_ref, v_ref, wc_ref, bonus_ref, g_ref,
                       gain_ref, bias_ref, mask_ref, o_ref, state_ref, *, nc):
    @pl.when(pl.program_id(0) == 0)
    def _():
        state_ref[...] = jnp.zeros_like(state_ref)

    n = GROUP_W
    ng = RWKV_WIDTH // GROUP_W
    items = [(c, g) for c in range(nc) for g in range(ng)]
    lane_head = lax.broadcasted_iota(jnp.int32, (CHUNK, n), 1) >> 6

    def tile(ref, c, g):
        return ref[c * CHUNK:(c + 1) * CHUNK, g * n:(g + 1) * n]

    def expand(x):
        zero = jnp.zeros_like(x)
        return jnp.concatenate([jnp.where(lane_head == h, x, zero) for h in range(GROUP_HEADS)],
                               axis=0)

    def repeat(x):
        return jnp.concatenate([x] * GROUP_HEADS, axis=0)

    mask = lambda k: mask_ref[k]
    bf = lambda x: x.astype(BF16)
    each = lambda f: [f(*it) for it in items]
    zipped = lambda f, *lists: [f(*args) for args in zip(*lists)]

    at = each(lambda c, g: tile(at_ref, c, g))
    bt = each(lambda c, g: tile(bt_ref, c, g))
    kt = each(lambda c, g: tile(kt_ref, c, g))
    wc = each(lambda c, g: wc_ref[c][:, g * n:(g + 1) * n])
    ae = [expand(x) for x in at]
    be = [expand(x) for x in bt]
    ke = [expand(x) for x in kt]
    ve = each(lambda c, g: expand(tile(v_ref, c, g)))
    at4 = [repeat(x) for x in at]
    rt4 = each(lambda c, g: repeat(tile(rt_ref, c, g)))
    bhe = zipped(lambda x, w: expand(bf(x.astype(F32) * w)), bt, wc)
    khe = zipped(lambda x, w: expand(bf(x.astype(F32) * w)), kt, wc)

    lab = zipped(lambda a, b: _bdot_nt(a, b) * mask(MASK_STRICT), at4, be)
    lak = zipped(lambda a, b: bf(_bdot_nt(a, b) * mask(MASK_STRICT)), at4, ke)
    arb = zipped(lambda a, b: bf(_bdot_nt(a, b) * mask(MASK_INCL)), rt4, be)
    ark = zipped(lambda a, b: bf(_bdot_nt(a, b) * mask(MASK_INCL)), rt4, ke)
    lv = zipped(lambda a, b: bf(_bdot(a, b)), lak, ve)

    l8 = [x * mask(MASK_BLK8) for x in lab]
    minv = [mask(MASK_EYE) + x for x in l8]
    pw = [bf(x) for x in l8]
    for _ in range(2):
        pw2 = [_bdot(x, x) for x in pw]
        pw = [bf(x) for x in pw2]
        minv = zipped(lambda p, m: m + _bdot(p, bf(m)), pw, minv)
    for k in (MASK_OFF16, MASK_OFF32, MASK_OFF64):
        off = [bf(x * mask(k)) for x in lab]
        mb = [bf(m) for m in minv]
        tmp = zipped(lambda m, o: bf(_bdot(m, o)), mb, off)
        minv = zipped(lambda m, t, m16: m + _bdot(t, m16), minv, tmp, mb)
    mb = [bf(m) for m in minv]

    p = zipped(lambda m, a: bf(_bdot(m, a)), mb, ae)
    q = zipped(lambda m, x: bf(_bdot(m, x)), mb, lv)
    g_mat = zipped(lambda r4, a, x: bf(r4.astype(F32) * mask(MASK_HEAD) + _bdot(a, x)), rt4, arb, p)
    y0 = zipped(lambda a, x, b, v: _bdot(a, x) + _bdot(b, v), arb, q, ark, ve)
    phi = zipped(lambda w, b, x: bf(mask(MASK_EYE) * w + _bdot_tn(b, x)), wc, bhe, p)
    psi = zipped(lambda b, x, k, v: _bdot_tn(b, x) + _bdot_tn(k, v), bhe, q, khe, ve)

    state = [state_ref[g] for g in range(ng)]
    for c in range(nc):
        idx = [c * ng + g for g in range(ng)]
        sb = [bf(s) for s in state]
        ys = [_bdot(g_mat[i], sb[g]) + y0[i] for g, i in enumerate(idx)]
        state = [_bdot(phi[i], sb[g]) + psi[i] for g, i in enumerate(idx)]
        for g, y in enumerate(ys):
            mean = jnp.sum(y, axis=1, keepdims=True) * (1.0 / HEAD_DIM)
            yc = (y - mean) * mask(MASK_HEAD)
            var = jnp.sum(yc * yc, axis=1, keepdims=True) * (1.0 / HEAD_DIM)
            yn = yc * lax.rsqrt(var + RWKV_GN_EPS)
            yw = (yn[0:CHUNK] + yn[CHUNK:2 * CHUNK] + yn[2 * CHUNK:3 * CHUNK]
                  + yn[3 * CHUNK:4 * CHUNK])
            rows = slice(c * CHUNK, (c + 1) * CHUNK)
            cols = slice(g * n, (g + 1) * n)
            out = (yw * gain_ref[:, cols] + bias_ref[:, cols] + bonus_ref[rows, cols]) * g_ref[rows, cols]
            o_ref[rows, cols] = out.astype(o_ref.dtype)
    for g in range(ng):
        state_ref[g] = state[g]


def _rwkv_chunks(at, bt, kt, rt, v, wc, bonus, g, gn_gain, gn_bias, *, nc):
    s, w = at.shape
    tb = nc * CHUNK
    assert s % tb == 0 and w % GROUP_W == 0
    tile = pl.BlockSpec((tb, w), lambda ti: (ti, 0))
    rowspec = pl.BlockSpec((1, w), lambda ti: (0, 0))
    masks = _chunk_masks()
    return pl.pallas_call(
        functools.partial(_rwkv_chunk_kernel, nc=nc),
        grid=(s // tb,),
        in_specs=[tile, tile, tile, tile, tile,
                  pl.BlockSpec((nc, 1, w), lambda ti: (ti, 0, 0)),
                  tile, tile, rowspec, rowspec,
                  pl.BlockSpec(masks.shape, lambda ti: (0, 0, 0))],
        out_specs=tile,
        out_shape=jax.ShapeDtypeStruct((s, w), BF16),
        scratch_shapes=[pltpu.VMEM((w // GROUP_W, GROUP_W, GROUP_W), F32)],
        compiler_params=_cparams(1),
        name="rwkv_chunks",
    )(at, bt, kt, rt, v, wc, bonus, g, gn_gain.reshape(1, -1).astype(F32),
      gn_bias.reshape(1, -1).astype(F32), masks)


def _proj_rope_kernel(pos_ref, invf_ref, x_ref, w_ref, o_ref, cos_ref, sin_ref, *, bm, bn):
    lane = lax.broadcasted_iota(jnp.int32, (bm, LANES), 1)
    first_half = (lane & (HEAD_DIM - 1)) < (HEAD_DIM // 2)

    @pl.when(pl.program_id(1) == 0)
    def _():
        ang = pos_ref[...].astype(F32) * invf_ref[...]
        cos_ref[...] = jnp.cos(ang)
        sin = jnp.sin(ang)
        sin_ref[...] = jnp.where(first_half, -sin, sin)

    acc = jnp.dot(x_ref[...], w_ref[...], preferred_element_type=F32)
    cos = cos_ref[...]
    sin = sin_ref[...]
    for c in range(bn // LANES):
        t = acc[:, c * LANES:(c + 1) * LANES]
        partner = jnp.where(first_half, pltpu.roll(t, LANES - HEAD_DIM // 2, axis=1),
                            pltpu.roll(t, HEAD_DIM // 2, axis=1))
        o_ref[:, c * LANES:(c + 1) * LANES] = (t * cos + partner * sin).astype(o_ref.dtype)


def _proj_rope(xb, w_all, positions, *, bm, bn, n, col0):
    s, k = xb.shape
    assert n % bn == 0 and col0 % bn == 0
    jb = col0 // bn
    half = jnp.arange(0, HEAD_DIM, 2, dtype=F32)
    inv_freq = ROPE_THETA ** (-half / HEAD_DIM)
    invf = jnp.tile(inv_freq, 2 * LANES // HEAD_DIM).reshape(1, LANES)
    return pl.pallas_call(
        functools.partial(_proj_rope_kernel, bm=bm, bn=bn),
        grid=(s // bm, n // bn),
        in_specs=[pl.BlockSpec((bm, 1), lambda i, j: (i, 0)),
                  pl.BlockSpec((1, LANES), lambda i, j: (0, 0)),
                  pl.BlockSpec((bm, k), lambda i, j: (i, 0)),
                  pl.BlockSpec((k, bn), lambda i, j: (0, jb + j))],
        out_specs=pl.BlockSpec((bm, bn), lambda i, j: (i, j)),
        out_shape=jax.ShapeDtypeStruct((s, n), BF16),
        scratch_shapes=[pltpu.VMEM((bm, LANES), F32), pltpu.VMEM((bm, LANES), F32)],
        compiler_params=_cparams(2),
        name="proj_qk_rope",
    )(positions.reshape(s, 1), invf, xb, w_all)


ATT_STRIP = 128


def _attn_kernel(lam_ref, gain_ref, q_ref, k_ref, vt_ref, o_ref, acc_ref, s0_ref, s1_ref, *, bq):
    i = pl.program_id(1)
    nq = 2 * bq
    q = q_ref[...]
    lane = lax.broadcasted_iota(jnp.int32, q.shape, 1)
    zero = jnp.zeros_like(q)
    q_all = jnp.concatenate([jnp.where(lane < HEAD_DIM, q, zero),
                             jnp.where(lane >= HEAD_DIM, q, zero)], axis=0)

    s_slots = (s0_ref, s1_ref)

    def scores(j, slot):
        start = pl.multiple_of(j * bq, bq)
        s_slots[slot][...] = lax.dot_general(k_ref[pl.ds(start, bq), :], q_all,
                                             (((1,), (1,)), ((), ())), preferred_element_type=F32)

    def consume(j, slot, m, l, masked):
        start = pl.multiple_of(j * bq, bq)
        vtb = vt_ref[:, pl.ds(start, bq)]
        p_strips, m_new, l_new, alphas = [], [], [], []
        for c in range(nq // ATT_STRIP):
            sl = slice(c * ATT_STRIP, (c + 1) * ATT_STRIP)
            s = s_slots[slot][:, sl]
            if masked:
                key = lax.broadcasted_iota(jnp.int32, s.shape, 0)
                qry = lax.broadcasted_iota(jnp.int32, s.shape, 1) + (c * ATT_STRIP) % bq
                s = jnp.where(key <= qry, s, -1e30)
            mn = jnp.maximum(m[:, sl], jnp.max(s, axis=0, keepdims=True))
            p = jnp.exp2(s - mn)
            alpha = jnp.exp2(m[:, sl] - mn)
            m_new.append(mn)
            l_new.append(alpha * l[:, sl] + jnp.sum(p, axis=0, keepdims=True))
            alphas.append(alpha)
            p_strips.append(p.astype(BF16))
        p_all = jnp.concatenate(p_strips, axis=1)
        alpha = jnp.concatenate(alphas, axis=1)
        acc_ref[...] = alpha * acc_ref[...] + jnp.dot(vtb, p_all, preferred_element_type=F32)
        return jnp.concatenate(m_new, axis=1), jnp.concatenate(l_new, axis=1)

    def pair(jj, carry):
        j = 2 * jj
        scores(j + 1, 1)
        m, l = consume(j, 0, carry[0], carry[1], False)
        scores(j + 2, 0)
        return consume(j + 1, 1, m, l, False)

    def tail_even(carry):
        return consume(i, 0, carry[0], carry[1], True)

    def tail_odd(carry):
        scores(i, 1)
        m, l = consume(i - 1, 0, carry[0], carry[1], False)
        return consume(i, 1, m, l, True)

    acc_ref[...] = jnp.zeros_like(acc_ref)
    scores(0, 0)
    init = (jnp.full((1, nq), -1e30, F32), jnp.zeros((1, nq), F32))
    carry = lax.fori_loop(0, i // 2, pair, init)
    m, l = lax.cond((i & 1) == 0, tail_even, tail_odd, carry)

    lam_v = lam_ref[...]
    lam = (jnp.exp(jnp.sum(lam_v[0:1] * lam_v[1:2], axis=1, keepdims=True))
           - jnp.exp(jnp.sum(lam_v[2:3] * lam_v[3:4], axis=1, keepdims=True)) + LAMBDA_INIT)
    o = acc_ref[...] / l
    out = o[:, :bq] - lam * o[:, bq:]
    out = out * lax.rsqrt(jnp.mean(jnp.square(out), axis=0, keepdims=True) + SUBLN_EPS)
    out = out * gain_ref[...] * (1.0 - LAMBDA_INIT)
    o_ref[...] = out.T.astype(o_ref.dtype)


def _attention(qk, vt, lam_vecs, subln_gain, *, bq):
    s = qk.shape[0]
    assert s % bq == 0 and bq % ATT_STRIP == 0
    return pl.pallas_call(
        functools.partial(_attn_kernel, bq=bq),
        grid=(DIFF_HEADS, s // bq),
        in_specs=[pl.BlockSpec((4, HEAD_DIM), lambda h, i: (0, 0)),
                  pl.BlockSpec((DIFF_V_DIM, 1), lambda h, i: (0, 0)),
                  pl.BlockSpec((bq, DIFF_V_DIM), lambda h, i: (i, h)),
                  pl.BlockSpec((s, DIFF_V_DIM), lambda h, i: (0, DIFF_HEADS + h)),
                  pl.BlockSpec((DIFF_V_DIM, s), lambda h, i: (h, 0))],
        out_specs=pl.BlockSpec((bq, DIFF_V_DIM), lambda h, i: (i, h)),
        out_shape=jax.ShapeDtypeStruct((s, DIFF_WIDTH), BF16),
        scratch_shapes=[pltpu.VMEM((DIFF_V_DIM, 2 * bq), F32), pltpu.VMEM((bq, 2 * bq), F32),
                        pltpu.VMEM((bq, 2 * bq), F32)],
        compiler_params=_cparams(2),
        name="diff_attn",
    )(lam_vecs, subln_gain.reshape(-1, 1).astype(F32), qk, qk, vt)


def _layer_norm(z, gain, bias):
    mean = jnp.mean(z, axis=-1, keepdims=True)
    zc = z - mean
    var = jnp.mean(zc * zc, axis=-1, keepdims=True)
    return zc * lax.rsqrt(var + LN_EPS) * gain + bias


def _outproj_kernel(yr_ref, yd_ref, wr_ref, wd_ref, x_ref, gain_ref, bias_ref, o32_ref, o16_ref):
    mix = (jnp.dot(yr_ref[...], wr_ref[...], preferred_element_type=F32)
           + jnp.dot(yd_ref[...], wd_ref[...], preferred_element_type=F32))
    y = _layer_norm(DEEPNORM_ALPHA * x_ref[...] + mix, gain_ref[...], bias_ref[...])
    o32_ref[...] = y
    o16_ref[...] = y.astype(BF16)


def _outproj(y_rwkv, y_diff, w_r, w_d, x, gain, bias, *, bm):
    s = x.shape[0]
    const = lambda shape: pl.BlockSpec(shape, lambda i: (0, 0))
    return pl.pallas_call(
        _outproj_kernel,
        grid=(s // bm,),
        in_specs=[pl.BlockSpec((bm, RWKV_WIDTH), lambda i: (i, 0)),
                  pl.BlockSpec((bm, DIFF_WIDTH), lambda i: (i, 0)),
                  const((RWKV_WIDTH, D_MODEL)), const((DIFF_WIDTH, D_MODEL)),
                  pl.BlockSpec((bm, D_MODEL), lambda i: (i, 0)),
                  const((1, D_MODEL)), const((1, D_MODEL))],
        out_specs=[pl.BlockSpec((bm, D_MODEL), lambda i: (i, 0))] * 2,
        out_shape=[jax.ShapeDtypeStruct((s, D_MODEL), F32), jax.ShapeDtypeStruct((s, D_MODEL), BF16)],
        compiler_params=_cparams(1),
        name="outproj_ln",
    )(y_rwkv, y_diff, w_r, w_d, x, gain.reshape(1, -1), bias.reshape(1, -1))


def _ff2_kernel(h_ref, w_ref, x_ref, gain_ref, bias_ref, o_ref):
    kk = pl.program_id(1)
    part = lambda: jnp.dot(h_ref[...], w_ref[...], preferred_element_type=F32)

    @pl.when(kk == 0)
    def _():
        o_ref[...] = part()

    @pl.when((kk > 0) & (kk < pl.num_programs(1) - 1))
    def _():
        o_ref[...] += part()

    @pl.when(kk == pl.num_programs(1) - 1)
    def _():
        o_ref[...] = _layer_norm(DEEPNORM_ALPHA * x_ref[...] + (o_ref[...] + part()),
                                 gain_ref[...], bias_ref[...])


def _ff2(h, w, x, gain, bias, *, bm, bk):
    s, kdim = h.shape
    assert s % bm == 0 and kdim % bk == 0 and kdim // bk >= 2
    const = lambda shape: pl.BlockSpec(shape, lambda i, k: (0, 0))
    return pl.pallas_call(
        _ff2_kernel,
        grid=(s // bm, kdim // bk),
        in_specs=[pl.BlockSpec((bm, bk), lambda i, k: (i, k)),
                  pl.BlockSpec((bk, D_MODEL), lambda i, k: (k, 0)),
                  pl.BlockSpec((bm, D_MODEL), lambda i, k: (i, 0), pipeline_mode=pl.Buffered(1)),
                  const((1, D_MODEL)), const((1, D_MODEL))],
        out_specs=pl.BlockSpec((bm, D_MODEL), lambda i, k: (i, 0)),
        out_shape=jax.ShapeDtypeStruct((s, D_MODEL), F32),
        compiler_params=_cparams(2),
        name="ff2_ln",
    )(h, w, x, gain.reshape(1, -1), bias.reshape(1, -1))


def _pad_rows(w, rows):
    return jnp.pad(w, ((0, rows - w.shape[0]), (0, 0)))


def _layer(x, positions, w_in, mu_shift, w0, w_up, a0, a_up, g_up, k_k, k_a, r_k, gn_gain, gn_bias,
           lq1, lk1, lq2, lk2, subln_gain, w_out, ln1_gain, ln1_bias, w_ff1, w_ff2, ln2_gain, ln2_bias,
           *, bm, prep_tb, nc, bq):
    w3 = 3 * RWKV_WIDTH
    rwkv_cols = w3 + DECAY_LORA + AAA_LORA + GATE_LORA
    qk0 = RWKV_PCOLS
    v0 = qk0 + 2 * DIFF_WIDTH

    def pad_cols(a, lo, hi, width):
        return jnp.pad(a[..., lo:hi], [(0, 0)] * (a.ndim - 1) + [(0, width - (hi - lo))])

    def rwkv_layout(a):
        return jnp.concatenate([
            a[..., :w3],
            pad_cols(a, w3, w3 + DECAY_LORA, DECAY_PAD),
            pad_cols(a, w3 + DECAY_LORA, w3 + DECAY_LORA + AAA_LORA, AAA_PAD),
            pad_cols(a, w3 + DECAY_LORA + AAA_LORA, rwkv_cols, GATE_PAD)], axis=-1)

    xb = x.astype(BF16)
    w_all = jnp.concatenate([
        rwkv_layout(w_in[:, :rwkv_cols]),
        w_in[:, rwkv_cols:rwkv_cols + DIFF_WIDTH] * (HEAD_DIM ** -0.5 * LOG2_E),
        w_in[:, rwkv_cols + DIFF_WIDTH:]], axis=1).astype(BF16)
    proj_r = _matmul(xb, w_all, bm=bm, bn=512, n=RWKV_PCOLS, out_dtype=F32, name="proj_rwkv")
    vt = _matmul(xb, w_all, bm=bm, bn=512, n=DIFF_WIDTH, col0=v0, out_dtype=BF16,
                 transpose_out=True, name="proj_vt")

    at, bt, kt, rt, v_r, wc, bonus, gate = _rwkv_prep(
        proj_r, rwkv_layout(mu_shift), w0,
        _pad_rows(w_up, DECAY_PAD).astype(BF16), a0, _pad_rows(a_up, AAA_PAD).astype(BF16),
        _pad_rows(g_up, GATE_PAD).astype(BF16), k_k, k_a, r_k.reshape(-1), tb=prep_tb)
    y_rwkv = _rwkv_chunks(at, bt, kt, rt, v_r, wc, bonus, gate, gn_gain, gn_bias, nc=nc)

    qk = _proj_rope(xb, w_all, positions, bm=bm, bn=512, n=2 * DIFF_WIDTH, col0=qk0)
    lam_vecs = jnp.stack([lq1, lk1, lq2, lk2]).astype(F32)
    y_diff = _attention(qk, vt, lam_vecs, subln_gain, bq=bq)

    wo = w_out.astype(BF16)
    x1, x1b = _outproj(y_rwkv, y_diff, wo[:RWKV_WIDTH], wo[RWKV_WIDTH:], x, ln1_gain, ln1_bias, bm=256)
    h = _ff1(x1b, w_ff1, bm=bm, bn=1024)
    return _ff2(h, w_ff2.astype(BF16), x1, ln2_gain, ln2_bias, bm=1024, bk=1024)


def kernel(x, positions, w_in, mu_shift, rwkv_w0, rwkv_w_up, rwkv_a0, rwkv_a_up, rwkv_g_up, rwkv_k_k,
           rwkv_k_a, rwkv_r_k, rwkv_gn_gain, rwkv_gn_bias, diff_lambda_q1, diff_lambda_k1,
           diff_lambda_q2, diff_lambda_k2, diff_subln_gain, w_out, ln1_gain, ln1_bias, w_ff1, w_ff2,
           ln2_gain, ln2_bias):
    batch, seq, _ = x.shape
    assert batch == 1 and w_in.shape[0] == 1
    out = _layer(
        x[0], positions[0], w_in[0], mu_shift[0], rwkv_w0[0], rwkv_w_up[0], rwkv_a0[0], rwkv_a_up[0],
        rwkv_g_up[0], rwkv_k_k[0], rwkv_k_a[0], rwkv_r_k[0], rwkv_gn_gain[0], rwkv_gn_bias[0],
        diff_lambda_q1[0], diff_lambda_k1[0], diff_lambda_q2[0], diff_lambda_k2[0],
        diff_subln_gain[0], w_out[0], ln1_gain[0], ln1_bias[0], w_ff1[0], w_ff2[0], ln2_gain[0],
        ln2_bias[0], bm=1024, prep_tb=256, nc=2, bq=512)
    return out[None]
```

```python
import functools
import math

import jax
import jax.numpy as jnp
from jax import lax
from jax.experimental import pallas as pl
from jax.experimental.pallas import tpu as pltpu

F32 = jnp.float32
BF16 = jnp.bfloat16

D_MODEL = 2048
RWKV_WIDTH = 1024
HEAD_DIM = 64
DECAY_LORA = 64
AAA_LORA = 64
GATE_LORA = 160
DIFF_WIDTH = 1024
DIFF_V_DIM = 128
DIFF_HEADS = 8
D_FF = 8192
ROPE_THETA = 10000.0
LN_EPS = 1e-5
RWKV_GN_EPS = 64e-5
SUBLN_EPS = 1e-5
DEEPNORM_ALPHA = 2.0 ** 0.25
LAMBDA_INIT = 0.8 - 0.6 * math.exp(0.0)
LOG2_E = math.log2(math.e)

LANES = 128
SUBLANES = 8
VMEM_LIMIT_BYTES = 56 * 1024 * 1024

DECAY_PAD = 128
AAA_PAD = 128
GATE_PAD = 256
RWKV_PCOLS = 3 * RWKV_WIDTH + DECAY_PAD + AAA_PAD + GATE_PAD
CHUNK = 64
GROUP_HEADS = 4
GROUP_W = GROUP_HEADS * HEAD_DIM


def _cparams(n_axes):
    return pltpu.CompilerParams(
        dimension_semantics=("arbitrary",) * n_axes,
        vmem_limit_bytes=VMEM_LIMIT_BYTES,
    )


def _dot(a, b):
    return jnp.dot(a.astype(BF16), b.astype(BF16), preferred_element_type=F32)


def _dot_nt(a, b):
    return lax.dot_general(a.astype(BF16), b.astype(BF16), (((1,), (1,)), ((), ())),
                           preferred_element_type=F32)


def _dot_tn(a, b):
    return lax.dot_general(a.astype(BF16), b.astype(BF16), (((0,), (0,)), ((), ())),
                           preferred_element_type=F32)


def _split3(x):
    h1 = x.astype(BF16)
    r1 = x - h1.astype(F32)
    h2 = r1.astype(BF16)
    h3 = (r1 - h2.astype(F32)).astype(BF16)
    return h1, h2, h3


def _split2(x):
    h1 = x.astype(BF16)
    h2 = (x - h1.astype(F32)).astype(BF16)
    return h1, h2


def _mm_kernel(a_ref, b_ref, o_ref, *, transpose_out):
    acc = jnp.dot(a_ref[...], b_ref[...], preferred_element_type=F32)
    if transpose_out:
        acc = acc.T
    o_ref[...] = acc.astype(o_ref.dtype)


def _matmul(a, b, *, bm, bn, n, col0=0, out_dtype, transpose_out=False, name):
    m, k = a.shape
    assert m % bm == 0 and n % bn == 0 and col0 % bn == 0
    jb = col0 // bn
    if transpose_out:
        out_spec = pl.BlockSpec((bn, bm), lambda i, j: (j, i))
        out_shape = jax.ShapeDtypeStruct((n, m), out_dtype)
    else:
        out_spec = pl.BlockSpec((bm, bn), lambda i, j: (i, j))
        out_shape = jax.ShapeDtypeStruct((m, n), out_dtype)
    return pl.pallas_call(
        functools.partial(_mm_kernel, transpose_out=transpose_out),
        grid=(m // bm, n // bn),
        in_specs=[pl.BlockSpec((bm, k), lambda i, j: (i, 0)),
                  pl.BlockSpec((k, bn), lambda i, j: (0, jb + j))],
        out_specs=out_spec,
        out_shape=out_shape,
        compiler_params=_cparams(2),
        name=name,
    )(a, b)


def _proj_cast_kernel(x_ref, w_ref, o_ref, xb_ref):
    @pl.when(pl.program_id(1) == 0)
    def _():
        xb_ref[...] = x_ref[...].astype(BF16)

    o_ref[...] = jnp.dot(xb_ref[...], w_ref[...], preferred_element_type=F32)


def _proj_cast(x, w_all, *, bm, bn, n):
    m, k = x.shape
    assert m % bm == 0 and n % bn == 0
    return pl.pallas_call(
        _proj_cast_kernel,
        grid=(m // bm, n // bn),
        in_specs=[pl.BlockSpec((bm, k), lambda i, j: (i, 0)),
                  pl.BlockSpec((k, bn), lambda i, j: (0, j))],
        out_specs=[pl.BlockSpec((bm, bn), lambda i, j: (i, j)),
                   pl.BlockSpec((bm, k), lambda i, j: (i, 0))],
        out_shape=[jax.ShapeDtypeStruct((m, n), F32), jax.ShapeDtypeStruct((m, k), BF16)],
        compiler_params=_cparams(2),
        name="proj_rwkv",
    )(x, w_all)


def _ff1_kernel(x_ref, w_ref, o_ref, wb_ref):
    @pl.when(pl.program_id(1) == 0)
    def _():
        wb_ref[...] = w_ref[...].astype(BF16)

    acc = jnp.dot(x_ref[...], wb_ref[...], preferred_element_type=F32)
    o_ref[...] = jnp.square(jnp.maximum(acc, 0.0)).astype(o_ref.dtype)


def _ff1(x, w, *, bm, bn):
    m, k = x.shape
    _, n = w.shape
    assert m % bm == 0 and n % bn == 0
    return pl.pallas_call(
        _ff1_kernel,
        grid=(n // bn, m // bm),
        in_specs=[pl.BlockSpec((bm, k), lambda j, i: (i, 0)),
                  pl.BlockSpec((k, bn), lambda j, i: (0, j))],
        out_specs=pl.BlockSpec((bm, bn), lambda j, i: (i, j)),
        out_shape=jax.ShapeDtypeStruct((m, n), BF16),
        scratch_shapes=[pltpu.VMEM((k, bn), BF16)],
        compiler_params=_cparams(2),
        name="ff1",
    )(x, w)


def _relayout_kernel(w_ref, o_ref):
    rows = w_ref.shape[0]
    w3 = 3 * RWKV_WIDTH
    a0 = w3 + DECAY_LORA
    g0 = a0 + AAA_LORA
    q0 = g0 + GATE_LORA
    zeros = lambda n: jnp.zeros((rows, n), F32)
    pieces = [w_ref[:, :a0], zeros(DECAY_PAD - DECAY_LORA),
              w_ref[:, a0:g0], zeros(AAA_PAD - AAA_LORA),
              w_ref[:, g0:q0], zeros(GATE_PAD - GATE_LORA),
              w_ref[:, q0:q0 + DIFF_WIDTH] * (HEAD_DIM ** -0.5 * LOG2_E),
              w_ref[:, q0 + DIFF_WIDTH:]]
    o_ref[...] = jnp.concatenate(pieces, axis=1).astype(o_ref.dtype)


def _relayout_w_in(w_in, *, bk):
    k, n = w_in.shape
    n_out = RWKV_PCOLS + 3 * DIFF_WIDTH
    assert n == n_out - (RWKV_PCOLS - 3 * RWKV_WIDTH - DECAY_LORA - AAA_LORA - GATE_LORA)
    return pl.pallas_call(
        _relayout_kernel,
        grid=(k // bk,),
        in_specs=[pl.BlockSpec((bk, n), lambda i: (i, 0))],
        out_specs=pl.BlockSpec((bk, n_out), lambda i: (i, 0)),
        out_shape=jax.ShapeDtypeStruct((k, n_out), BF16),
        compiler_params=_cparams(1),
        name="relayout_w_in",
    )(w_in)


def _rwkv_prep_kernel(p_ref, mu_ref, w0_ref, wup_ref, a0_ref, aup_ref, gup_ref, kk_ref, ka_ref,
                      rk_ref, segin_ref, segout_ref, tri_ref,
                      at_ref, bt_ref, kt_ref, rt_ref, v_ref, wc_ref, bonus_ref, g_ref,
                      carry_ref, *, tb):
    i = pl.program_id(0)

    @pl.when(i == 0)
    def _():
        carry_ref[...] = jnp.zeros_like(carry_ref)

    def mixed(lo, hi):
        p = p_ref[:, lo:hi]
        prev_last = carry_ref[0:1, lo:hi]
        shifted = pltpu.roll(p, 1, axis=0)
        row = lax.broadcasted_iota(jnp.int32, p.shape, 0)
        prev = jnp.where(row == 0, prev_last, shifted)
        carry_ref[0:1, lo:hi] = p[tb - 1:tb, :]
        return p + (prev - p) * mu_ref[:, lo:hi]

    w3 = 3 * RWKV_WIDTH
    r = mixed(0, RWKV_WIDTH)
    k = mixed(RWKV_WIDTH, 2 * RWKV_WIDTH)
    v = mixed(2 * RWKV_WIDTH, w3)
    w_d = mixed(w3, w3 + DECAY_PAD)
    a_d = mixed(w3 + DECAY_PAD, w3 + DECAY_PAD + AAA_PAD)
    g_d = mixed(w3 + DECAY_PAD + AAA_PAD, RWKV_PCOLS)

    z = w0_ref[...] + _dot(jnp.tanh(w_d), wup_ref[...])
    nz = -z
    softplus = jnp.maximum(nz, 0.0) + jnp.log1p(jnp.exp(-jnp.abs(nz)))
    lw = -jnp.exp(-softplus - 0.5)
    a = jax.nn.sigmoid(a0_ref[...] + _dot(a_d, aup_ref[...]))
    g_ref[...] = _dot(jax.nn.sigmoid(g_d), gup_ref[...])

    def segsum(x):
        h1, h2 = _split2(jnp.dot(x.astype(BF16), segin_ref[...], preferred_element_type=F32))
        return (jnp.dot(h1, segout_ref[...], preferred_element_type=F32)
                + jnp.dot(h2, segout_ref[...], preferred_element_type=F32))

    kk0 = k * kk_ref[...]
    kk = kk0 / jnp.maximum(jnp.sqrt(segsum(kk0 * kk0)), 1e-12)
    k2 = k * (1.0 + (a - 1.0) * ka_ref[...])
    bonus_ref[...] = segsum(r * k2 * rk_ref[...]) * v

    tri = tri_ref[...]
    l1, l2, l3 = _split3(lw)
    cs = (jnp.dot(tri, l1, preferred_element_type=F32)
          + jnp.dot(tri, l2, preferred_element_type=F32)
          + jnp.dot(tri, l3, preferred_element_type=F32))
    cl = cs[:tb]
    tot = cs[tb:]
    e_neg = jnp.exp(-cl)
    at_ref[...] = (-kk * jnp.exp(cl - lw)).astype(BF16)
    bt_ref[...] = (kk * a * e_neg).astype(BF16)
    kt_ref[...] = (k2 * e_neg).astype(BF16)
    rt_ref[...] = (r * jnp.exp(cl)).astype(BF16)
    v_ref[...] = v.astype(BF16)
    for c in range(tb // CHUNK):
        wc_ref[c] = jnp.exp(tot[c * CHUNK:c * CHUNK + 1, :])


def _rwkv_prep(proj_r, mu, w0, wup, a0, aup, gup, k_k, k_a, r_k, *, tb):
    s = proj_r.shape[0]
    assert s % tb == 0 and tb % CHUNK == 0
    w = RWKV_WIDTH
    lane = jnp.arange(w)
    seg_in = (lane[:, None] // HEAD_DIM == jnp.arange(LANES)[None, :]).astype(BF16)
    t = jnp.arange(tb)
    same = t[:, None] // CHUNK == t[None, :] // CHUNK
    tri = jnp.concatenate([(same & (t[:, None] >= t[None, :])), same], axis=0).astype(BF16)
    row = lambda x: x.reshape(1, -1).astype(F32)
    const = lambda shape: pl.BlockSpec(shape, lambda i: (0,) * len(shape))
    tile = lambda width: pl.BlockSpec((tb, width), lambda i: (i, 0))
    outs = pl.pallas_call(
        functools.partial(_rwkv_prep_kernel, tb=tb),
        grid=(s // tb,),
        in_specs=[tile(RWKV_PCOLS), const((1, RWKV_PCOLS)), const((1, w)), const((DECAY_PAD, w)),
                  const((1, w)), const((AAA_PAD, w)), const((GATE_PAD, w)), const((1, w)),
                  const((1, w)), const((1, w)), const((w, LANES)), const((LANES, w)),
                  const((2 * tb, tb))],
        out_specs=[tile(w), tile(w), tile(w), tile(w), tile(w),
                   pl.BlockSpec((tb // CHUNK, 1, w), lambda i: (i, 0, 0)),
                   tile(w), tile(w)],
        out_shape=[jax.ShapeDtypeStruct((s, w), BF16)] * 5
        + [jax.ShapeDtypeStruct((s // CHUNK, 1, w), F32),
           jax.ShapeDtypeStruct((s, w), F32), jax.ShapeDtypeStruct((s, w), F32)],
        scratch_shapes=[pltpu.VMEM((SUBLANES, RWKV_PCOLS), F32)],
        compiler_params=_cparams(1),
        name="rwkv_prep",
    )(proj_r, row(mu), row(w0), wup, row(a0), aup, gup, row(k_k), row(k_a), row(r_k), seg_in, seg_in.T, tri)
    return outs


def _bdot(a, b):
    return jnp.dot(a, b, preferred_element_type=F32)


def _bdot_nt(a, b):
    return lax.dot_general(a, b, (((1,), (1,)), ((), ())), preferred_element_type=F32)


def _bdot_tn(a, b):
    return lax.dot_general(a, b, (((0,), (0,)), ((), ())), preferred_element_type=F32)


(MASK_STRICT, MASK_INCL, MASK_EYE, MASK_BLK8, MASK_OFF16, MASK_OFF32, MASK_OFF64,
 MASK_HEAD) = range(8)


def _chunk_masks():
    n = GROUP_W
    r = jnp.arange(n)[:, None]
    c = jnp.arange(n)[None, :]
    head = (r // CHUNK) == (c // CHUNK)
    t_r, t_c = r % CHUNK, c % CHUNK
    blk = lambda size: (r // size) == (c // size)
    masks = [head & (t_r > t_c), head & (t_r >= t_c), r == c, blk(8),
             blk(16) & ~blk(8), blk(32) & ~blk(16), blk(64) & ~blk(32), head]
    return jnp.stack(masks).astype(F32)


def _rwkv_chunk_kernel(at_ref, bt_ref, kt_ref, rt_ref, v_ref, wc_ref, bonus_ref, g_ref,
                       gain_ref, bias_ref, mask_ref, o_ref, state_ref, *, nc):
    @pl.when(pl.program_id(0) == 0)
    def _():
        state_ref[...] = jnp.zeros_like(state_ref)

    n = GROUP_W
    ng = RWKV_WIDTH // GROUP_W
    items = [(c, g) for c in range(nc) for g in range(ng)]
    lane_head = lax.broadcasted_iota(jnp.int32, (CHUNK, n), 1) >> 6

    def tile(ref, c, g):
        return ref[c * CHUNK:(c + 1) * CHUNK, g * n:(g + 1) * n]

    def expand(x):
        zero = jnp.zeros_like(x)
        return jnp.concatenate([jnp.where(lane_head == h, x, zero) for h in range(GROUP_HEADS)],
                               axis=0)

    def repeat(x):
        return jnp.concatenate([x] * GROUP_HEADS, axis=0)

    mask = lambda k: mask_ref[k]
    bf = lambda x: x.astype(BF16)
    each = lambda f: [f(*it) for it in items]
    zipped = lambda f, *lists: [f(*args) for args in zip(*lists)]

    at = each(lambda c, g: tile(at_ref, c, g))
    bt = each(lambda c, g: tile(bt_ref, c, g))
    kt = each(lambda c, g: tile(kt_ref, c, g))
    wc = each(lambda c, g: wc_ref[c][:, g * n:(g + 1) * n])
    ae = [expand(x) for x in at]
    be = [expand(x) for x in bt]
    ke = [expand(x) for x in kt]
    ve = each(lambda c, g: expand(tile(v_ref, c, g)))
    at4 = [repeat(x) for x in at]
    rt4 = each(lambda c, g: repeat(tile(rt_ref, c, g)))
    bhe = zipped(lambda x, w: expand(bf(x.astype(F32) * w)), bt, wc)
    khe = zipped(lambda x, w: expand(bf(x.astype(F32) * w)), kt, wc)

    lab = zipped(lambda a, b: _bdot_nt(a, b) * mask(MASK_STRICT), at4, be)
    lak = zipped(lambda a, b: bf(_bdot_nt(a, b) * mask(MASK_STRICT)), at4, ke)
    arb = zipped(lambda a, b: bf(_bdot_nt(a, b) * mask(MASK_INCL)), rt4, be)
    ark = zipped(lambda a, b: bf(_bdot_nt(a, b) * mask(MASK_INCL)), rt4, ke)
    lv = zipped(lambda a, b: bf(_bdot(a, b)), lak, ve)

    l8 = [x * mask(MASK_BLK8) for x in lab]
    minv = [mask(MASK_EYE) + x for x in l8]
    pw = [bf(x) for x in l8]
    for _ in range(2):
        pw2 = [_bdot(x, x) for x in pw]
        pw = [bf(x) for x in pw2]
        minv = zipped(lambda p, m: m + _bdot(p, bf(m)), pw, minv)
    for k in (MASK_OFF16, MASK_OFF32, MASK_OFF64):
        off = [bf(x * mask(k)) for x in lab]
        mb = [bf(m) for m in minv]
        tmp = zipped(lambda m, o: bf(_bdot(m, o)), mb, off)
        minv = zipped(lambda m, t, m16: m + _bdot(t, m16), minv, tmp, mb)
    mb = [bf(m) for m in minv]

    p = zipped(lambda m, a: bf(_bdot(m, a)), mb, ae)
    q = zipped(lambda m, x: bf(_bdot(m, x)), mb, lv)
    g_mat = zipped(lambda r4, a, x: bf(r4.astype(F32) * mask(MASK_HEAD) + _bdot(a, x)), rt4, arb, p)
    y0 = zipped(lambda a, x, b, v: _bdot(a, x) + _bdot(b, v), arb, q, ark, ve)
    phi = zipped(lambda w, b, x: bf(mask(MASK_EYE) * w + _bdot_tn(b, x)), wc, bhe, p)
    psi = zipped(lambda b, x, k, v: _bdot_tn(b, x) + _bdot_tn(k, v), bhe, q, khe, ve)

    state = [state_ref[g] for g in range(ng)]
    for c in range(nc):
        idx = [c * ng + g for g in range(ng)]
        sb = [bf(s) for s in state]
        ys = [_bdot(g_mat[i], sb[g]) + y0[i] for g, i in enumerate(idx)]
        state = [_bdot(phi[i], sb[g]) + psi[i] for g, i in enumerate(idx)]
        for g, y in enumerate(ys):
            mean = jnp.sum(y, axis=1, keepdims=True) * (1.0 / HEAD_DIM)
            yc = (y - mean) * mask(MASK_HEAD)
            var = jnp.sum(yc * yc, axis=1, keepdims=True) * (1.0 / HEAD_DIM)
            yn = yc * lax.rsqrt(var + RWKV_GN_EPS)
            yw = (yn[0:CHUNK] + yn[CHUNK:2 * CHUNK] + yn[2 * CHUNK:3 * CHUNK]
                  + yn[3 * CHUNK:4 * CHUNK])
            rows = slice(c * CHUNK, (c + 1) * CHUNK)
            cols = slice(g * n, (g + 1) * n)
            out = (yw * gain_ref[:, cols] + bias_ref[:, cols] + bonus_ref[rows, cols]) * g_ref[rows, cols]
            o_ref[rows, cols] = out.astype(o_ref.dtype)
    for g in range(ng):
        state_ref[g] = state[g]


def _rwkv_chunks(at, bt, kt, rt, v, wc, bonus, g, gn_gain, gn_bias, *, nc):
    s, w = at.shape
    tb = nc * CHUNK
    assert s % tb == 0 and w % GROUP_W == 0
    tile = pl.BlockSpec((tb, w), lambda ti: (ti, 0))
    rowspec = pl.BlockSpec((1, w), lambda ti: (0, 0))
    masks = _chunk_masks()
    return pl.pallas_call(
        functools.partial(_rwkv_chunk_kernel, nc=nc),
        grid=(s // tb,),
        in_specs=[tile, tile, tile, tile, tile,
                  pl.BlockSpec((nc, 1, w), lambda ti: (ti, 0, 0)),
                  tile, tile, rowspec, rowspec,
                  pl.BlockSpec(masks.shape, lambda ti: (0, 0, 0))],
        out_specs=tile,
        out_shape=jax.ShapeDtypeStruct((s, w), BF16),
        scratch_shapes=[pltpu.VMEM((w // GROUP_W, GROUP_W, GROUP_W), F32)],
        compiler_params=_cparams(1),
        name="rwkv_chunks",
    )(at, bt, kt, rt, v, wc, bonus, g, gn_gain.reshape(1, -1).astype(F32),
      gn_bias.reshape(1, -1).astype(F32), masks)


def _proj_rope_kernel(pos_ref, invf_ref, x_ref, w_ref, o_ref, cos_ref, sin_ref, *, bm, bn):
    lane = lax.broadcasted_iota(jnp.int32, (bm, LANES), 1)
    first_half = (lane & (HEAD_DIM - 1)) < (HEAD_DIM // 2)

    @pl.when(pl.program_id(1) == 0)
    def _():
        ang = pos_ref[...].astype(F32) * invf_ref[...]
        cos_ref[...] = jnp.cos(ang)
        sin = jnp.sin(ang)
        sin_ref[...] = jnp.where(first_half, -sin, sin)

    acc = jnp.dot(x_ref[...], w_ref[...], preferred_element_type=F32)
    cos = cos_ref[...]
    sin = sin_ref[...]
    for c in range(bn // LANES):
        t = acc[:, c * LANES:(c + 1) * LANES]
        partner = jnp.where(first_half, pltpu.roll(t, LANES - HEAD_DIM // 2, axis=1),
                            pltpu.roll(t, HEAD_DIM // 2, axis=1))
        o_ref[:, c * LANES:(c + 1) * LANES] = (t * cos + partner * sin).astype(o_ref.dtype)


def _proj_rope(xb, w_all, positions, *, bm, bn, n, col0):
    s, k = xb.shape
    assert n % bn == 0 and col0 % bn == 0
    jb = col0 // bn
    half = jnp.arange(0, HEAD_DIM, 2, dtype=F32)
    inv_freq = ROPE_THETA ** (-half / HEAD_DIM)
    invf = jnp.tile(inv_freq, 2 * LANES // HEAD_DIM).reshape(1, LANES)
    return pl.pallas_call(
        functools.partial(_proj_rope_kernel, bm=bm, bn=bn),
        grid=(s // bm, n // bn),
        in_specs=[pl.BlockSpec((bm, 1), lambda i, j: (i, 0)),
                  pl.BlockSpec((1, LANES), lambda i, j: (0, 0)),
                  pl.BlockSpec((bm, k), lambda i, j: (i, 0)),
                  pl.BlockSpec((k, bn), lambda i, j: (0, jb + j))],
        out_specs=pl.BlockSpec((bm, bn), lambda i, j: (i, j)),
        out_shape=jax.ShapeDtypeStruct((s, n), BF16),
        scratch_shapes=[pltpu.VMEM((bm, LANES), F32), pltpu.VMEM((bm, LANES), F32)],
        compiler_params=_cparams(2),
        name="proj_qk_rope",
    )(positions.reshape(s, 1), invf, xb, w_all)


ATT_STRIP = 128


ATT_KB = 512


def _attn_kernel(lam_ref, gain_ref, q_ref, k_ref, vt_ref, o_ref, acc_ref, s0_ref, s1_ref, *, bq):
    i = pl.program_id(1)
    nq = 2 * bq
    kb = ATT_KB
    q = q_ref[...]
    lane = lax.broadcasted_iota(jnp.int32, q.shape, 1)
    zero = jnp.zeros_like(q)
    q_all = jnp.concatenate([jnp.where(lane < HEAD_DIM, q, zero),
                             jnp.where(lane >= HEAD_DIM, q, zero)], axis=0)
    s_slots = (s0_ref, s1_ref)
    all_cols = ((0, nq),)
    late_cols = ((kb, bq), (bq + kb, nq))

    def scores(j, slot):
        start = pl.multiple_of(j * kb, kb)
        s_slots[slot][...] = lax.dot_general(k_ref[pl.ds(start, kb), :], q_all,
                                             (((1,), (1,)), ((), ())), preferred_element_type=F32)

    def consume(j, slot, m, l, groups, key0):
        start = pl.multiple_of(j * kb, kb)
        vtb = vt_ref[:, pl.ds(start, kb)]
        m_parts = {c: m[:, c * ATT_STRIP:(c + 1) * ATT_STRIP] for c in range(nq // ATT_STRIP)}
        l_parts = {c: l[:, c * ATT_STRIP:(c + 1) * ATT_STRIP] for c in range(nq // ATT_STRIP)}
        for lo, hi in groups:
            p_strips, alphas = [], []
            for c in range(lo // ATT_STRIP, hi // ATT_STRIP):
                sl = slice(c * ATT_STRIP, (c + 1) * ATT_STRIP)
                s = s_slots[slot][:, sl]
                q_first = (c * ATT_STRIP) % bq
                if key0 is not None and key0 + kb - 1 > q_first:
                    key = lax.broadcasted_iota(jnp.int32, s.shape, 0) + key0
                    qry = lax.broadcasted_iota(jnp.int32, s.shape, 1) + q_first
                    s = jnp.where(key <= qry, s, -1e30)
                mn = jnp.maximum(m_parts[c], jnp.max(s, axis=0, keepdims=True))
                p = jnp.exp2(s - mn)
                alpha = jnp.exp2(m_parts[c] - mn)
                l_parts[c] = alpha * l_parts[c] + jnp.sum(p, axis=0, keepdims=True)
                m_parts[c] = mn
                alphas.append(alpha)
                p_strips.append(p.astype(BF16))
            p_all = jnp.concatenate(p_strips, axis=1)
            alpha = jnp.concatenate(alphas, axis=1)
            acc_ref[:, lo:hi] = alpha * acc_ref[:, lo:hi] + jnp.dot(vtb, p_all,
                                                                    preferred_element_type=F32)
        order = range(nq // ATT_STRIP)
        return (jnp.concatenate([m_parts[c] for c in order], axis=1),
                jnp.concatenate([l_parts[c] for c in order], axis=1))

    def pair(jj, carry):
        j = 2 * jj
        scores(j + 1, 1)
        m, l = consume(j, 0, carry[0], carry[1], all_cols, None)
        scores(j + 2, 0)
        return consume(j + 1, 1, m, l, all_cols, None)

    acc_ref[...] = jnp.zeros_like(acc_ref)
    scores(0, 0)
    init = (jnp.full((1, nq), -1e30, F32), jnp.zeros((1, nq), F32))
    m, l = lax.fori_loop(0, i, pair, init)
    scores(2 * i + 1, 1)
    m, l = consume(2 * i, 0, m, l, all_cols, 0)
    m, l = consume(2 * i + 1, 1, m, l, late_cols, kb)

    lam_v = lam_ref[...]
    lam = (jnp.exp(jnp.sum(lam_v[0:1] * lam_v[1:2], axis=1, keepdims=True))
           - jnp.exp(jnp.sum(lam_v[2:3] * lam_v[3:4], axis=1, keepdims=True)) + LAMBDA_INIT)
    o = acc_ref[...] / l
    out = o[:, :bq] - lam * o[:, bq:]
    out = out * lax.rsqrt(jnp.mean(jnp.square(out), axis=0, keepdims=True) + SUBLN_EPS)
    out = out * gain_ref[...] * (1.0 - LAMBDA_INIT)
    o_ref[...] = out.T.astype(o_ref.dtype)


def _attention(qk, vt, lam_vecs, subln_gain, *, bq):
    s = qk.shape[0]
    assert s % bq == 0 and bq == 2 * ATT_KB
    return pl.pallas_call(
        functools.partial(_attn_kernel, bq=bq),
        grid=(DIFF_HEADS, s // bq),
        in_specs=[pl.BlockSpec((4, HEAD_DIM), lambda h, i: (0, 0)),
                  pl.BlockSpec((DIFF_V_DIM, 1), lambda h, i: (0, 0)),
                  pl.BlockSpec((bq, DIFF_V_DIM), lambda h, i: (i, h)),
                  pl.BlockSpec((s, DIFF_V_DIM), lambda h, i: (0, DIFF_HEADS + h)),
                  pl.BlockSpec((DIFF_V_DIM, s), lambda h, i: (h, 0))],
        out_specs=pl.BlockSpec((bq, DIFF_V_DIM), lambda h, i: (i, h)),
        out_shape=jax.ShapeDtypeStruct((s, DIFF_WIDTH), BF16),
        scratch_shapes=[pltpu.VMEM((DIFF_V_DIM, 2 * bq), F32), pltpu.VMEM((ATT_KB, 2 * bq), F32),
                        pltpu.VMEM((ATT_KB, 2 * bq), F32)],
        compiler_params=_cparams(2),
        name="diff_attn",
    )(lam_vecs, subln_gain.reshape(-1, 1).astype(F32), qk, qk, vt)


def _layer_norm(z, gain, bias):
    mean = jnp.mean(z, axis=-1, keepdims=True)
    zc = z - mean
    var = jnp.mean(zc * zc, axis=-1, keepdims=True)
    return zc * lax.rsqrt(var + LN_EPS) * gain + bias


LN_ROWS = 128


def _outproj_kernel(yr_ref, yd_ref, wr_ref, wd_ref, x_ref, gain_ref, bias_ref, o32_ref, o16_ref):
    for r in range(x_ref.shape[0] // LN_ROWS):
        rows = slice(r * LN_ROWS, (r + 1) * LN_ROWS)
        mix = (jnp.dot(yr_ref[rows, :], wr_ref[...], preferred_element_type=F32)
               + jnp.dot(yd_ref[rows, :], wd_ref[...], preferred_element_type=F32))
        y = _layer_norm(DEEPNORM_ALPHA * x_ref[rows, :] + mix, gain_ref[...], bias_ref[...])
        o32_ref[rows, :] = y
        o16_ref[rows, :] = y.astype(BF16)


def _outproj(y_rwkv, y_diff, w_r, w_d, x, gain, bias, *, bm):
    s = x.shape[0]
    const = lambda shape: pl.BlockSpec(shape, lambda i: (0, 0))
    return pl.pallas_call(
        _outproj_kernel,
        grid=(s // bm,),
        in_specs=[pl.BlockSpec((bm, RWKV_WIDTH), lambda i: (i, 0)),
                  pl.BlockSpec((bm, DIFF_WIDTH), lambda i: (i, 0)),
                  const((RWKV_WIDTH, D_MODEL)), const((DIFF_WIDTH, D_MODEL)),
                  pl.BlockSpec((bm, D_MODEL), lambda i: (i, 0)),
                  const((1, D_MODEL)), const((1, D_MODEL))],
        out_specs=[pl.BlockSpec((bm, D_MODEL), lambda i: (i, 0))] * 2,
        out_shape=[jax.ShapeDtypeStruct((s, D_MODEL), F32), jax.ShapeDtypeStruct((s, D_MODEL), BF16)],
        compiler_params=_cparams(1),
        name="outproj_ln",
    )(y_rwkv, y_diff, w_r, w_d, x, gain.reshape(1, -1), bias.reshape(1, -1))


def _ff2_kernel(h_ref, w_ref, x_ref, gain_ref, bias_ref, o_ref):
    kk = pl.program_id(1)
    part = lambda: jnp.dot(h_ref[...], w_ref[...], preferred_element_type=F32)

    @pl.when(kk == 0)
    def _():
        o_ref[...] = part()

    @pl.when((kk > 0) & (kk < pl.num_programs(1) - 1))
    def _():
        o_ref[...] += part()

    @pl.when(kk == pl.num_programs(1) - 1)
    def _():
        for r in range(o_ref.shape[0] // LN_ROWS):
            rows = slice(r * LN_ROWS, (r + 1) * LN_ROWS)
            acc = o_ref[rows, :] + jnp.dot(h_ref[rows, :], w_ref[...], preferred_element_type=F32)
            o_ref[rows, :] = _layer_norm(DEEPNORM_ALPHA * x_ref[rows, :] + acc,
                                         gain_ref[...], bias_ref[...])


def _ff2(h, w, x, gain, bias, *, bm, bk):
    s, kdim = h.shape
    assert s % bm == 0 and kdim % bk == 0 and kdim // bk >= 2
    const = lambda shape: pl.BlockSpec(shape, lambda i, k: (0, 0))
    return pl.pallas_call(
        _ff2_kernel,
        grid=(s // bm, kdim // bk),
        in_specs=[pl.BlockSpec((bm, bk), lambda i, k: (i, k)),
                  pl.BlockSpec((bk, D_MODEL), lambda i, k: (k, 0)),
                  pl.BlockSpec((bm, D_MODEL), lambda i, k: (i, 0)),
                  const((1, D_MODEL)), const((1, D_MODEL))],
        out_specs=pl.BlockSpec((bm, D_MODEL), lambda i, k: (i, 0)),
        out_shape=jax.ShapeDtypeStruct((s, D_MODEL), F32),
        compiler_params=_cparams(2),
        name="ff2_ln",
    )(h, w, x, gain.reshape(1, -1), bias.reshape(1, -1))


def _pad_rows(w, rows):
    return jnp.pad(w, ((0, rows - w.shape[0]), (0, 0)))


def _layer(x, positions, w_in, mu_shift, w0, w_up, a0, a_up, g_up, k_k, k_a, r_k, gn_gain, gn_bias,
           lq1, lk1, lq2, lk2, subln_gain, w_out, ln1_gain, ln1_bias, w_ff1, w_ff2, ln2_gain, ln2_bias,
           *, bm, prep_tb, nc, bq):
    w3 = 3 * RWKV_WIDTH
    rwkv_cols = w3 + DECAY_LORA + AAA_LORA + GATE_LORA
    qk0 = RWKV_PCOLS
    v0 = qk0 + 2 * DIFF_WIDTH

    def pad_cols(a, lo, hi, width):
        return jnp.pad(a[..., lo:hi], [(0, 0)] * (a.ndim - 1) + [(0, width - (hi - lo))])

    def rwkv_layout(a):
        return jnp.concatenate([
            a[..., :w3],
            pad_cols(a, w3, w3 + DECAY_LORA, DECAY_PAD),
            pad_cols(a, w3 + DECAY_LORA, w3 + DECAY_LORA + AAA_LORA, AAA_PAD),
            pad_cols(a, w3 + DECAY_LORA + AAA_LORA, rwkv_cols, GATE_PAD)], axis=-1)

    w_all = _relayout_w_in(w_in, bk=256)
    proj_r, xb = _proj_cast(x, w_all, bm=bm, bn=512, n=RWKV_PCOLS)
    vt = _matmul(xb, w_all, bm=bm, bn=512, n=DIFF_WIDTH, col0=v0, out_dtype=BF16,
                 transpose_out=True, name="proj_vt")

    at, bt, kt, rt, v_r, wc, bonus, gate = _rwkv_prep(
        proj_r, rwkv_layout(mu_shift), w0,
        _pad_rows(w_up, DECAY_PAD).astype(BF16), a0, _pad_rows(a_up, AAA_PAD).astype(BF16),
        _pad_rows(g_up, GATE_PAD).astype(BF16), k_k, k_a, r_k.reshape(-1), tb=prep_tb)
    y_rwkv = _rwkv_chunks(at, bt, kt, rt, v_r, wc, bonus, gate, gn_gain, gn_bias, nc=nc)

    qk = _proj_rope(xb, w_all, positions, bm=bm, bn=512, n=2 * DIFF_WIDTH, col0=qk0)
    lam_vecs = jnp.stack([lq1, lk1, lq2, lk2]).astype(F32)
    y_diff = _attention(qk, vt, lam_vecs, subln_gain, bq=bq)

    wo = w_out.astype(BF16)
    x1, x1b = _outproj(y_rwkv, y_diff, wo[:RWKV_WIDTH], wo[RWKV_WIDTH:], x, ln1_gain, ln1_bias, bm=512)
    h = _ff1(x1b, w_ff1, bm=bm, bn=1024)
    return _ff2(h, w_ff2.astype(BF16), x1, ln2_gain, ln2_bias, bm=1024, bk=1024)


def kernel(x, positions, w_in, mu_shift, rwkv_w0, rwkv_w_up, rwkv_a0, rwkv_a_up, rwkv_g_up, rwkv_k_k,
           rwkv_k_a, rwkv_r_k, rwkv_gn_gain, rwkv_gn_bias, diff_lambda_q1, diff_lambda_k1,
           diff_lambda_q2, diff_lambda_k2, diff_subln_gain, w_out, ln1_gain, ln1_bias, w_ff1, w_ff2,
           ln2_gain, ln2_bias):
    batch, seq, _ = x.shape
    assert batch == 1 and w_in.shape[0] == 1
    out = _layer(
        x[0], positions[0], w_in[0], mu_shift[0], rwkv_w0[0], rwkv_w_up[0], rwkv_a0[0], rwkv_a_up[0],
        rwkv_g_up[0], rwkv_k_k[0], rwkv_k_a[0], rwkv_r_k[0], rwkv_gn_gain[0], rwkv_gn_bias[0],
        diff_lambda_q1[0], diff_lambda_k1[0], diff_lambda_q2[0], diff_lambda_k2[0],
        diff_subln_gain[0], w_out[0], ln1_gain[0], ln1_bias[0], w_ff1[0], w_ff2[0], ln2_gain[0],
        ln2_bias[0], bm=1024, prep_tb=256, nc=2, bq=1024)
    return out[None]
```

```python
import functools
import math

import jax
import jax.numpy as jnp
from jax import lax
from jax.experimental import pallas as pl
from jax.experimental.pallas import tpu as pltpu

F32 = jnp.float32
BF16 = jnp.bfloat16

D_MODEL = 2048
RWKV_WIDTH = 1024
HEAD_DIM = 64
DECAY_LORA = 64
AAA_LORA = 64
GATE_LORA = 160
DIFF_WIDTH = 1024
DIFF_V_DIM = 128
DIFF_HEADS = 8
D_FF = 8192
ROPE_THETA = 10000.0
LN_EPS = 1e-5
RWKV_GN_EPS = 64e-5
SUBLN_EPS = 1e-5
DEEPNORM_ALPHA = 2.0 ** 0.25
LAMBDA_INIT = 0.8 - 0.6 * math.exp(0.0)
LOG2_E = math.log2(math.e)

LANES = 128
SUBLANES = 8
VMEM_LIMIT_BYTES = 56 * 1024 * 1024

DECAY_PAD = 128
AAA_PAD = 128
GATE_PAD = 256
RWKV_PCOLS = 3 * RWKV_WIDTH + DECAY_PAD + AAA_PAD + GATE_PAD
CHUNK = 64
GROUP_HEADS = 4
GROUP_W = GROUP_HEADS * HEAD_DIM


def _cparams(n_axes):
    return pltpu.CompilerParams(
        dimension_semantics=("arbitrary",) * n_axes,
        vmem_limit_bytes=VMEM_LIMIT_BYTES,
    )


def _dot(a, b):
    return jnp.dot(a.astype(BF16), b.astype(BF16), preferred_element_type=F32)


def _dot_nt(a, b):
    return lax.dot_general(a.astype(BF16), b.astype(BF16), (((1,), (1,)), ((), ())),
                           preferred_element_type=F32)


def _dot_tn(a, b):
    return lax.dot_general(a.astype(BF16), b.astype(BF16), (((0,), (0,)), ((), ())),
                           preferred_element_type=F32)


def _split3(x):
    h1 = x.astype(BF16)
    r1 = x - h1.astype(F32)
    h2 = r1.astype(BF16)
    h3 = (r1 - h2.astype(F32)).astype(BF16)
    return h1, h2, h3


def _split2(x):
    h1 = x.astype(BF16)
    h2 = (x - h1.astype(F32)).astype(BF16)
    return h1, h2


def _mm_kernel(a_ref, b_ref, o_ref, *, transpose_out):
    acc = jnp.dot(a_ref[...], b_ref[...], preferred_element_type=F32)
    if transpose_out:
        acc = acc.T
    o_ref[...] = acc.astype(o_ref.dtype)


def _matmul(a, b, *, bm, bn, n, col0=0, out_dtype, transpose_out=False, name):
    m, k = a.shape
    assert m % bm == 0 and n % bn == 0 and col0 % bn == 0
    jb = col0 // bn
    if transpose_out:
        out_spec = pl.BlockSpec((bn, bm), lambda i, j: (j, i))
        out_shape = jax.ShapeDtypeStruct((n, m), out_dtype)
    else:
        out_spec = pl.BlockSpec((bm, bn), lambda i, j: (i, j))
        out_shape = jax.ShapeDtypeStruct((m, n), out_dtype)
    return pl.pallas_call(
        functools.partial(_mm_kernel, transpose_out=transpose_out),
        grid=(m // bm, n // bn),
        in_specs=[pl.BlockSpec((bm, k), lambda i, j: (i, 0)),
                  pl.BlockSpec((k, bn), lambda i, j: (0, jb + j))],
        out_specs=out_spec,
        out_shape=out_shape,
        compiler_params=_cparams(2),
        name=name,
    )(a, b)


def _proj_cast_kernel(x_ref, w_ref, o_ref, xb_ref):
    @pl.when(pl.program_id(1) == 0)
    def _():
        xb_ref[...] = x_ref[...].astype(BF16)

    o_ref[...] = jnp.dot(xb_ref[...], w_ref[...], preferred_element_type=F32)


def _proj_cast(x, w_all, *, bm, bn, n):
    m, k = x.shape
    assert m % bm == 0 and n % bn == 0
    return pl.pallas_call(
        _proj_cast_kernel,
        grid=(m // bm, n // bn),
        in_specs=[pl.BlockSpec((bm, k), lambda i, j: (i, 0)),
                  pl.BlockSpec((k, bn), lambda i, j: (0, j))],
        out_specs=[pl.BlockSpec((bm, bn), lambda i, j: (i, j)),
                   pl.BlockSpec((bm, k), lambda i, j: (i, 0))],
        out_shape=[jax.ShapeDtypeStruct((m, n), F32), jax.ShapeDtypeStruct((m, k), BF16)],
        compiler_params=_cparams(2),
        name="proj_rwkv",
    )(x, w_all)


def _ff1_kernel(x_ref, w_ref, o_ref, wb_ref):
    @pl.when(pl.program_id(1) == 0)
    def _():
        wb_ref[...] = w_ref[...].astype(BF16)

    acc = jnp.dot(x_ref[...], wb_ref[...], preferred_element_type=F32)
    o_ref[...] = jnp.square(jnp.maximum(acc, 0.0)).astype(o_ref.dtype)


def _ff1(x, w, *, bm, bn):
    m, k = x.shape
    _, n = w.shape
    assert m % bm == 0 and n % bn == 0
    return pl.pallas_call(
        _ff1_kernel,
        grid=(n // bn, m // bm),
        in_specs=[pl.BlockSpec((bm, k), lambda j, i: (i, 0)),
                  pl.BlockSpec((k, bn), lambda j, i: (0, j))],
        out_specs=pl.BlockSpec((bm, bn), lambda j, i: (i, j)),
        out_shape=jax.ShapeDtypeStruct((m, n), BF16),
        scratch_shapes=[pltpu.VMEM((k, bn), BF16)],
        compiler_params=_cparams(2),
        name="ff1",
    )(x, w)


def _relayout_kernel(wt_ref, o_ref):
    cols = wt_ref.shape[1]
    w3 = 3 * RWKV_WIDTH
    a0 = w3 + DECAY_LORA
    g0 = a0 + AAA_LORA
    q0 = g0 + GATE_LORA
    zeros = lambda n: jnp.zeros((n, cols), F32)
    pieces = [wt_ref[:a0, :], zeros(DECAY_PAD - DECAY_LORA),
              wt_ref[a0:g0, :], zeros(AAA_PAD - AAA_LORA),
              wt_ref[g0:q0, :], zeros(GATE_PAD - GATE_LORA),
              wt_ref[q0:q0 + DIFF_WIDTH, :] * (HEAD_DIM ** -0.5 * LOG2_E),
              wt_ref[q0 + DIFF_WIDTH:, :]]
    for c0 in range(0, o_ref.shape[1], RELAYOUT_COLS):
        rows = slice(c0, c0 + RELAYOUT_COLS)
        tile = jnp.concatenate(_row_window(pieces, c0, c0 + RELAYOUT_COLS), axis=0)
        o_ref[:, rows] = tile.T.astype(o_ref.dtype)


def _row_window(pieces, lo, hi):
    out, start = [], 0
    for p in pieces:
        stop = start + p.shape[0]
        a, b = max(lo, start), min(hi, stop)
        if a < b:
            out.append(p[a - start:b - start])
        start = stop
    return out


RELAYOUT_COLS = 512


def _relayout_w_in(w_in_t, *, bk):
    n, k = w_in_t.shape
    n_out = RWKV_PCOLS + 3 * DIFF_WIDTH
    assert n == n_out - (RWKV_PCOLS - 3 * RWKV_WIDTH - DECAY_LORA - AAA_LORA - GATE_LORA)
    assert n_out % RELAYOUT_COLS == 0
    return pl.pallas_call(
        _relayout_kernel,
        grid=(k // bk,),
        in_specs=[pl.BlockSpec((n, bk), lambda i: (0, i))],
        out_specs=pl.BlockSpec((bk, n_out), lambda i: (i, 0)),
        out_shape=jax.ShapeDtypeStruct((k, n_out), BF16),
        compiler_params=_cparams(1),
        name="relayout_w_in",
    )(w_in_t)


def _rwkv_prep_kernel(p_ref, mu_ref, w0_ref, wup_ref, a0_ref, aup_ref, gup_ref, kk_ref, ka_ref,
                      rk_ref, segin_ref, segout_ref, tri_ref,
                      at_ref, bt_ref, kt_ref, rt_ref, v_ref, wc_ref, bonus_ref, g_ref,
                      carry_ref, *, tb):
    i = pl.program_id(0)

    @pl.when(i == 0)
    def _():
        carry_ref[...] = jnp.zeros_like(carry_ref)

    def mixed(lo, hi):
        p = p_ref[:, lo:hi]
        prev_last = carry_ref[0:1, lo:hi]
        shifted = pltpu.roll(p, 1, axis=0)
        row = lax.broadcasted_iota(jnp.int32, p.shape, 0)
        prev = jnp.where(row == 0, prev_last, shifted)
        carry_ref[0:1, lo:hi] = p[tb - 1:tb, :]
        return p + (prev - p) * mu_ref[:, lo:hi]

    w3 = 3 * RWKV_WIDTH
    r = mixed(0, RWKV_WIDTH)
    k = mixed(RWKV_WIDTH, 2 * RWKV_WIDTH)
    v = mixed(2 * RWKV_WIDTH, w3)
    w_d = mixed(w3, w3 + DECAY_PAD)
    a_d = mixed(w3 + DECAY_PAD, w3 + DECAY_PAD + AAA_PAD)
    g_d = mixed(w3 + DECAY_PAD + AAA_PAD, RWKV_PCOLS)

    z = w0_ref[...] + _dot(jnp.tanh(w_d), wup_ref[...])
    nz = -z
    softplus = jnp.maximum(nz, 0.0) + jnp.log1p(jnp.exp(-jnp.abs(nz)))
    lw = -jnp.exp(-softplus - 0.5)
    a = jax.nn.sigmoid(a0_ref[...] + _dot(a_d, aup_ref[...]))
    g_ref[...] = _dot(jax.nn.sigmoid(g_d), gup_ref[...])

    def segsum(x):
        h1, h2 = _split2(jnp.dot(x.astype(BF16), segin_ref[...], preferred_element_type=F32))
        return (jnp.dot(h1, segout_ref[...], preferred_element_type=F32)
                + jnp.dot(h2, segout_ref[...], preferred_element_type=F32))

    kk0 = k * kk_ref[...]
    kk = kk0 / jnp.maximum(jnp.sqrt(segsum(kk0 * kk0)), 1e-12)
    k2 = k * (1.0 + (a - 1.0) * ka_ref[...])
    bonus_ref[...] = segsum(r * k2 * rk_ref[...]) * v

    tri = tri_ref[...]
    l1, l2, l3 = _split3(lw)
    cs = (jnp.dot(tri, l1, preferred_element_type=F32)
          + jnp.dot(tri, l2, preferred_element_type=F32)
          + jnp.dot(tri, l3, preferred_element_type=F32))
    cl = cs[:tb]
    tot = cs[tb:]
    e_neg = jnp.exp(-cl)
    at_ref[...] = (-kk * jnp.exp(cl - lw)).astype(BF16)
    bt_ref[...] = (kk * a * e_neg).astype(BF16)
    kt_ref[...] = (k2 * e_neg).astype(BF16)
    rt_ref[...] = (r * jnp.exp(cl)).astype(BF16)
    v_ref[...] = v.astype(BF16)
    for c in range(tb // CHUNK):
        wc_ref[c] = jnp.exp(tot[c * CHUNK:c * CHUNK + 1, :])


def _rwkv_prep(proj_r, mu, w0, wup, a0, aup, gup, k_k, k_a, r_k, *, tb):
    s = proj_r.shape[0]
    assert s % tb == 0 and tb % CHUNK == 0
    w = RWKV_WIDTH
    lane = jnp.arange(w)
    seg_in = (lane[:, None] // HEAD_DIM == jnp.arange(LANES)[None, :]).astype(BF16)
    t = jnp.arange(tb)
    same = t[:, None] // CHUNK == t[None, :] // CHUNK
    tri = jnp.concatenate([(same & (t[:, None] >= t[None, :])), same], axis=0).astype(BF16)
    row = lambda x: x.reshape(1, -1).astype(F32)
    const = lambda shape: pl.BlockSpec(shape, lambda i: (0,) * len(shape))
    tile = lambda width: pl.BlockSpec((tb, width), lambda i: (i, 0))
    outs = pl.pallas_call(
        functools.partial(_rwkv_prep_kernel, tb=tb),
        grid=(s // tb,),
        in_specs=[tile(RWKV_PCOLS), const((1, RWKV_PCOLS)), const((1, w)), const((DECAY_PAD, w)),
                  const((1, w)), const((AAA_PAD, w)), const((GATE_PAD, w)), const((1, w)),
                  const((1, w)), const((1, w)), const((w, LANES)), const((LANES, w)),
                  const((2 * tb, tb))],
        out_specs=[tile(w), tile(w), tile(w), tile(w), tile(w),
                   pl.BlockSpec((tb // CHUNK, 1, w), lambda i: (i, 0, 0)),
                   tile(w), tile(w)],
        out_shape=[jax.ShapeDtypeStruct((s, w), BF16)] * 5
        + [jax.ShapeDtypeStruct((s // CHUNK, 1, w), F32),
           jax.ShapeDtypeStruct((s, w), F32), jax.ShapeDtypeStruct((s, w), F32)],
        scratch_shapes=[pltpu.VMEM((SUBLANES, RWKV_PCOLS), F32)],
        compiler_params=_cparams(1),
        name="rwkv_prep",
    )(proj_r, row(mu), row(w0), wup, row(a0), aup, gup, row(k_k), row(k_a), row(r_k), seg_in, seg_in.T, tri)
    return outs


def _bdot(a, b):
    return jnp.dot(a, b, preferred_element_type=F32)


def _bdot_nt(a, b):
    return lax.dot_general(a, b, (((1,), (1,)), ((), ())), preferred_element_type=F32)


def _bdot_tn(a, b):
    return lax.dot_general(a, b, (((0,), (0,)), ((), ())), preferred_element_type=F32)


(MASK_STRICT, MASK_INCL, MASK_EYE, MASK_BLK8, MASK_OFF16, MASK_OFF32, MASK_OFF64,
 MASK_HEAD) = range(8)


def _chunk_masks():
    n = GROUP_W
    r = jnp.arange(n)[:, None]
    c = jnp.arange(n)[None, :]
    head = (r // CHUNK) == (c // CHUNK)
    t_r, t_c = r % CHUNK, c % CHUNK
    blk = lambda size: (r // size) == (c // size)
    masks = [head & (t_r > t_c), head & (t_r >= t_c), r == c, blk(8),
             blk(16) & ~blk(8), blk(32) & ~blk(16), blk(64) & ~blk(32), head]
    return jnp.stack(masks).astype(F32)


def _rwkv_chunk_kernel(at_ref, bt_ref, kt_ref, rt_ref, v_ref, wc_ref, bonus_ref, g_ref,
                       gain_ref, bias_ref, mask_ref, o_ref, state_ref, *, nc):
    @pl.when(pl.program_id(0) == 0)
    def _():
        state_ref[...] = jnp.zeros_like(state_ref)

    n = GROUP_W
    ng = RWKV_WIDTH // GROUP_W
    items = [(c, g) for c in range(nc) for g in range(ng)]
    lane_head = lax.broadcasted_iota(jnp.int32, (CHUNK, n), 1) >> 6

    def tile(ref, c, g):
        return ref[c * CHUNK:(c + 1) * CHUNK, g * n:(g + 1) * n]

    def expand(x):
        zero = jnp.zeros_like(x)
        return jnp.concatenate([jnp.where(lane_head == h, x, zero) for h in range(GROUP_HEADS)],
                               axis=0)

    def repeat(x):
        return jnp.concatenate([x] * GROUP_HEADS, axis=0)

    mask = lambda k: mask_ref[k]
    bf = lambda x: x.astype(BF16)
    each = lambda f: [f(*it) for it in items]
    zipped = lambda f, *lists: [f(*args) for args in zip(*lists)]

    at = each(lambda c, g: tile(at_ref, c, g))
    bt = each(lambda c, g: tile(bt_ref, c, g))
    kt = each(lambda c, g: tile(kt_ref, c, g))
    wc = each(lambda c, g: wc_ref[c][:, g * n:(g + 1) * n])
    ae = [expand(x) for x in at]
    be = [expand(x) for x in bt]
    ke = [expand(x) for x in kt]
    ve = each(lambda c, g: expand(tile(v_ref, c, g)))
    at4 = [repeat(x) for x in at]
    rt4 = each(lambda c, g: repeat(tile(rt_ref, c, g)))
    bhe = zipped(lambda x, w: expand(bf(x.astype(F32) * w)), bt, wc)
    khe = zipped(lambda x, w: expand(bf(x.astype(F32) * w)), kt, wc)

    lab = zipped(lambda a, b: _bdot_nt(a, b) * mask(MASK_STRICT), at4, be)
    lak = zipped(lambda a, b: bf(_bdot_nt(a, b) * mask(MASK_STRICT)), at4, ke)
    arb = zipped(lambda a, b: bf(_bdot_nt(a, b) * mask(MASK_INCL)), rt4, be)
    ark = zipped(lambda a, b: bf(_bdot_nt(a, b) * mask(MASK_INCL)), rt4, ke)
    lv = zipped(lambda a, b: bf(_bdot(a, b)), lak, ve)

    l8 = [x * mask(MASK_BLK8) for x in lab]
    minv = [mask(MASK_EYE) + x for x in l8]
    pw = [bf(x) for x in l8]
    for _ in range(2):
        pw2 = [_bdot(x, x) for x in pw]
        pw = [bf(x) for x in pw2]
        minv = zipped(lambda p, m: m + _bdot(p, bf(m)), pw, minv)
    for k in (MASK_OFF16, MASK_OFF32, MASK_OFF64):
        off = [bf(x * mask(k)) for x in lab]
        mb = [bf(m) for m in minv]
        tmp = zipped(lambda m, o: bf(_bdot(m, o)), mb, off)
        minv = zipped(lambda m, t, m16: m + _bdot(t, m16), minv, tmp, mb)
    mb = [bf(m) for m in minv]

    p = zipped(lambda m, a: bf(_bdot(m, a)), mb, ae)
    q = zipped(lambda m, x: bf(_bdot(m, x)), mb, lv)
    g_mat = zipped(lambda r4, a, x: bf(r4.astype(F32) * mask(MASK_HEAD) + _bdot(a, x)), rt4, arb, p)
    y0 = zipped(lambda a, x, b, v: _bdot(a, x) + _bdot(b, v), arb, q, ark, ve)
    phi = zipped(lambda w, b, x: bf(mask(MASK_EYE) * w + _bdot_tn(b, x)), wc, bhe, p)
    psi = zipped(lambda b, x, k, v: _bdot_tn(b, x) + _bdot_tn(k, v), bhe, q, khe, ve)

    state = [state_ref[g] for g in range(ng)]
    for c in range(nc):
        idx = [c * ng + g for g in range(ng)]
        sb = [bf(s) for s in state]
        ys = [_bdot(g_mat[i], sb[g]) + y0[i] for g, i in enumerate(idx)]
        state = [_bdot(phi[i], sb[g]) + psi[i] for g, i in enumerate(idx)]
        for g, y in enumerate(ys):
            mean = jnp.sum(y, axis=1, keepdims=True) * (1.0 / HEAD_DIM)
            yc = (y - mean) * mask(MASK_HEAD)
            var = jnp.sum(yc * yc, axis=1, keepdims=True) * (1.0 / HEAD_DIM)
            yn = yc * lax.rsqrt(var + RWKV_GN_EPS)
            yw = (yn[0:CHUNK] + yn[CHUNK:2 * CHUNK] + yn[2 * CHUNK:3 * CHUNK]
                  + yn[3 * CHUNK:4 * CHUNK])
            rows = slice(c * CHUNK, (c + 1) * CHUNK)
            cols = slice(g * n, (g + 1) * n)
            out = (yw * gain_ref[:, cols] + bias_ref[:, cols] + bonus_ref[rows, cols]) * g_ref[rows, cols]
            o_ref[rows, cols] = out.astype(o_ref.dtype)
    for g in range(ng):
        state_ref[g] = state[g]


def _rwkv_chunks(at, bt, kt, rt, v, wc, bonus, g, gn_gain, gn_bias, *, nc):
    s, w = at.shape
    tb = nc * CHUNK
    assert s % tb == 0 and w % GROUP_W == 0
    tile = pl.BlockSpec((tb, w), lambda ti: (ti, 0))
    rowspec = pl.BlockSpec((1, w), lambda ti: (0, 0))
    masks = _chunk_masks()
    return pl.pallas_call(
        functools.partial(_rwkv_chunk_kernel, nc=nc),
        grid=(s // tb,),
        in_specs=[tile, tile, tile, tile, tile,
                  pl.BlockSpec((nc, 1, w), lambda ti: (ti, 0, 0)),
                  tile, tile, rowspec, rowspec,
                  pl.BlockSpec(masks.shape, lambda ti: (0, 0, 0))],
        out_specs=tile,
        out_shape=jax.ShapeDtypeStruct((s, w), BF16),
        scratch_shapes=[pltpu.VMEM((w // GROUP_W, GROUP_W, GROUP_W), F32)],
        compiler_params=_cparams(1),
        name="rwkv_chunks",
    )(at, bt, kt, rt, v, wc, bonus, g, gn_gain.reshape(1, -1).astype(F32),
      gn_bias.reshape(1, -1).astype(F32), masks)


ROPE_ROWS = 256


def _proj_rope_kernel(pos_ref, invf_ref, x_ref, w_ref, o_ref, cos_ref, sin_ref, *, bm, bn):
    def first_half(rows):
        lane = lax.broadcasted_iota(jnp.int32, (rows, LANES), 1)
        return (lane & (HEAD_DIM - 1)) < (HEAD_DIM // 2)

    @pl.when(pl.program_id(1) == 0)
    def _():
        ang = pos_ref[...].astype(F32) * invf_ref[...]
        cos_ref[...] = jnp.cos(ang)
        sin = jnp.sin(ang)
        sin_ref[...] = jnp.where(first_half(bm), -sin, sin)

    half = first_half(ROPE_ROWS)

    for r in range(bm // ROPE_ROWS):
        rows = slice(r * ROPE_ROWS, (r + 1) * ROPE_ROWS)
        acc = jnp.dot(x_ref[rows, :], w_ref[...], preferred_element_type=F32)
        cos = cos_ref[rows, :]
        sin = sin_ref[rows, :]
        for c in range(bn // LANES):
            t = acc[:, c * LANES:(c + 1) * LANES]
            partner = jnp.where(half, pltpu.roll(t, LANES - HEAD_DIM // 2, axis=1),
                                pltpu.roll(t, HEAD_DIM // 2, axis=1))
            o_ref[rows, c * LANES:(c + 1) * LANES] = (t * cos + partner * sin).astype(o_ref.dtype)


def _proj_rope(xb, w_all, positions, *, bm, bn, n, col0):
    s, k = xb.shape
    assert n % bn == 0 and col0 % bn == 0
    jb = col0 // bn
    half = jnp.arange(0, HEAD_DIM, 2, dtype=F32)
    inv_freq = ROPE_THETA ** (-half / HEAD_DIM)
    invf = jnp.tile(inv_freq, 2 * LANES // HEAD_DIM).reshape(1, LANES)
    return pl.pallas_call(
        functools.partial(_proj_rope_kernel, bm=bm, bn=bn),
        grid=(s // bm, n // bn),
        in_specs=[pl.BlockSpec((bm, 1), lambda i, j: (i, 0)),
                  pl.BlockSpec((1, LANES), lambda i, j: (0, 0)),
                  pl.BlockSpec((bm, k), lambda i, j: (i, 0)),
                  pl.BlockSpec((k, bn), lambda i, j: (0, jb + j))],
        out_specs=pl.BlockSpec((bm, bn), lambda i, j: (i, j)),
        out_shape=jax.ShapeDtypeStruct((s, n), BF16),
        scratch_shapes=[pltpu.VMEM((bm, LANES), F32), pltpu.VMEM((bm, LANES), F32)],
        compiler_params=_cparams(2),
        name="proj_qk_rope",
    )(positions.reshape(s, 1), invf, xb, w_all)


ATT_STRIP = 128


ATT_KB = 512


def _attn_kernel(lam_ref, gain_ref, q_ref, k_ref, vt_ref, o_ref, acc_ref, s0_ref, s1_ref, *, bq):
    i = pl.program_id(1)
    nq = 2 * bq
    kb = ATT_KB
    q = q_ref[...]
    lane = lax.broadcasted_iota(jnp.int32, q.shape, 1)
    zero = jnp.zeros_like(q)
    q_all = jnp.concatenate([jnp.where(lane < HEAD_DIM, q, zero),
                             jnp.where(lane >= HEAD_DIM, q, zero)], axis=0)
    s_slots = (s0_ref, s1_ref)
    all_cols = ((0, nq),)
    late_cols = ((kb, bq), (bq + kb, nq))

    def scores(j, slot):
        start = pl.multiple_of(j * kb, kb)
        s_slots[slot][...] = lax.dot_general(k_ref[pl.ds(start, kb), :], q_all,
                                             (((1,), (1,)), ((), ())), preferred_element_type=F32)

    def consume(j, slot, m, l, groups, key0):
        start = pl.multiple_of(j * kb, kb)
        vtb = vt_ref[:, pl.ds(start, kb)]
        m_parts = {c: m[:, c * ATT_STRIP:(c + 1) * ATT_STRIP] for c in range(nq // ATT_STRIP)}
        l_parts = {c: l[:, c * ATT_STRIP:(c + 1) * ATT_STRIP] for c in range(nq // ATT_STRIP)}
        for lo, hi in groups:
            p_strips, alphas = [], []
            for c in range(lo // ATT_STRIP, hi // ATT_STRIP):
                sl = slice(c * ATT_STRIP, (c + 1) * ATT_STRIP)
                s = s_slots[slot][:, sl]
                q_first = (c * ATT_STRIP) % bq
                if key0 is not None and key0 + kb - 1 > q_first:
                    key = lax.broadcasted_iota(jnp.int32, s.shape, 0) + key0
                    qry = lax.broadcasted_iota(jnp.int32, s.shape, 1) + q_first
                    s = jnp.where(key <= qry, s, -1e30)
                mn = jnp.maximum(m_parts[c], jnp.max(s, axis=0, keepdims=True))
                p = jnp.exp2(s - mn)
                alpha = jnp.exp2(m_parts[c] - mn)
                l_parts[c] = alpha * l_parts[c] + jnp.sum(p, axis=0, keepdims=True)
                m_parts[c] = mn
                alphas.append(alpha)
                p_strips.append(p.astype(BF16))
            p_all = jnp.concatenate(p_strips, axis=1)
            alpha = jnp.concatenate(alphas, axis=1)
            acc_ref[:, lo:hi] = alpha * acc_ref[:, lo:hi] + jnp.dot(vtb, p_all,
                                                                    preferred_element_type=F32)
        order = range(nq // ATT_STRIP)
        return (jnp.concatenate([m_parts[c] for c in order], axis=1),
                jnp.concatenate([l_parts[c] for c in order], axis=1))

    def pair(jj, carry):
        j = 2 * jj
        scores(j + 1, 1)
        m, l = consume(j, 0, carry[0], carry[1], all_cols, None)
        scores(j + 2, 0)
        return consume(j + 1, 1, m, l, all_cols, None)

    acc_ref[...] = jnp.zeros_like(acc_ref)
    scores(0, 0)
    init = (jnp.full((1, nq), -1e30, F32), jnp.zeros((1, nq), F32))
    m, l = lax.fori_loop(0, i, pair, init)
    scores(2 * i + 1, 1)
    m, l = consume(2 * i, 0, m, l, all_cols, 0)
    m, l = consume(2 * i + 1, 1, m, l, late_cols, kb)

    lam_v = lam_ref[...]
    lam = (jnp.exp(jnp.sum(lam_v[0:1] * lam_v[1:2], axis=1, keepdims=True))
           - jnp.exp(jnp.sum(lam_v[2:3] * lam_v[3:4], axis=1, keepdims=True)) + LAMBDA_INIT)
    o = acc_ref[...] / l
    out = o[:, :bq] - lam * o[:, bq:]
    out = out * lax.rsqrt(jnp.mean(jnp.square(out), axis=0, keepdims=True) + SUBLN_EPS)
    out = out * gain_ref[...] * (1.0 - LAMBDA_INIT)
    o_ref[...] = out.T.astype(o_ref.dtype)


def _attention(qk, vt, lam_vecs, subln_gain, *, bq):
    s = qk.shape[0]
    assert s % bq == 0 and bq == 2 * ATT_KB
    return pl.pallas_call(
        functools.partial(_attn_kernel, bq=bq),
        grid=(DIFF_HEADS, s // bq),
        in_specs=[pl.BlockSpec((4, HEAD_DIM), lambda h, i: (0, 0)),
                  pl.BlockSpec((DIFF_V_DIM, 1), lambda h, i: (0, 0)),
                  pl.BlockSpec((bq, DIFF_V_DIM), lambda h, i: (i, h)),
                  pl.BlockSpec((s, DIFF_V_DIM), lambda h, i: (0, DIFF_HEADS + h)),
                  pl.BlockSpec((DIFF_V_DIM, s), lambda h, i: (h, 0))],
        out_specs=pl.BlockSpec((bq, DIFF_V_DIM), lambda h, i: (i, h)),
        out_shape=jax.ShapeDtypeStruct((s, DIFF_WIDTH), BF16),
        scratch_shapes=[pltpu.VMEM((DIFF_V_DIM, 2 * bq), F32), pltpu.VMEM((ATT_KB, 2 * bq), F32),
                        pltpu.VMEM((ATT_KB, 2 * bq), F32)],
        compiler_params=_cparams(2),
        name="diff_attn",
    )(lam_vecs, subln_gain.reshape(-1, 1).astype(F32), qk, qk, vt)


def _layer_norm(z, gain, bias):
    mean = jnp.mean(z, axis=-1, keepdims=True)
    zc = z - mean
    var = jnp.mean(zc * zc, axis=-1, keepdims=True)
    return zc * lax.rsqrt(var + LN_EPS) * gain + bias


LN_ROWS = 128


def _outproj_kernel(yr_ref, yd_ref, wr_ref, wd_ref, x_ref, gain_ref, bias_ref, o32_ref, o16_ref):
    for r in range(x_ref.shape[0] // LN_ROWS):
        rows = slice(r * LN_ROWS, (r + 1) * LN_ROWS)
        mix = (jnp.dot(yr_ref[rows, :], wr_ref[...], preferred_element_type=F32)
               + jnp.dot(yd_ref[rows, :], wd_ref[...], preferred_element_type=F32))
        y = _layer_norm(DEEPNORM_ALPHA * x_ref[rows, :] + mix, gain_ref[...], bias_ref[...])
        o32_ref[rows, :] = y
        o16_ref[rows, :] = y.astype(BF16)


def _outproj(y_rwkv, y_diff, w_r, w_d, x, gain, bias, *, bm):
    s = x.shape[0]
    const = lambda shape: pl.BlockSpec(shape, lambda i: (0, 0))
    return pl.pallas_call(
        _outproj_kernel,
        grid=(s // bm,),
        in_specs=[pl.BlockSpec((bm, RWKV_WIDTH), lambda i: (i, 0)),
                  pl.BlockSpec((bm, DIFF_WIDTH), lambda i: (i, 0)),
                  const((RWKV_WIDTH, D_MODEL)), const((DIFF_WIDTH, D_MODEL)),
                  pl.BlockSpec((bm, D_MODEL), lambda i: (i, 0)),
                  const((1, D_MODEL)), const((1, D_MODEL))],
        out_specs=[pl.BlockSpec((bm, D_MODEL), lambda i: (i, 0))] * 2,
        out_shape=[jax.ShapeDtypeStruct((s, D_MODEL), F32), jax.ShapeDtypeStruct((s, D_MODEL), BF16)],
        compiler_params=_cparams(1),
        name="outproj_ln",
    )(y_rwkv, y_diff, w_r, w_d, x, gain.reshape(1, -1), bias.reshape(1, -1))


def _ff2_kernel(h_ref, w_ref, x_ref, gain_ref, bias_ref, o_ref):
    kk = pl.program_id(1)
    part = lambda: jnp.dot(h_ref[...], w_ref[...], preferred_element_type=F32)

    @pl.when(kk == 0)
    def _():
        o_ref[...] = part()

    @pl.when((kk > 0) & (kk < pl.num_programs(1) - 1))
    def _():
        o_ref[...] += part()

    @pl.when(kk == pl.num_programs(1) - 1)
    def _():
        for r in range(o_ref.shape[0] // LN_ROWS):
            rows = slice(r * LN_ROWS, (r + 1) * LN_ROWS)
            acc = o_ref[rows, :] + jnp.dot(h_ref[rows, :], w_ref[...], preferred_element_type=F32)
            o_ref[rows, :] = _layer_norm(DEEPNORM_ALPHA * x_ref[rows, :] + acc,
                                         gain_ref[...], bias_ref[...])


def _ff2(h, w, x, gain, bias, *, bm, bk):
    s, kdim = h.shape
    assert s % bm == 0 and kdim % bk == 0 and kdim // bk >= 2
    const = lambda shape: pl.BlockSpec(shape, lambda i, k: (0, 0))
    return pl.pallas_call(
        _ff2_kernel,
        grid=(s // bm, kdim // bk),
        in_specs=[pl.BlockSpec((bm, bk), lambda i, k: (i, k)),
                  pl.BlockSpec((bk, D_MODEL), lambda i, k: (k, 0)),
                  pl.BlockSpec((bm, D_MODEL), lambda i, k: (i, 0)),
                  const((1, D_MODEL)), const((1, D_MODEL))],
        out_specs=pl.BlockSpec((bm, D_MODEL), lambda i, k: (i, 0)),
        out_shape=jax.ShapeDtypeStruct((s, D_MODEL), F32),
        compiler_params=_cparams(2),
        name="ff2_ln",
    )(h, w, x, gain.reshape(1, -1), bias.reshape(1, -1))


def _pad_rows(w, rows):
    return jnp.pad(w, ((0, rows - w.shape[0]), (0, 0)))


def _layer(x, positions, w_in, mu_shift, w0, w_up, a0, a_up, g_up, k_k, k_a, r_k, gn_gain, gn_bias,
           lq1, lk1, lq2, lk2, subln_gain, w_out, ln1_gain, ln1_bias, w_ff1, w_ff2, ln2_gain, ln2_bias,
           *, bm, prep_tb, nc, bq):
    w3 = 3 * RWKV_WIDTH
    rwkv_cols = w3 + DECAY_LORA + AAA_LORA + GATE_LORA
    qk0 = RWKV_PCOLS
    v0 = qk0 + 2 * DIFF_WIDTH

    def pad_cols(a, lo, hi, width):
        return jnp.pad(a[..., lo:hi], [(0, 0)] * (a.ndim - 1) + [(0, width - (hi - lo))])

    def rwkv_layout(a):
        return jnp.concatenate([
            a[..., :w3],
            pad_cols(a, w3, w3 + DECAY_LORA, DECAY_PAD),
            pad_cols(a, w3 + DECAY_LORA, w3 + DECAY_LORA + AAA_LORA, AAA_PAD),
            pad_cols(a, w3 + DECAY_LORA + AAA_LORA, rwkv_cols, GATE_PAD)], axis=-1)

    w_all = _relayout_w_in(w_in.T, bk=256)
    proj_r, xb = _proj_cast(x, w_all, bm=bm, bn=RWKV_PCOLS // 2, n=RWKV_PCOLS)
    vt = _matmul(xb, w_all, bm=bm, bn=512, n=DIFF_WIDTH, col0=v0, out_dtype=BF16,
                 transpose_out=True, name="proj_vt")

    at, bt, kt, rt, v_r, wc, bonus, gate = _rwkv_prep(
        proj_r, rwkv_layout(mu_shift), w0,
        _pad_rows(w_up, DECAY_PAD).astype(BF16), a0, _pad_rows(a_up, AAA_PAD).astype(BF16),
        _pad_rows(g_up, GATE_PAD).astype(BF16), k_k, k_a, r_k.reshape(-1), tb=prep_tb)
    y_rwkv = _rwkv_chunks(at, bt, kt, rt, v_r, wc, bonus, gate, gn_gain, gn_bias, nc=nc)

    qk = _proj_rope(xb, w_all, positions, bm=bm, bn=512, n=2 * DIFF_WIDTH, col0=qk0)
    lam_vecs = jnp.stack([lq1, lk1, lq2, lk2]).astype(F32)
    y_diff = _attention(qk, vt, lam_vecs, subln_gain, bq=bq)

    wo = w_out.astype(BF16)
    x1, x1b = _outproj(y_rwkv, y_diff, wo[:RWKV_WIDTH], wo[RWKV_WIDTH:], x, ln1_gain, ln1_bias, bm=512)
    h = _ff1(x1b, w_ff1, bm=2 * bm, bn=1024)
    return _ff2(h, w_ff2.astype(BF16), x1, ln2_gain, ln2_bias, bm=1024, bk=1024)


def kernel(x, positions, w_in, mu_shift, rwkv_w0, rwkv_w_up, rwkv_a0, rwkv_a_up, rwkv_g_up, rwkv_k_k,
           rwkv_k_a, rwkv_r_k, rwkv_gn_gain, rwkv_gn_bias, diff_lambda_q1, diff_lambda_k1,
           diff_lambda_q2, diff_lambda_k2, diff_subln_gain, w_out, ln1_gain, ln1_bias, w_ff1, w_ff2,
           ln2_gain, ln2_bias):
    batch, seq, _ = x.shape
    assert batch == 1 and w_in.shape[0] == 1
    out = _layer(
        x[0], positions[0], w_in[0], mu_shift[0], rwkv_w0[0], rwkv_w_up[0], rwkv_a0[0], rwkv_a_up[0],
        rwkv_g_up[0], rwkv_k_k[0], rwkv_k_a[0], rwkv_r_k[0], rwkv_gn_gain[0], rwkv_gn_bias[0],
        diff_lambda_q1[0], diff_lambda_k1[0], diff_lambda_q2[0], diff_lambda_k2[0],
        diff_subln_gain[0], w_out[0], ln1_gain[0], ln1_bias[0], w_ff1[0], w_ff2[0], ln2_gain[0],
        ln2_bias[0], bm=1024, prep_tb=256, nc=2, bq=1024)
    return out[None]
```

```python
import functools
import math

import jax
import jax.numpy as jnp
from jax import lax
from jax.experimental import pallas as pl
from jax.experimental.pallas import tpu as pltpu

F32 = jnp.float32
BF16 = jnp.bfloat16

D_MODEL = 2048
RWKV_WIDTH = 1024
HEAD_DIM = 64
DECAY_LORA = 64
AAA_LORA = 64
GATE_LORA = 160
DIFF_WIDTH = 1024
DIFF_V_DIM = 128
DIFF_HEADS = 8
D_FF = 8192
ROPE_THETA = 10000.0
LN_EPS = 1e-5
RWKV_GN_EPS = 64e-5
SUBLN_EPS = 1e-5
DEEPNORM_ALPHA = 2.0 ** 0.25
LAMBDA_INIT = 0.8 - 0.6 * math.exp(0.0)
LOG2_E = math.log2(math.e)

LANES = 128
SUBLANES = 8
VMEM_LIMIT_BYTES = 56 * 1024 * 1024

DECAY_PAD = 128
AAA_PAD = 128
GATE_PAD = 256
RWKV_PCOLS = 3 * RWKV_WIDTH + DECAY_PAD + AAA_PAD + GATE_PAD
CHUNK = 64
GROUP_HEADS = 4
GROUP_W = GROUP_HEADS * HEAD_DIM


def _cparams(n_axes):
    return pltpu.CompilerParams(
        dimension_semantics=("arbitrary",) * n_axes,
        vmem_limit_bytes=VMEM_LIMIT_BYTES,
    )


def _dot(a, b):
    return jnp.dot(a.astype(BF16), b.astype(BF16), preferred_element_type=F32)


def _split3(x):
    h1 = x.astype(BF16)
    r1 = x - h1.astype(F32)
    h2 = r1.astype(BF16)
    h3 = (r1 - h2.astype(F32)).astype(BF16)
    return h1, h2, h3


def _split2(x):
    h1 = x.astype(BF16)
    h2 = (x - h1.astype(F32)).astype(BF16)
    return h1, h2


def _mm_kernel(a_ref, b_ref, o_ref, *, transpose_out):
    acc = jnp.dot(a_ref[...], b_ref[...], preferred_element_type=F32)
    if transpose_out:
        acc = acc.T
    o_ref[...] = acc.astype(o_ref.dtype)


def _matmul(a, b, *, bm, bn, n, col0=0, out_dtype, transpose_out=False, name):
    m, k = a.shape
    assert m % bm == 0 and n % bn == 0 and col0 % bn == 0
    jb = col0 // bn
    if transpose_out:
        out_spec = pl.BlockSpec((bn, bm), lambda i, j: (j, i))
        out_shape = jax.ShapeDtypeStruct((n, m), out_dtype)
    else:
        out_spec = pl.BlockSpec((bm, bn), lambda i, j: (i, j))
        out_shape = jax.ShapeDtypeStruct((m, n), out_dtype)
    return pl.pallas_call(
        functools.partial(_mm_kernel, transpose_out=transpose_out),
        grid=(m // bm, n // bn),
        in_specs=[pl.BlockSpec((bm, k), lambda i, j: (i, 0)),
                  pl.BlockSpec((k, bn), lambda i, j: (0, jb + j))],
        out_specs=out_spec,
        out_shape=out_shape,
        compiler_params=_cparams(2),
        name=name,
    )(a, b)


def _proj_cast_kernel(x_ref, w_ref, o_ref, xb_ref):
    @pl.when(pl.program_id(1) == 0)
    def _():
        xb_ref[...] = x_ref[...].astype(BF16)

    o_ref[...] = jnp.dot(xb_ref[...], w_ref[...], preferred_element_type=F32)


def _proj_cast(x, w_all, *, bm, bn, n):
    m, k = x.shape
    assert m % bm == 0 and n % bn == 0
    return pl.pallas_call(
        _proj_cast_kernel,
        grid=(m // bm, n // bn),
        in_specs=[pl.BlockSpec((bm, k), lambda i, j: (i, 0)),
                  pl.BlockSpec((k, bn), lambda i, j: (0, j))],
        out_specs=[pl.BlockSpec((bm, bn), lambda i, j: (i, j)),
                   pl.BlockSpec((bm, k), lambda i, j: (i, 0))],
        out_shape=[jax.ShapeDtypeStruct((m, n), F32), jax.ShapeDtypeStruct((m, k), BF16)],
        compiler_params=_cparams(2),
        name="proj_rwkv",
    )(x, w_all)


def _ff1_kernel(x_ref, w_ref, o_ref, wb_ref):
    @pl.when(pl.program_id(1) == 0)
    def _():
        wb_ref[...] = w_ref[...].astype(BF16)

    acc = jnp.dot(x_ref[...], wb_ref[...], preferred_element_type=F32)
    o_ref[...] = jnp.square(jnp.maximum(acc, 0.0)).astype(o_ref.dtype)


def _ff1(x, w, *, bm, bn):
    m, k = x.shape
    _, n = w.shape
    assert m % bm == 0 and n % bn == 0
    return pl.pallas_call(
        _ff1_kernel,
        grid=(n // bn, m // bm),
        in_specs=[pl.BlockSpec((bm, k), lambda j, i: (i, 0)),
                  pl.BlockSpec((k, bn), lambda j, i: (0, j))],
        out_specs=pl.BlockSpec((bm, bn), lambda j, i: (i, j)),
        out_shape=jax.ShapeDtypeStruct((m, n), BF16),
        scratch_shapes=[pltpu.VMEM((k, bn), BF16)],
        compiler_params=_cparams(2),
        name="ff1",
    )(x, w)


def _relayout_kernel(wt_ref, o_ref):
    cols = wt_ref.shape[1]
    w3 = 3 * RWKV_WIDTH
    a0 = w3 + DECAY_LORA
    g0 = a0 + AAA_LORA
    q0 = g0 + GATE_LORA
    zeros = lambda n: jnp.zeros((n, cols), F32)
    pieces = [wt_ref[:a0, :], zeros(DECAY_PAD - DECAY_LORA),
              wt_ref[a0:g0, :], zeros(AAA_PAD - AAA_LORA),
              wt_ref[g0:q0, :], zeros(GATE_PAD - GATE_LORA),
              wt_ref[q0:q0 + DIFF_WIDTH, :] * (HEAD_DIM ** -0.5 * LOG2_E),
              wt_ref[q0 + DIFF_WIDTH:, :]]
    for c0 in range(0, o_ref.shape[1], RELAYOUT_COLS):
        rows = slice(c0, c0 + RELAYOUT_COLS)
        tile = jnp.concatenate(_row_window(pieces, c0, c0 + RELAYOUT_COLS), axis=0)
        o_ref[:, rows] = tile.T.astype(o_ref.dtype)


def _row_window(pieces, lo, hi):
    out, start = [], 0
    for p in pieces:
        stop = start + p.shape[0]
        a, b = max(lo, start), min(hi, stop)
        if a < b:
            out.append(p[a - start:b - start])
        start = stop
    return out


RELAYOUT_COLS = 512


def _relayout_w_in(w_in_t, *, bk):
    n, k = w_in_t.shape
    n_out = RWKV_PCOLS + 3 * DIFF_WIDTH
    assert n == n_out - (RWKV_PCOLS - 3 * RWKV_WIDTH - DECAY_LORA - AAA_LORA - GATE_LORA)
    assert n_out % RELAYOUT_COLS == 0
    return pl.pallas_call(
        _relayout_kernel,
        grid=(k // bk,),
        in_specs=[pl.BlockSpec((n, bk), lambda i: (0, i))],
        out_specs=pl.BlockSpec((bk, n_out), lambda i: (i, 0)),
        out_shape=jax.ShapeDtypeStruct((k, n_out), BF16),
        compiler_params=_cparams(1),
        name="relayout_w_in",
    )(w_in_t)


def _rwkv_prep_kernel(p_ref, mu_ref, w0_ref, wup_ref, a0_ref, aup_ref, gup_ref, kk_ref, ka_ref,
                      rk_ref, segin_ref, segout_ref, tri_ref,
                      at_ref, bt_ref, kt_ref, rt_ref, v_ref, wc_ref, bonus_ref, g_ref,
                      carry_ref, *, tb):
    i = pl.program_id(0)

    @pl.when(i == 0)
    def _():
        carry_ref[...] = jnp.zeros_like(carry_ref)

    def mixed(lo, hi):
        p = p_ref[:, lo:hi]
        prev_last = carry_ref[0:1, lo:hi]
        shifted = pltpu.roll(p, 1, axis=0)
        row = lax.broadcasted_iota(jnp.int32, p.shape, 0)
        prev = jnp.where(row == 0, prev_last, shifted)
        carry_ref[0:1, lo:hi] = p[tb - 1:tb, :]
        return p + (prev - p) * mu_ref[:, lo:hi]

    w3 = 3 * RWKV_WIDTH
    r = mixed(0, RWKV_WIDTH)
    k = mixed(RWKV_WIDTH, 2 * RWKV_WIDTH)
    v = mixed(2 * RWKV_WIDTH, w3)
    w_d = mixed(w3, w3 + DECAY_PAD)
    a_d = mixed(w3 + DECAY_PAD, w3 + DECAY_PAD + AAA_PAD)
    g_d = mixed(w3 + DECAY_PAD + AAA_PAD, RWKV_PCOLS)

    z = w0_ref[...] + _dot(jnp.tanh(w_d), wup_ref[...])
    softplus = jnp.maximum(-z, 0.0) + jnp.log(1.0 + jnp.exp(-jnp.abs(z)))
    lw = -jnp.exp(-softplus - 0.5)
    a = jax.nn.sigmoid(a0_ref[...] + _dot(a_d, aup_ref[...]))
    g_ref[...] = _dot(jax.nn.sigmoid(g_d), gup_ref[...])

    def segsum(x):
        h1, h2 = _split2(jnp.dot(x.astype(BF16), segin_ref[...], preferred_element_type=F32))
        return (jnp.dot(h1, segout_ref[...], preferred_element_type=F32)
                + jnp.dot(h2, segout_ref[...], preferred_element_type=F32))

    kk0 = k * kk_ref[...]
    kk = kk0 * lax.rsqrt(jnp.maximum(segsum(kk0 * kk0), 1e-24))
    k2 = k * (1.0 + (a - 1.0) * ka_ref[...])
    bonus_ref[...] = segsum(r * k2 * rk_ref[...]) * v

    tri = tri_ref[...]
    l1, l2, l3 = _split3(lw)
    cs = (jnp.dot(tri, l1, preferred_element_type=F32)
          + jnp.dot(tri, l2, preferred_element_type=F32)
          + jnp.dot(tri, l3, preferred_element_type=F32))
    cl = cs[:tb]
    tot = cs[tb:]
    e_neg = jnp.exp(-cl)
    at_ref[...] = (-kk * jnp.exp(cl - lw)).astype(BF16)
    bt_ref[...] = (kk * a * e_neg).astype(BF16)
    kt_ref[...] = (k2 * e_neg).astype(BF16)
    rt_ref[...] = (r * jnp.exp(cl)).astype(BF16)
    v_ref[...] = v.astype(BF16)
    for c in range(tb // CHUNK):
        wc_ref[c] = jnp.exp(tot[c * CHUNK:c * CHUNK + 1, :])


def _rwkv_prep(proj_r, mu, w0, wup, a0, aup, gup, k_k, k_a, r_k, *, tb):
    s = proj_r.shape[0]
    assert s % tb == 0 and tb % CHUNK == 0
    w = RWKV_WIDTH
    lane = jnp.arange(w)
    seg_in = (lane[:, None] // HEAD_DIM == jnp.arange(LANES)[None, :]).astype(BF16)
    t = jnp.arange(tb)
    same = t[:, None] // CHUNK == t[None, :] // CHUNK
    tri = jnp.concatenate([(same & (t[:, None] >= t[None, :])), same], axis=0).astype(BF16)
    row = lambda x: x.reshape(1, -1).astype(F32)
    const = lambda shape: pl.BlockSpec(shape, lambda i: (0,) * len(shape))
    tile = lambda width: pl.BlockSpec((tb, width), lambda i: (i, 0))
    outs = pl.pallas_call(
        functools.partial(_rwkv_prep_kernel, tb=tb),
        grid=(s // tb,),
        in_specs=[tile(RWKV_PCOLS), const((1, RWKV_PCOLS)), const((1, w)), const((DECAY_PAD, w)),
                  const((1, w)), const((AAA_PAD, w)), const((GATE_PAD, w)), const((1, w)),
                  const((1, w)), const((1, w)), const((w, LANES)), const((LANES, w)),
                  const((2 * tb, tb))],
        out_specs=[tile(w), tile(w), tile(w), tile(w), tile(w),
                   pl.BlockSpec((tb // CHUNK, 1, w), lambda i: (i, 0, 0)),
                   tile(w), tile(w)],
        out_shape=[jax.ShapeDtypeStruct((s, w), BF16)] * 5
        + [jax.ShapeDtypeStruct((s // CHUNK, 1, w), F32),
           jax.ShapeDtypeStruct((s, w), F32), jax.ShapeDtypeStruct((s, w), F32)],
        scratch_shapes=[pltpu.VMEM((SUBLANES, RWKV_PCOLS), F32)],
        compiler_params=_cparams(1),
        name="rwkv_prep",
    )(proj_r, row(mu), row(w0), wup, row(a0), aup, gup, row(k_k), row(k_a), row(r_k), seg_in, seg_in.T, tri)
    return outs


def _bdot(a, b):
    return jnp.dot(a, b, preferred_element_type=F32)


def _bdot_nt(a, b):
    return lax.dot_general(a, b, (((1,), (1,)), ((), ())), preferred_element_type=F32)


(MASK_STRICT, MASK_INCL, MASK_EYE, MASK_BLK8, MASK_OFF16, MASK_OFF32, MASK_OFF64,
 MASK_HEAD) = range(8)


def _chunk_masks():
    n = GROUP_W
    r = jnp.arange(n)[:, None]
    c = jnp.arange(n)[None, :]
    head = (r // CHUNK) == (c // CHUNK)
    t_r, t_c = r % CHUNK, c % CHUNK
    blk = lambda size: (r // size) == (c // size)
    masks = [head & (t_r > t_c), head & (t_r >= t_c), r == c, blk(8),
             blk(16) & ~blk(8), blk(32) & ~blk(16), blk(64) & ~blk(32), head]
    return jnp.stack(masks).astype(F32)


def _rwkv_chunk_kernel(at_ref, bt_ref, kt_ref, rt_ref, v_ref, wc_ref, bonus_ref, g_ref,
                       gain_ref, bias_ref, mask_ref, o_ref, state_ref, *, nc):
    @pl.when(pl.program_id(0) == 0)
    def _():
        state_ref[...] = jnp.zeros_like(state_ref)

    n = GROUP_W
    ng = RWKV_WIDTH // GROUP_W
    items = [(c, g) for c in range(nc) for g in range(ng)]
    lane_head = lax.broadcasted_iota(jnp.int32, (CHUNK, n), 1) >> 6

    def tile(ref, c, g):
        return ref[c * CHUNK:(c + 1) * CHUNK, g * n:(g + 1) * n]

    def expand(x):
        zero = jnp.zeros_like(x)
        return jnp.concatenate([jnp.where(lane_head == h, x, zero) for h in range(GROUP_HEADS)],
                               axis=0)

    def repeat(x):
        return jnp.concatenate([x] * GROUP_HEADS, axis=0)

    mask = lambda k: mask_ref[k]
    bf = lambda x: x.astype(BF16)
    each = lambda f: [f(*it) for it in items]
    zipped = lambda f, *lists: [f(*args) for args in zip(*lists)]

    at = each(lambda c, g: tile(at_ref, c, g))
    bt = each(lambda c, g: tile(bt_ref, c, g))
    kt = each(lambda c, g: tile(kt_ref, c, g))
    wc = each(lambda c, g: wc_ref[c][:, g * n:(g + 1) * n])
    ae = [expand(x) for x in at]
    be = [expand(x) for x in bt]
    ke = [expand(x) for x in kt]
    ve = each(lambda c, g: expand(tile(v_ref, c, g)))
    at4 = [repeat(x) for x in at]
    rt4 = each(lambda c, g: repeat(tile(rt_ref, c, g)))
    bhe = zipped(lambda x, w: expand(bf(x.astype(F32) * w)), bt, wc)
    khe = zipped(lambda x, w: expand(bf(x.astype(F32) * w)), kt, wc)

    lab = zipped(lambda a, b: _bdot_nt(a, b) * mask(MASK_STRICT), at4, be)
    lak = zipped(lambda a, b: bf(_bdot_nt(a, b) * mask(MASK_STRICT)), at4, ke)
    arb = zipped(lambda a, b: bf(_bdot_nt(a, b) * mask(MASK_INCL)), rt4, be)
    ark = zipped(lambda a, b: bf(_bdot_nt(a, b) * mask(MASK_INCL)), rt4, ke)
    lv = zipped(lambda a, b: bf(_bdot(a, b)), lak, ve)

    l8 = [x * mask(MASK_BLK8) for x in lab]
    minv = [mask(MASK_EYE) + x for x in l8]
    pw = [bf(x) for x in l8]
    for _ in range(2):
        pw2 = [_bdot(x, x) for x in pw]
        pw = [bf(x) for x in pw2]
        minv = zipped(lambda p, m: m + _bdot(p, bf(m)), pw, minv)
    for k in (MASK_OFF16, MASK_OFF32, MASK_OFF64):
        off = [bf(x * mask(k)) for x in lab]
        mb = [bf(m) for m in minv]
        tmp = zipped(lambda m, o: bf(_bdot(m, o)), mb, off)
        minv = zipped(lambda m, t, m16: m + _bdot(t, m16), minv, tmp, mb)
    mb = [bf(m) for m in minv]

    p = zipped(lambda m, a: bf(_bdot(m, a)), mb, ae)
    q = zipped(lambda m, x: bf(_bdot(m, x)), mb, lv)
    g_mat = zipped(lambda r4, a, x: bf(r4.astype(F32) * mask(MASK_HEAD) + _bdot(a, x)), rt4, arb, p)
    y0 = zipped(lambda a, x, b, v: _bdot(a, x) + _bdot(b, v), arb, q, ark, ve)
    bht = [x.T for x in bhe]
    kht = [x.T for x in khe]
    phi = zipped(lambda w, b, x: bf(mask(MASK_EYE) * w + _bdot(b, x)), wc, bht, p)
    psi = zipped(lambda b, x, k, v: _bdot(b, x) + _bdot(k, v), bht, q, kht, ve)

    state = [state_ref[g] for g in range(ng)]
    for c in range(nc):
        idx = [c * ng + g for g in range(ng)]
        sb = [bf(s) for s in state]
        ys = [_bdot(g_mat[i], sb[g]) + y0[i] for g, i in enumerate(idx)]
        state = [_bdot(phi[i], sb[g]) + psi[i] for g, i in enumerate(idx)]
        for g, y in enumerate(ys):
            mean = jnp.sum(y, axis=1, keepdims=True) * (1.0 / HEAD_DIM)
            yc = (y - mean) * mask(MASK_HEAD)
            var = jnp.sum(yc * yc, axis=1, keepdims=True) * (1.0 / HEAD_DIM)
            yn = yc * lax.rsqrt(var + RWKV_GN_EPS)
            yw = (yn[0:CHUNK] + yn[CHUNK:2 * CHUNK] + yn[2 * CHUNK:3 * CHUNK]
                  + yn[3 * CHUNK:4 * CHUNK])
            rows = slice(c * CHUNK, (c + 1) * CHUNK)
            cols = slice(g * n, (g + 1) * n)
            out = (yw * gain_ref[:, cols] + bias_ref[:, cols] + bonus_ref[rows, cols]) * g_ref[rows, cols]
            o_ref[rows, cols] = out.astype(o_ref.dtype)
    for g in range(ng):
        state_ref[g] = state[g]


def _rwkv_chunks(at, bt, kt, rt, v, wc, bonus, g, gn_gain, gn_bias, *, nc):
    s, w = at.shape
    tb = nc * CHUNK
    assert s % tb == 0 and w % GROUP_W == 0
    tile = pl.BlockSpec((tb, w), lambda ti: (ti, 0))
    rowspec = pl.BlockSpec((1, w), lambda ti: (0, 0))
    masks = _chunk_masks()
    return pl.pallas_call(
        functools.partial(_rwkv_chunk_kernel, nc=nc),
        grid=(s // tb,),
        in_specs=[tile, tile, tile, tile, tile,
                  pl.BlockSpec((nc, 1, w), lambda ti: (ti, 0, 0)),
                  tile, tile, rowspec, rowspec,
                  pl.BlockSpec(masks.shape, lambda ti: (0, 0, 0))],
        out_specs=tile,
        out_shape=jax.ShapeDtypeStruct((s, w), BF16),
        scratch_shapes=[pltpu.VMEM((w // GROUP_W, GROUP_W, GROUP_W), F32)],
        compiler_params=_cparams(1),
        name="rwkv_chunks",
    )(at, bt, kt, rt, v, wc, bonus, g, gn_gain.reshape(1, -1).astype(F32),
      gn_bias.reshape(1, -1).astype(F32), masks)


ROPE_ROWS = 256


def _proj_rope_kernel(pos_ref, invf_ref, x_ref, w_ref, o_ref, cos_ref, sin_ref, *, bm, bn):
    def first_half(rows):
        lane = lax.broadcasted_iota(jnp.int32, (rows, LANES), 1)
        return (lane & (HEAD_DIM - 1)) < (HEAD_DIM // 2)

    @pl.when(pl.program_id(1) == 0)
    def _():
        ang = pos_ref[...].astype(F32) * invf_ref[...]
        cos_ref[...] = jnp.cos(ang)
        sin = jnp.sin(ang)
        sin_ref[...] = jnp.where(first_half(bm), -sin, sin)

    half = first_half(ROPE_ROWS)

    for r in range(bm // ROPE_ROWS):
        rows = slice(r * ROPE_ROWS, (r + 1) * ROPE_ROWS)
        acc = jnp.dot(x_ref[rows, :], w_ref[...], preferred_element_type=F32)
        cos = cos_ref[rows, :]
        sin = sin_ref[rows, :]
        for c in range(bn // LANES):
            t = acc[:, c * LANES:(c + 1) * LANES]
            partner = jnp.where(half, pltpu.roll(t, LANES - HEAD_DIM // 2, axis=1),
                                pltpu.roll(t, HEAD_DIM // 2, axis=1))
            o_ref[rows, c * LANES:(c + 1) * LANES] = (t * cos + partner * sin).astype(o_ref.dtype)


def _proj_rope(xb, w_all, positions, *, bm, bn, n, col0):
    s, k = xb.shape
    assert n % bn == 0 and col0 % bn == 0
    jb = col0 // bn
    half = jnp.arange(0, HEAD_DIM, 2, dtype=F32)
    inv_freq = ROPE_THETA ** (-half / HEAD_DIM)
    invf = jnp.tile(inv_freq, 2 * LANES // HEAD_DIM).reshape(1, LANES)
    return pl.pallas_call(
        functools.partial(_proj_rope_kernel, bm=bm, bn=bn),
        grid=(s // bm, n // bn),
        in_specs=[pl.BlockSpec((bm, 1), lambda i, j: (i, 0)),
                  pl.BlockSpec((1, LANES), lambda i, j: (0, 0)),
                  pl.BlockSpec((bm, k), lambda i, j: (i, 0)),
                  pl.BlockSpec((k, bn), lambda i, j: (0, jb + j))],
        out_specs=pl.BlockSpec((bm, bn), lambda i, j: (i, j)),
        out_shape=jax.ShapeDtypeStruct((s, n), BF16),
        scratch_shapes=[pltpu.VMEM((bm, LANES), F32), pltpu.VMEM((bm, LANES), F32)],
        compiler_params=_cparams(2),
        name="proj_qk_rope",
    )(positions.reshape(s, 1), invf, xb, w_all)


ATT_STRIP = 128


ATT_KB = 512


def _attn_kernel(lam_ref, gain_ref, q_ref, k_ref, vt_ref, o_ref, acc_ref, s0_ref, s1_ref, *, bq):
    i = pl.program_id(1)
    nq = 2 * bq
    kb = ATT_KB
    q = q_ref[...]
    lane = lax.broadcasted_iota(jnp.int32, q.shape, 1)
    zero = jnp.zeros_like(q)
    q_all = jnp.concatenate([jnp.where(lane < HEAD_DIM, q, zero),
                             jnp.where(lane >= HEAD_DIM, q, zero)], axis=0)
    s_slots = (s0_ref, s1_ref)
    all_cols = ((0, nq),)
    late_cols = ((kb, bq), (bq + kb, nq))

    def scores(j, slot):
        start = pl.multiple_of(j * kb, kb)
        s_slots[slot][...] = lax.dot_general(k_ref[pl.ds(start, kb), :], q_all,
                                             (((1,), (1,)), ((), ())), preferred_element_type=F32)

    def consume(j, slot, m, l, groups, key0):
        start = pl.multiple_of(j * kb, kb)
        vtb = vt_ref[:, pl.ds(start, kb)]
        m_parts = {c: m[:, c * ATT_STRIP:(c + 1) * ATT_STRIP] for c in range(nq // ATT_STRIP)}
        l_parts = {c: l[:, c * ATT_STRIP:(c + 1) * ATT_STRIP] for c in range(nq // ATT_STRIP)}
        for lo, hi in groups:
            p_strips, alphas = [], []
            for c in range(lo // ATT_STRIP, hi // ATT_STRIP):
                sl = slice(c * ATT_STRIP, (c + 1) * ATT_STRIP)
                s = s_slots[slot][:, sl]
                q_first = (c * ATT_STRIP) % bq
                if key0 is not None and key0 + kb - 1 > q_first:
                    key = lax.broadcasted_iota(jnp.int32, s.shape, 0) + key0
                    qry = lax.broadcasted_iota(jnp.int32, s.shape, 1) + q_first
                    s = jnp.where(key <= qry, s, -1e30)
                mn = jnp.maximum(m_parts[c], jnp.max(s, axis=0, keepdims=True))
                p = jnp.exp2(s - mn)
                alpha = jnp.exp2(m_parts[c] - mn)
                l_parts[c] = alpha * l_parts[c] + jnp.sum(p, axis=0, keepdims=True)
                m_parts[c] = mn
                alphas.append(alpha)
                p_strips.append(p.astype(BF16))
            p_all = jnp.concatenate(p_strips, axis=1)
            alpha = jnp.concatenate(alphas, axis=1)
            acc_ref[:, lo:hi] = alpha * acc_ref[:, lo:hi] + jnp.dot(vtb, p_all,
                                                                    preferred_element_type=F32)
        order = range(nq // ATT_STRIP)
        return (jnp.concatenate([m_parts[c] for c in order], axis=1),
                jnp.concatenate([l_parts[c] for c in order], axis=1))

    def pair(jj, carry):
        j = 2 * jj
        scores(j + 1, 1)
        m, l = consume(j, 0, carry[0], carry[1], all_cols, None)
        scores(j + 2, 0)
        return consume(j + 1, 1, m, l, all_cols, None)

    acc_ref[...] = jnp.zeros_like(acc_ref)
    scores(0, 0)
    init = (jnp.full((1, nq), -1e30, F32), jnp.zeros((1, nq), F32))
    m, l = lax.fori_loop(0, i, pair, init)
    scores(2 * i + 1, 1)
    m, l = consume(2 * i, 0, m, l, all_cols, 0)
    m, l = consume(2 * i + 1, 1, m, l, late_cols, kb)

    lam_v = lam_ref[...]
    lam = (jnp.exp(jnp.sum(lam_v[0:1] * lam_v[1:2], axis=1, keepdims=True))
           - jnp.exp(jnp.sum(lam_v[2:3] * lam_v[3:4], axis=1, keepdims=True)) + LAMBDA_INIT)
    o = acc_ref[...] * (1.0 / l)
    out = o[:, :bq] - lam * o[:, bq:]
    out = out * lax.rsqrt(jnp.mean(jnp.square(out), axis=0, keepdims=True) + SUBLN_EPS)
    out = out * gain_ref[...] * (1.0 - LAMBDA_INIT)
    o_ref[...] = out.T.astype(o_ref.dtype)


def _attention(qk, vt, lam_vecs, subln_gain, *, bq):
    s = qk.shape[0]
    assert s % bq == 0 and bq == 2 * ATT_KB
    return pl.pallas_call(
        functools.partial(_attn_kernel, bq=bq),
        grid=(DIFF_HEADS, s // bq),
        in_specs=[pl.BlockSpec((4, HEAD_DIM), lambda h, i: (0, 0)),
                  pl.BlockSpec((DIFF_V_DIM, 1), lambda h, i: (0, 0)),
                  pl.BlockSpec((bq, DIFF_V_DIM), lambda h, i: (i, h)),
                  pl.BlockSpec((s, DIFF_V_DIM), lambda h, i: (0, DIFF_HEADS + h)),
                  pl.BlockSpec((DIFF_V_DIM, s), lambda h, i: (h, 0))],
        out_specs=pl.BlockSpec((bq, DIFF_V_DIM), lambda h, i: (i, h)),
        out_shape=jax.ShapeDtypeStruct((s, DIFF_WIDTH), BF16),
        scratch_shapes=[pltpu.VMEM((DIFF_V_DIM, 2 * bq), F32), pltpu.VMEM((ATT_KB, 2 * bq), F32),
                        pltpu.VMEM((ATT_KB, 2 * bq), F32)],
        compiler_params=_cparams(2),
        name="diff_attn",
    )(lam_vecs, subln_gain.reshape(-1, 1).astype(F32), qk, qk, vt)


def _layer_norm(z, gain, bias):
    mean = jnp.mean(z, axis=-1, keepdims=True)
    zc = z - mean
    var = jnp.mean(zc * zc, axis=-1, keepdims=True)
    return zc * lax.rsqrt(var + LN_EPS) * gain + bias


LN_ROWS = 128


def _outproj_kernel(yr_ref, yd_ref, wr_ref, wd_ref, x_ref, gain_ref, bias_ref, o32_ref, o16_ref):
    for r in range(x_ref.shape[0] // LN_ROWS):
        rows = slice(r * LN_ROWS, (r + 1) * LN_ROWS)
        mix = (jnp.dot(yr_ref[rows, :], wr_ref[...], preferred_element_type=F32)
               + jnp.dot(yd_ref[rows, :], wd_ref[...], preferred_element_type=F32))
        y = _layer_norm(DEEPNORM_ALPHA * x_ref[rows, :] + mix, gain_ref[...], bias_ref[...])
        o32_ref[rows, :] = y
        o16_ref[rows, :] = y.astype(BF16)


def _outproj(y_rwkv, y_diff, w_r, w_d, x, gain, bias, *, bm):
    s = x.shape[0]
    const = lambda shape: pl.BlockSpec(shape, lambda i: (0, 0))
    return pl.pallas_call(
        _outproj_kernel,
        grid=(s // bm,),
        in_specs=[pl.BlockSpec((bm, RWKV_WIDTH), lambda i: (i, 0)),
                  pl.BlockSpec((bm, DIFF_WIDTH), lambda i: (i, 0)),
                  const((RWKV_WIDTH, D_MODEL)), const((DIFF_WIDTH, D_MODEL)),
                  pl.BlockSpec((bm, D_MODEL), lambda i: (i, 0)),
                  const((1, D_MODEL)), const((1, D_MODEL))],
        out_specs=[pl.BlockSpec((bm, D_MODEL), lambda i: (i, 0))] * 2,
        out_shape=[jax.ShapeDtypeStruct((s, D_MODEL), F32), jax.ShapeDtypeStruct((s, D_MODEL), BF16)],
        compiler_params=_cparams(1),
        name="outproj_ln",
    )(y_rwkv, y_diff, w_r, w_d, x, gain.reshape(1, -1), bias.reshape(1, -1))


def _ff2_kernel(h_ref, w_ref, x_ref, gain_ref, bias_ref, o_ref):
    kk = pl.program_id(1)
    part = lambda: jnp.dot(h_ref[...], w_ref[...], preferred_element_type=F32)

    @pl.when(kk == 0)
    def _():
        o_ref[...] = part()

    @pl.when((kk > 0) & (kk < pl.num_programs(1) - 1))
    def _():
        o_ref[...] += part()

    @pl.when(kk == pl.num_programs(1) - 1)
    def _():
        for r in range(o_ref.shape[0] // LN_ROWS):
            rows = slice(r * LN_ROWS, (r + 1) * LN_ROWS)
            acc = o_ref[rows, :] + jnp.dot(h_ref[rows, :], w_ref[...], preferred_element_type=F32)
            o_ref[rows, :] = _layer_norm(DEEPNORM_ALPHA * x_ref[rows, :] + acc,
                                         gain_ref[...], bias_ref[...])


def _ff2(h, w, x, gain, bias, *, bm, bk):
    s, kdim = h.shape
    assert s % bm == 0 and kdim % bk == 0 and kdim // bk >= 2
    const = lambda shape: pl.BlockSpec(shape, lambda i, k: (0, 0))
    return pl.pallas_call(
        _ff2_kernel,
        grid=(s // bm, kdim // bk),
        in_specs=[pl.BlockSpec((bm, bk), lambda i, k: (i, k)),
                  pl.BlockSpec((bk, D_MODEL), lambda i, k: (k, 0)),
                  pl.BlockSpec((bm, D_MODEL), lambda i, k: (i, 0)),
                  const((1, D_MODEL)), const((1, D_MODEL))],
        out_specs=pl.BlockSpec((bm, D_MODEL), lambda i, k: (i, 0)),
        out_shape=jax.ShapeDtypeStruct((s, D_MODEL), F32),
        compiler_params=_cparams(2),
        name="ff2_ln",
    )(h, w, x, gain.reshape(1, -1), bias.reshape(1, -1))


def _pad_rows(w, rows):
    return jnp.pad(w, ((0, rows - w.shape[0]), (0, 0)))


def _layer(x, positions, w_in, mu_shift, w0, w_up, a0, a_up, g_up, k_k, k_a, r_k, gn_gain, gn_bias,
           lq1, lk1, lq2, lk2, subln_gain, w_out, ln1_gain, ln1_bias, w_ff1, w_ff2, ln2_gain, ln2_bias,
           *, bm, prep_tb, nc, bq):
    w3 = 3 * RWKV_WIDTH
    rwkv_cols = w3 + DECAY_LORA + AAA_LORA + GATE_LORA
    qk0 = RWKV_PCOLS
    v0 = qk0 + 2 * DIFF_WIDTH

    def pad_cols(a, lo, hi, width):
        return jnp.pad(a[..., lo:hi], [(0, 0)] * (a.ndim - 1) + [(0, width - (hi - lo))])

    def rwkv_layout(a):
        return jnp.concatenate([
            a[..., :w3],
            pad_cols(a, w3, w3 + DECAY_LORA, DECAY_PAD),
            pad_cols(a, w3 + DECAY_LORA, w3 + DECAY_LORA + AAA_LORA, AAA_PAD),
            pad_cols(a, w3 + DECAY_LORA + AAA_LORA, rwkv_cols, GATE_PAD)], axis=-1)

    w_all = _relayout_w_in(w_in.T, bk=256)
    proj_r, xb = _proj_cast(x, w_all, bm=bm, bn=RWKV_PCOLS // 2, n=RWKV_PCOLS)
    vt = _matmul(xb, w_all, bm=bm, bn=512, n=DIFF_WIDTH, col0=v0, out_dtype=BF16,
                 transpose_out=True, name="proj_vt")

    at, bt, kt, rt, v_r, wc, bonus, gate = _rwkv_prep(
        proj_r, rwkv_layout(mu_shift), w0,
        _pad_rows(w_up, DECAY_PAD).astype(BF16), a0, _pad_rows(a_up, AAA_PAD).astype(BF16),
        _pad_rows(g_up, GATE_PAD).astype(BF16), k_k, k_a, r_k.reshape(-1), tb=prep_tb)
    y_rwkv = _rwkv_chunks(at, bt, kt, rt, v_r, wc, bonus, gate, gn_gain, gn_bias, nc=nc)

    qk = _proj_rope(xb, w_all, positions, bm=bm, bn=512, n=2 * DIFF_WIDTH, col0=qk0)
    lam_vecs = jnp.stack([lq1, lk1, lq2, lk2]).astype(F32)
    y_diff = _attention(qk, vt, lam_vecs, subln_gain, bq=bq)

    wo = w_out.astype(BF16)
    x1, x1b = _outproj(y_rwkv, y_diff, wo[:RWKV_WIDTH], wo[RWKV_WIDTH:], x, ln1_gain, ln1_bias, bm=512)
    h = _ff1(x1b, w_ff1, bm=2 * bm, bn=1024)
    return _ff2(h, w_ff2.astype(BF16), x1, ln2_gain, ln2_bias, bm=1024, bk=1024)


def kernel(x, positions, w_in, mu_shift, rwkv_w0, rwkv_w_up, rwkv_a0, rwkv_a_up, rwkv_g_up, rwkv_k_k,
           rwkv_k_a, rwkv_r_k, rwkv_gn_gain, rwkv_gn_bias, diff_lambda_q1, diff_lambda_k1,
           diff_lambda_q2, diff_lambda_k2, diff_subln_gain, w_out, ln1_gain, ln1_bias, w_ff1, w_ff2,
           ln2_gain, ln2_bias):
    batch, seq, _ = x.shape
    assert batch == 1 and w_in.shape[0] == 1
    out = _layer(
        x[0], positions[0], w_in[0], mu_shift[0], rwkv_w0[0], rwkv_w_up[0], rwkv_a0[0], rwkv_a_up[0],
        rwkv_g_up[0], rwkv_k_k[0], rwkv_k_a[0], rwkv_r_k[0], rwkv_gn_gain[0], rwkv_gn_bias[0],
        diff_lambda_q1[0], diff_lambda_k1[0], diff_lambda_q2[0], diff_lambda_k2[0],
        diff_subln_gain[0], w_out[0], ln1_gain[0], ln1_bias[0], w_ff1[0], w_ff2[0], ln2_gain[0],
        ln2_bias[0], bm=1024, prep_tb=256, nc=2, bq=1024)
    return out[None]
```

```python
import functools
import math

import jax
import jax.numpy as jnp
from jax import lax
from jax.experimental import pallas as pl
from jax.experimental.pallas import tpu as pltpu

F32 = jnp.float32
BF16 = jnp.bfloat16

D_MODEL = 2048
RWKV_WIDTH = 1024
HEAD_DIM = 64
DECAY_LORA = 64
AAA_LORA = 64
GATE_LORA = 160
DIFF_WIDTH = 1024
DIFF_V_DIM = 128
DIFF_HEADS = 8
D_FF = 8192
ROPE_THETA = 10000.0
LN_EPS = 1e-5
RWKV_GN_EPS = 64e-5
SUBLN_EPS = 1e-5
DEEPNORM_ALPHA = 2.0 ** 0.25
LAMBDA_INIT = 0.8 - 0.6 * math.exp(0.0)
LOG2_E = math.log2(math.e)

LANES = 128
SUBLANES = 8
VMEM_LIMIT_BYTES = 56 * 1024 * 1024

DECAY_PAD = 128
AAA_PAD = 128
GATE_PAD = 256
RWKV_PCOLS = 3 * RWKV_WIDTH + DECAY_PAD + AAA_PAD + GATE_PAD
CHUNK = 64
GROUP_HEADS = 4
GROUP_W = GROUP_HEADS * HEAD_DIM


def _cparams(n_axes):
    return pltpu.CompilerParams(
        dimension_semantics=("arbitrary",) * n_axes,
        vmem_limit_bytes=VMEM_LIMIT_BYTES,
    )


def _dot(a, b):
    return jnp.dot(a.astype(BF16), b.astype(BF16), preferred_element_type=F32)


def _split3(x):
    h1 = x.astype(BF16)
    r1 = x - h1.astype(F32)
    h2 = r1.astype(BF16)
    h3 = (r1 - h2.astype(F32)).astype(BF16)
    return h1, h2, h3


def _split2(x):
    h1 = x.astype(BF16)
    h2 = (x - h1.astype(F32)).astype(BF16)
    return h1, h2


def _mm_kernel(a_ref, b_ref, o_ref, *, transpose_out):
    acc = jnp.dot(a_ref[...], b_ref[...], preferred_element_type=F32)
    if transpose_out:
        acc = acc.T
    o_ref[...] = acc.astype(o_ref.dtype)


def _matmul(a, b, *, bm, bn, n, col0=0, out_dtype, transpose_out=False, name):
    m, k = a.shape
    assert m % bm == 0 and n % bn == 0 and col0 % bn == 0
    jb = col0 // bn
    if transpose_out:
        out_spec = pl.BlockSpec((bn, bm), lambda i, j: (j, i))
        out_shape = jax.ShapeDtypeStruct((n, m), out_dtype)
    else:
        out_spec = pl.BlockSpec((bm, bn), lambda i, j: (i, j))
        out_shape = jax.ShapeDtypeStruct((m, n), out_dtype)
    return pl.pallas_call(
        functools.partial(_mm_kernel, transpose_out=transpose_out),
        grid=(m // bm, n // bn),
        in_specs=[pl.BlockSpec((bm, k), lambda i, j: (i, 0)),
                  pl.BlockSpec((k, bn), lambda i, j: (0, jb + j))],
        out_specs=out_spec,
        out_shape=out_shape,
        compiler_params=_cparams(2),
        name=name,
    )(a, b)


def _proj_cast_kernel(x_ref, w_ref, o_ref, xb_ref):
    @pl.when(pl.program_id(1) == 0)
    def _():
        xb_ref[...] = x_ref[...].astype(BF16)

    o_ref[...] = jnp.dot(xb_ref[...], w_ref[...], preferred_element_type=F32)


def _proj_cast(x, w_all, *, bm, bn, n):
    m, k = x.shape
    assert m % bm == 0 and n % bn == 0
    return pl.pallas_call(
        _proj_cast_kernel,
        grid=(m // bm, n // bn),
        in_specs=[pl.BlockSpec((bm, k), lambda i, j: (i, 0)),
                  pl.BlockSpec((k, bn), lambda i, j: (0, j))],
        out_specs=[pl.BlockSpec((bm, bn), lambda i, j: (i, j)),
                   pl.BlockSpec((bm, k), lambda i, j: (i, 0))],
        out_shape=[jax.ShapeDtypeStruct((m, n), F32), jax.ShapeDtypeStruct((m, k), BF16)],
        compiler_params=_cparams(2),
        name="proj_rwkv",
    )(x, w_all)


def _ff1_kernel(x_ref, w_ref, o_ref, wb_ref):
    @pl.when(pl.program_id(1) == 0)
    def _():
        wb_ref[...] = w_ref[...].astype(BF16)

    acc = jnp.dot(x_ref[...], wb_ref[...], preferred_element_type=F32)
    o_ref[...] = jnp.square(jnp.maximum(acc, 0.0)).astype(o_ref.dtype)


def _ff1(x, w, *, bm, bn):
    m, k = x.shape
    _, n = w.shape
    assert m % bm == 0 and n % bn == 0
    return pl.pallas_call(
        _ff1_kernel,
        grid=(n // bn, m // bm),
        in_specs=[pl.BlockSpec((bm, k), lambda j, i: (i, 0)),
                  pl.BlockSpec((k, bn), lambda j, i: (0, j))],
        out_specs=pl.BlockSpec((bm, bn), lambda j, i: (i, j)),
        out_shape=jax.ShapeDtypeStruct((m, n), BF16),
        scratch_shapes=[pltpu.VMEM((k, bn), BF16)],
        compiler_params=_cparams(2),
        name="ff1",
    )(x, w)


def _relayout_kernel(wt_ref, o_ref):
    cols = wt_ref.shape[1]
    w3 = 3 * RWKV_WIDTH
    a0 = w3 + DECAY_LORA
    g0 = a0 + AAA_LORA
    q0 = g0 + GATE_LORA
    zeros = lambda n: jnp.zeros((n, cols), F32)
    pieces = [wt_ref[:a0, :], zeros(DECAY_PAD - DECAY_LORA),
              wt_ref[a0:g0, :], zeros(AAA_PAD - AAA_LORA),
              wt_ref[g0:q0, :], zeros(GATE_PAD - GATE_LORA),
              wt_ref[q0:q0 + DIFF_WIDTH, :] * (HEAD_DIM ** -0.5 * LOG2_E),
              wt_ref[q0 + DIFF_WIDTH:, :]]
    for c0 in range(0, o_ref.shape[1], RELAYOUT_COLS):
        rows = slice(c0, c0 + RELAYOUT_COLS)
        tile = jnp.concatenate(_row_window(pieces, c0, c0 + RELAYOUT_COLS), axis=0)
        o_ref[:, rows] = tile.T.astype(o_ref.dtype)


def _row_window(pieces, lo, hi):
    out, start = [], 0
    for p in pieces:
        stop = start + p.shape[0]
        a, b = max(lo, start), min(hi, stop)
        if a < b:
            out.append(p[a - start:b - start])
        start = stop
    return out


RELAYOUT_COLS = 512


def _relayout_w_in(w_in_t, *, bk):
    n, k = w_in_t.shape
    n_out = RWKV_PCOLS + 3 * DIFF_WIDTH
    assert n == n_out - (RWKV_PCOLS - 3 * RWKV_WIDTH - DECAY_LORA - AAA_LORA - GATE_LORA)
    assert n_out % RELAYOUT_COLS == 0
    return pl.pallas_call(
        _relayout_kernel,
        grid=(k // bk,),
        in_specs=[pl.BlockSpec((n, bk), lambda i: (0, i))],
        out_specs=pl.BlockSpec((bk, n_out), lambda i: (i, 0)),
        out_shape=jax.ShapeDtypeStruct((k, n_out), BF16),
        compiler_params=_cparams(1),
        name="relayout_w_in",
    )(w_in_t)


def _rwkv_prep_kernel(p_ref, mu_ref, w0_ref, wup_ref, a0_ref, aup_ref, gup_ref, kk_ref, ka_ref,
                      rk_ref, segin_ref, segout_ref, tri_ref,
                      at_ref, bt_ref, kt_ref, rt_ref, v_ref, wc_ref, bonus_ref, g_ref,
                      carry_ref, *, tb):
    i = pl.program_id(0)

    @pl.when(i == 0)
    def _():
        carry_ref[...] = jnp.zeros_like(carry_ref)

    def mixed(lo, hi):
        p = p_ref[:, lo:hi]
        prev_last = carry_ref[0:1, lo:hi]
        shifted = pltpu.roll(p, 1, axis=0)
        row = lax.broadcasted_iota(jnp.int32, p.shape, 0)
        prev = jnp.where(row == 0, prev_last, shifted)
        carry_ref[0:1, lo:hi] = p[tb - 1:tb, :]
        return p + (prev - p) * mu_ref[:, lo:hi]

    w3 = 3 * RWKV_WIDTH
    r = mixed(0, RWKV_WIDTH)
    k = mixed(RWKV_WIDTH, 2 * RWKV_WIDTH)
    v = mixed(2 * RWKV_WIDTH, w3)
    w_d = mixed(w3, w3 + DECAY_PAD)
    a_d = mixed(w3 + DECAY_PAD, w3 + DECAY_PAD + AAA_PAD)
    g_d = mixed(w3 + DECAY_PAD + AAA_PAD, RWKV_PCOLS)

    z = w0_ref[...] + _dot(jnp.tanh(w_d), wup_ref[...])
    softplus = jnp.maximum(-z, 0.0) + jnp.log(1.0 + jnp.exp(-jnp.abs(z)))
    lw = -jnp.exp(-softplus - 0.5)
    a = jax.nn.sigmoid(a0_ref[...] + _dot(a_d, aup_ref[...]))
    g_ref[...] = _dot(jax.nn.sigmoid(g_d), gup_ref[...])

    def segsum(x):
        h1, h2 = _split2(jnp.dot(x.astype(BF16), segin_ref[...], preferred_element_type=F32))
        return (jnp.dot(h1, segout_ref[...], preferred_element_type=F32)
                + jnp.dot(h2, segout_ref[...], preferred_element_type=F32))

    kk0 = k * kk_ref[...]
    kk = kk0 * lax.rsqrt(jnp.maximum(segsum(kk0 * kk0), 1e-24))
    k2 = k * (1.0 + (a - 1.0) * ka_ref[...])
    bonus_ref[...] = segsum(r * k2 * rk_ref[...]) * v

    tri = tri_ref[...]
    l1, l2, l3 = _split3(lw)
    cs = (jnp.dot(tri, l1, preferred_element_type=F32)
          + jnp.dot(tri, l2, preferred_element_type=F32)
          + jnp.dot(tri, l3, preferred_element_type=F32))
    cl = cs[:tb]
    tot = cs[tb:]
    e_neg = jnp.exp(-cl)
    at_ref[...] = (-kk * jnp.exp(cl - lw)).astype(BF16)
    bt_ref[...] = (kk * a * e_neg).astype(BF16)
    kt_ref[...] = (k2 * e_neg).astype(BF16)
    rt_ref[...] = (r * jnp.exp(cl)).astype(BF16)
    v_ref[...] = v.astype(BF16)
    for c in range(tb // CHUNK):
        wc_ref[c] = jnp.exp(tot[c * CHUNK:c * CHUNK + 1, :])


def _rwkv_prep(proj_r, mu, w0, wup, a0, aup, gup, k_k, k_a, r_k, *, tb):
    s = proj_r.shape[0]
    assert s % tb == 0 and tb % CHUNK == 0
    w = RWKV_WIDTH
    lane = jnp.arange(w)
    seg_in = (lane[:, None] // HEAD_DIM == jnp.arange(LANES)[None, :]).astype(BF16)
    t = jnp.arange(tb)
    same = t[:, None] // CHUNK == t[None, :] // CHUNK
    tri = jnp.concatenate([(same & (t[:, None] >= t[None, :])), same], axis=0).astype(BF16)
    row = lambda x: x.reshape(1, -1).astype(F32)
    const = lambda shape: pl.BlockSpec(shape, lambda i: (0,) * len(shape))
    tile = lambda width: pl.BlockSpec((tb, width), lambda i: (i, 0))
    outs = pl.pallas_call(
        functools.partial(_rwkv_prep_kernel, tb=tb),
        grid=(s // tb,),
        in_specs=[tile(RWKV_PCOLS), const((1, RWKV_PCOLS)), const((1, w)), const((DECAY_PAD, w)),
                  const((1, w)), const((AAA_PAD, w)), const((GATE_PAD, w)), const((1, w)),
                  const((1, w)), const((1, w)), const((w, LANES)), const((LANES, w)),
                  const((2 * tb, tb))],
        out_specs=[tile(w), tile(w), tile(w), tile(w), tile(w),
                   pl.BlockSpec((tb // CHUNK, 1, w), lambda i: (i, 0, 0)),
                   tile(w), tile(w)],
        out_shape=[jax.ShapeDtypeStruct((s, w), BF16)] * 5
        + [jax.ShapeDtypeStruct((s // CHUNK, 1, w), F32),
           jax.ShapeDtypeStruct((s, w), F32), jax.ShapeDtypeStruct((s, w), F32)],
        scratch_shapes=[pltpu.VMEM((SUBLANES, RWKV_PCOLS), F32)],
        compiler_params=_cparams(1),
        name="rwkv_prep",
    )(proj_r, row(mu), row(w0), wup, row(a0), aup, gup, row(k_k), row(k_a), row(r_k), seg_in, seg_in.T, tri)
    return outs


def _bdot(a, b):
    return jnp.dot(a, b, preferred_element_type=F32)


def _bdot_nt(a, b):
    return lax.dot_general(a, b, (((1,), (1,)), ((), ())), preferred_element_type=F32)


(MASK_STRICT, MASK_INCL, MASK_EYE, MASK_BLK8, MASK_OFF16, MASK_OFF32, MASK_OFF64) = range(7)


def _chunk_masks():
    a = jnp.arange(CHUNK)[:, None]
    b = jnp.arange(GROUP_W)[None, :] % CHUNK
    blk = lambda size: (a // size) == (b // size)
    masks = [a > b, a >= b, a == b, blk(8),
             blk(16) & ~blk(8), blk(32) & ~blk(16), blk(64) & ~blk(32)]
    return jnp.stack(masks).astype(F32)


def _rwkv_chunk_kernel(at_ref, bt_ref, kt_ref, rt_ref, v_ref, wc_ref, bonus_ref, g_ref,
                       gain_ref, bias_ref, mask_ref, head_ref, o_ref, state_ref, *, nc):
    @pl.when(pl.program_id(0) == 0)
    def _():
        state_ref[...] = jnp.zeros_like(state_ref)

    n = GROUP_W
    ng = RWKV_WIDTH // GROUP_W
    items = [(c, g) for c in range(nc) for g in range(ng)]
    lane_head = lax.broadcasted_iota(jnp.int32, (CHUNK, n), 1) >> 6

    def tile(ref, c, g):
        return ref[c * CHUNK:(c + 1) * CHUNK, g * n:(g + 1) * n]

    bf = lambda x: x.astype(BF16)

    def expand(x):
        x = bf(x)
        zero = jnp.zeros_like(x)
        return jnp.concatenate([jnp.where(lane_head == h, x, zero) for h in range(GROUP_HEADS)],
                               axis=0)

    def collapse(x):
        return (x[0:CHUNK] + x[CHUNK:2 * CHUNK] + x[2 * CHUNK:3 * CHUNK] + x[3 * CHUNK:4 * CHUNK])

    mask = lambda k: mask_ref[k]
    each = lambda f: [f(*it) for it in items]
    zipped = lambda f, *lists: [f(*args) for args in zip(*lists)]
    wdot = lambda w, e: _bdot(bf(w), e)

    at = each(lambda c, g: tile(at_ref, c, g))
    bt = each(lambda c, g: tile(bt_ref, c, g))
    kt = each(lambda c, g: tile(kt_ref, c, g))
    rt = each(lambda c, g: tile(rt_ref, c, g))
    wc = each(lambda c, g: wc_ref[c][:, g * n:(g + 1) * n])
    ae = [expand(x) for x in at]
    be = [expand(x) for x in bt]
    ke = [expand(x) for x in kt]
    ve = each(lambda c, g: expand(tile(v_ref, c, g)))
    bhe = zipped(lambda x, w: expand(x.astype(F32) * w), bt, wc)
    khe = zipped(lambda x, w: expand(x.astype(F32) * w), kt, wc)

    sc = zipped(lambda a, r, b, k: _bdot_nt(jnp.concatenate([a, r], axis=0),
                                            jnp.concatenate([b, k], axis=0)), at, rt, be, ke)
    lab = [x[:CHUNK, :n] * mask(MASK_STRICT) for x in sc]
    lak = [bf(x[:CHUNK, n:] * mask(MASK_STRICT)) for x in sc]
    arb = [bf(x[CHUNK:, :n] * mask(MASK_INCL)) for x in sc]
    ark = [bf(x[CHUNK:, n:] * mask(MASK_INCL)) for x in sc]
    bht = [collapse(x.T) for x in bhe]
    kht = [collapse(x.T) for x in khe]
    xv = zipped(lambda a, b, c, v: _bdot(jnp.concatenate([a, b, c], axis=0), v), lak, ark, kht, ve)
    lv = [x[:CHUNK] for x in xv]
    ark_v = [x[CHUNK:2 * CHUNK] for x in xv]
    kht_v = [x[2 * CHUNK:] for x in xv]

    pw = [x * mask(MASK_BLK8) for x in lab]
    minv = [mask(MASK_EYE) + x for x in pw]
    for _ in range(2):
        pw = zipped(lambda x: wdot(x, expand(x)), pw)
        minv = zipped(lambda x, m: m + wdot(x, expand(m)), pw, minv)
    for k in (MASK_OFF16, MASK_OFF32, MASK_OFF64):
        off = [expand(x * mask(k)) for x in lab]
        tmp = zipped(wdot, minv, off)
        minv = zipped(lambda m, t: m + wdot(t, expand(m)), minv, tmp)
    mb = [bf(m) for m in minv]

    p = [expand(_bdot(m, a)) for m, a in zip(mb, ae)]
    q = [expand(_bdot(m, expand(x))) for m, x in zip(mb, lv)]
    ab = zipped(lambda a, b: jnp.concatenate([a, b], axis=0), arb, bht)
    xp = zipped(_bdot, ab, p)
    xq = zipped(_bdot, ab, q)
    g_mat = zipped(lambda r, x: r.astype(F32) + x[:CHUNK], rt, xp)
    phi = zipped(lambda w, x: mask(MASK_EYE) * w + x[CHUNK:], wc, xp)
    gp = zipped(lambda g, f: bf(jnp.concatenate([g, f], axis=0)), g_mat, phi)
    y0 = zipped(lambda x, v: x[:CHUNK] + v, xq, ark_v)
    psi = zipped(lambda x, v: x[CHUNK:] + v, xq, kht_v)

    head = head_ref[...]

    def head_mean(x):
        h1, h2 = _split2(x)
        return (_bdot(h1, head) + _bdot(h2, head)) * (1.0 / HEAD_DIM)

    state = [state_ref[g] for g in range(ng)]
    for c in range(nc):
        idx = [c * ng + g for g in range(ng)]
        se = [expand(s) for s in state]
        xs = [_bdot(gp[i], se[g]) for g, i in enumerate(idx)]
        ys = [x[:CHUNK] + y0[i] for x, i in zip(xs, idx)]
        state = [x[CHUNK:] + psi[i] for x, i in zip(xs, idx)]
        for g, y in enumerate(ys):
            yc = y - head_mean(y)
            yn = yc * lax.rsqrt(head_mean(yc * yc) + RWKV_GN_EPS)
            rows = slice(c * CHUNK, (c + 1) * CHUNK)
            cols = slice(g * n, (g + 1) * n)
            out = (yn * gain_ref[:, cols] + bias_ref[:, cols] + bonus_ref[rows, cols]) * g_ref[rows, cols]
            o_ref[rows, cols] = out.astype(o_ref.dtype)
    for g in range(ng):
        state_ref[g] = state[g]


def _rwkv_chunks(at, bt, kt, rt, v, wc, bonus, g, gn_gain, gn_bias, *, nc):
    s, w = at.shape
    tb = nc * CHUNK
    assert s % tb == 0 and w % GROUP_W == 0
    tile = pl.BlockSpec((tb, w), lambda ti: (ti, 0))
    rowspec = pl.BlockSpec((1, w), lambda ti: (0, 0))
    masks = _chunk_masks()
    lane = jnp.arange(GROUP_W)
    head = (lane[:, None] // HEAD_DIM == lane[None, :] // HEAD_DIM).astype(BF16)
    return pl.pallas_call(
        functools.partial(_rwkv_chunk_kernel, nc=nc),
        grid=(s // tb,),
        in_specs=[tile, tile, tile, tile, tile,
                  pl.BlockSpec((nc, 1, w), lambda ti: (ti, 0, 0)),
                  tile, tile, rowspec, rowspec,
                  pl.BlockSpec(masks.shape, lambda ti: (0, 0, 0)),
                  pl.BlockSpec(head.shape, lambda ti: (0, 0))],
        out_specs=tile,
        out_shape=jax.ShapeDtypeStruct((s, w), BF16),
        scratch_shapes=[pltpu.VMEM((w // GROUP_W, CHUNK, GROUP_W), F32)],
        compiler_params=_cparams(1),
        name="rwkv_chunks",
    )(at, bt, kt, rt, v, wc, bonus, g, gn_gain.reshape(1, -1).astype(F32),
      gn_bias.reshape(1, -1).astype(F32), masks, head)


ROPE_ROWS = 256


def _proj_rope_kernel(pos_ref, invf_ref, x_ref, w_ref, o_ref, cos_ref, sin_ref, *, bm, bn):
    def first_half(rows):
        lane = lax.broadcasted_iota(jnp.int32, (rows, LANES), 1)
        return (lane & (HEAD_DIM - 1)) < (HEAD_DIM // 2)

    @pl.when(pl.program_id(1) == 0)
    def _():
        ang = pos_ref[...].astype(F32) * invf_ref[...]
        cos_ref[...] = jnp.cos(ang)
        sin = jnp.sin(ang)
        sin_ref[...] = jnp.where(first_half(bm), -sin, sin)

    half = first_half(ROPE_ROWS)

    for r in range(bm // ROPE_ROWS):
        rows = slice(r * ROPE_ROWS, (r + 1) * ROPE_ROWS)
        acc = jnp.dot(x_ref[rows, :], w_ref[...], preferred_element_type=F32)
        cos = cos_ref[rows, :]
        sin = sin_ref[rows, :]
        for c in range(bn // LANES):
            t = acc[:, c * LANES:(c + 1) * LANES]
            partner = jnp.where(half, pltpu.roll(t, LANES - HEAD_DIM // 2, axis=1),
                                pltpu.roll(t, HEAD_DIM // 2, axis=1))
            o_ref[rows, c * LANES:(c + 1) * LANES] = (t * cos + partner * sin).astype(o_ref.dtype)


def _proj_rope(xb, w_all, positions, *, bm, bn, n, col0):
    s, k = xb.shape
    assert n % bn == 0 and col0 % bn == 0
    jb = col0 // bn
    half = jnp.arange(0, HEAD_DIM, 2, dtype=F32)
    inv_freq = ROPE_THETA ** (-half / HEAD_DIM)
    invf = jnp.tile(inv_freq, 2 * LANES // HEAD_DIM).reshape(1, LANES)
    return pl.pallas_call(
        functools.partial(_proj_rope_kernel, bm=bm, bn=bn),
        grid=(s // bm, n // bn),
        in_specs=[pl.BlockSpec((bm, 1), lambda i, j: (i, 0)),
                  pl.BlockSpec((1, LANES), lambda i, j: (0, 0)),
                  pl.BlockSpec((bm, k), lambda i, j: (i, 0)),
                  pl.BlockSpec((k, bn), lambda i, j: (0, jb + j))],
        out_specs=pl.BlockSpec((bm, bn), lambda i, j: (i, j)),
        out_shape=jax.ShapeDtypeStruct((s, n), BF16),
        scratch_shapes=[pltpu.VMEM((bm, LANES), F32), pltpu.VMEM((bm, LANES), F32)],
        compiler_params=_cparams(2),
        name="proj_qk_rope",
    )(positions.reshape(s, 1), invf, xb, w_all)


ATT_STRIP = 128


ATT_KB = 512


def _attn_kernel(lam_ref, gain_ref, q_ref, k_ref, vt_ref, o_ref, acc_ref, s0_ref, s1_ref, *, bq):
    i = pl.program_id(1)
    nq = 2 * bq
    kb = ATT_KB
    q = q_ref[...]
    lane = lax.broadcasted_iota(jnp.int32, q.shape, 1)
    zero = jnp.zeros_like(q)
    q_all = jnp.concatenate([jnp.where(lane < HEAD_DIM, q, zero),
                             jnp.where(lane >= HEAD_DIM, q, zero)], axis=0)
    s_slots = (s0_ref, s1_ref)
    all_cols = ((0, nq),)
    late_cols = ((kb, bq), (bq + kb, nq))

    def scores(j, slot):
        start = pl.multiple_of(j * kb, kb)
        s_slots[slot][...] = lax.dot_general(k_ref[pl.ds(start, kb), :], q_all,
                                             (((1,), (1,)), ((), ())), preferred_element_type=F32)

    def consume(j, slot, m, l, groups, key0):
        start = pl.multiple_of(j * kb, kb)
        vtb = vt_ref[:, pl.ds(start, kb)]
        m_parts = {c: m[:, c * ATT_STRIP:(c + 1) * ATT_STRIP] for c in range(nq // ATT_STRIP)}
        l_parts = {c: l[:, c * ATT_STRIP:(c + 1) * ATT_STRIP] for c in range(nq // ATT_STRIP)}
        for lo, hi in groups:
            p_strips, alphas = [], []
            for c in range(lo // ATT_STRIP, hi // ATT_STRIP):
                sl = slice(c * ATT_STRIP, (c + 1) * ATT_STRIP)
                s = s_slots[slot][:, sl]
                q_first = (c * ATT_STRIP) % bq
                if key0 is not None and key0 + kb - 1 > q_first:
                    key = lax.broadcasted_iota(jnp.int32, s.shape, 0) + key0
                    qry = lax.broadcasted_iota(jnp.int32, s.shape, 1) + q_first
                    s = jnp.where(key <= qry, s, -1e30)
                mn = jnp.maximum(m_parts[c], jnp.max(s, axis=0, keepdims=True))
                p = jnp.exp2(s - mn)
                alpha = jnp.exp2(m_parts[c] - mn)
                l_parts[c] = alpha * l_parts[c] + jnp.sum(p, axis=0, keepdims=True)
                m_parts[c] = mn
                alphas.append(alpha)
                p_strips.append(p.astype(BF16))
            p_all = jnp.concatenate(p_strips, axis=1)
            alpha = jnp.concatenate(alphas, axis=1)
            acc_ref[:, lo:hi] = alpha * acc_ref[:, lo:hi] + jnp.dot(vtb, p_all,
                                                                    preferred_element_type=F32)
        order = range(nq // ATT_STRIP)
        return (jnp.concatenate([m_parts[c] for c in order], axis=1),
                jnp.concatenate([l_parts[c] for c in order], axis=1))

    def pair(jj, carry):
        j = 2 * jj
        scores(j + 1, 1)
        m, l = consume(j, 0, carry[0], carry[1], all_cols, None)
        scores(j + 2, 0)
        return consume(j + 1, 1, m, l, all_cols, None)

    acc_ref[...] = jnp.zeros_like(acc_ref)
    scores(0, 0)
    init = (jnp.full((1, nq), -1e30, F32), jnp.zeros((1, nq), F32))
    m, l = lax.fori_loop(0, i, pair, init)
    scores(2 * i + 1, 1)
    m, l = consume(2 * i, 0, m, l, all_cols, 0)
    m, l = consume(2 * i + 1, 1, m, l, late_cols, kb)

    lam_v = lam_ref[...]
    lam = (jnp.exp(jnp.sum(lam_v[0:1] * lam_v[1:2], axis=1, keepdims=True))
           - jnp.exp(jnp.sum(lam_v[2:3] * lam_v[3:4], axis=1, keepdims=True)) + LAMBDA_INIT)
    o = acc_ref[...] * (1.0 / l)
    out = o[:, :bq] - lam * o[:, bq:]
    out = out * lax.rsqrt(jnp.mean(jnp.square(out), axis=0, keepdims=True) + SUBLN_EPS)
    out = out * gain_ref[...] * (1.0 - LAMBDA_INIT)
    o_ref[...] = out.T.astype(o_ref.dtype)


def _attention(qk, vt, lam_vecs, subln_gain, *, bq):
    s = qk.shape[0]
    assert s % bq == 0 and bq == 2 * ATT_KB
    return pl.pallas_call(
        functools.partial(_attn_kernel, bq=bq),
        grid=(DIFF_HEADS, s // bq),
        in_specs=[pl.BlockSpec((4, HEAD_DIM), lambda h, i: (0, 0)),
                  pl.BlockSpec((DIFF_V_DIM, 1), lambda h, i: (0, 0)),
                  pl.BlockSpec((bq, DIFF_V_DIM), lambda h, i: (i, h)),
                  pl.BlockSpec((s, DIFF_V_DIM), lambda h, i: (0, DIFF_HEADS + h)),
                  pl.BlockSpec((DIFF_V_DIM, s), lambda h, i: (h, 0))],
        out_specs=pl.BlockSpec((bq, DIFF_V_DIM), lambda h, i: (i, h)),
        out_shape=jax.ShapeDtypeStruct((s, DIFF_WIDTH), BF16),
        scratch_shapes=[pltpu.VMEM((DIFF_V_DIM, 2 * bq), F32), pltpu.VMEM((ATT_KB, 2 * bq), F32),
                        pltpu.VMEM((ATT_KB, 2 * bq), F32)],
        compiler_params=_cparams(2),
        name="diff_attn",
    )(lam_vecs, subln_gain.reshape(-1, 1).astype(F32), qk, qk, vt)


def _layer_norm(z, gain, bias):
    mean = jnp.mean(z, axis=-1, keepdims=True)
    zc = z - mean
    var = jnp.mean(zc * zc, axis=-1, keepdims=True)
    return zc * lax.rsqrt(var + LN_EPS) * gain + bias


LN_ROWS = 128


def _outproj_kernel(yr_ref, yd_ref, wr_ref, wd_ref, x_ref, gain_ref, bias_ref, o32_ref, o16_ref):
    for r in range(x_ref.shape[0] // LN_ROWS):
        rows = slice(r * LN_ROWS, (r + 1) * LN_ROWS)
        mix = (jnp.dot(yr_ref[rows, :], wr_ref[...], preferred_element_type=F32)
               + jnp.dot(yd_ref[rows, :], wd_ref[...], preferred_element_type=F32))
        y = _layer_norm(DEEPNORM_ALPHA * x_ref[rows, :] + mix, gain_ref[...], bias_ref[...])
        o32_ref[rows, :] = y
        o16_ref[rows, :] = y.astype(BF16)


def _outproj(y_rwkv, y_diff, w_r, w_d, x, gain, bias, *, bm):
    s = x.shape[0]
    const = lambda shape: pl.BlockSpec(shape, lambda i: (0, 0))
    return pl.pallas_call(
        _outproj_kernel,
        grid=(s // bm,),
        in_specs=[pl.BlockSpec((bm, RWKV_WIDTH), lambda i: (i, 0)),
                  pl.BlockSpec((bm, DIFF_WIDTH), lambda i: (i, 0)),
                  const((RWKV_WIDTH, D_MODEL)), const((DIFF_WIDTH, D_MODEL)),
                  pl.BlockSpec((bm, D_MODEL), lambda i: (i, 0)),
                  const((1, D_MODEL)), const((1, D_MODEL))],
        out_specs=[pl.BlockSpec((bm, D_MODEL), lambda i: (i, 0))] * 2,
        out_shape=[jax.ShapeDtypeStruct((s, D_MODEL), F32), jax.ShapeDtypeStruct((s, D_MODEL), BF16)],
        compiler_params=_cparams(1),
        name="outproj_ln",
    )(y_rwkv, y_diff, w_r, w_d, x, gain.reshape(1, -1), bias.reshape(1, -1))


def _ff2_kernel(h_ref, w_ref, x_ref, gain_ref, bias_ref, o_ref):
    kk = pl.program_id(1)
    part = lambda: jnp.dot(h_ref[...], w_ref[...], preferred_element_type=F32)

    @pl.when(kk == 0)
    def _():
        o_ref[...] = part()

    @pl.when((kk > 0) & (kk < pl.num_programs(1) - 1))
    def _():
        o_ref[...] += part()

    @pl.when(kk == pl.num_programs(1) - 1)
    def _():
        for r in range(o_ref.shape[0] // LN_ROWS):
            rows = slice(r * LN_ROWS, (r + 1) * LN_ROWS)
            acc = o_ref[rows, :] + jnp.dot(h_ref[rows, :], w_ref[...], preferred_element_type=F32)
            o_ref[rows, :] = _layer_norm(DEEPNORM_ALPHA * x_ref[rows, :] + acc,
                                         gain_ref[...], bias_ref[...])


def _ff2(h, w, x, gain, bias, *, bm, bk):
    s, kdim = h.shape
    assert s % bm == 0 and kdim % bk == 0 and kdim // bk >= 2
    const = lambda shape: pl.BlockSpec(shape, lambda i, k: (0, 0))
    return pl.pallas_call(
        _ff2_kernel,
        grid=(s // bm, kdim // bk),
        in_specs=[pl.BlockSpec((bm, bk), lambda i, k: (i, k)),
                  pl.BlockSpec((bk, D_MODEL), lambda i, k: (k, 0)),
                  pl.BlockSpec((bm, D_MODEL), lambda i, k: (i, 0)),
                  const((1, D_MODEL)), const((1, D_MODEL))],
        out_specs=pl.BlockSpec((bm, D_MODEL), lambda i, k: (i, 0)),
        out_shape=jax.ShapeDtypeStruct((s, D_MODEL), F32),
        compiler_params=_cparams(2),
        name="ff2_ln",
    )(h, w, x, gain.reshape(1, -1), bias.reshape(1, -1))


def _pad_rows(w, rows):
    return jnp.pad(w, ((0, rows - w.shape[0]), (0, 0)))


def _layer(x, positions, w_in, mu_shift, w0, w_up, a0, a_up, g_up, k_k, k_a, r_k, gn_gain, gn_bias,
           lq1, lk1, lq2, lk2, subln_gain, w_out, ln1_gain, ln1_bias, w_ff1, w_ff2, ln2_gain, ln2_bias,
           *, bm, prep_tb, nc, bq):
    w3 = 3 * RWKV_WIDTH
    rwkv_cols = w3 + DECAY_LORA + AAA_LORA + GATE_LORA
    qk0 = RWKV_PCOLS
    v0 = qk0 + 2 * DIFF_WIDTH

    def pad_cols(a, lo, hi, width):
        return jnp.pad(a[..., lo:hi], [(0, 0)] * (a.ndim - 1) + [(0, width - (hi - lo))])

    def rwkv_layout(a):
        return jnp.concatenate([
            a[..., :w3],
            pad_cols(a, w3, w3 + DECAY_LORA, DECAY_PAD),
            pad_cols(a, w3 + DECAY_LORA, w3 + DECAY_LORA + AAA_LORA, AAA_PAD),
            pad_cols(a, w3 + DECAY_LORA + AAA_LORA, rwkv_cols, GATE_PAD)], axis=-1)

    w_all = _relayout_w_in(w_in.T, bk=256)
    proj_r, xb = _proj_cast(x, w_all, bm=bm, bn=RWKV_PCOLS // 2, n=RWKV_PCOLS)
    vt = _matmul(xb, w_all, bm=bm, bn=512, n=DIFF_WIDTH, col0=v0, out_dtype=BF16,
                 transpose_out=True, name="proj_vt")

    at, bt, kt, rt, v_r, wc, bonus, gate = _rwkv_prep(
        proj_r, rwkv_layout(mu_shift), w0,
        _pad_rows(w_up, DECAY_PAD).astype(BF16), a0, _pad_rows(a_up, AAA_PAD).astype(BF16),
        _pad_rows(g_up, GATE_PAD).astype(BF16), k_k, k_a, r_k.reshape(-1), tb=prep_tb)
    y_rwkv = _rwkv_chunks(at, bt, kt, rt, v_r, wc, bonus, gate, gn_gain, gn_bias, nc=nc)

    qk = _proj_rope(xb, w_all, positions, bm=bm, bn=512, n=2 * DIFF_WIDTH, col0=qk0)
    lam_vecs = jnp.stack([lq1, lk1, lq2, lk2]).astype(F32)
    y_diff = _attention(qk, vt, lam_vecs, subln_gain, bq=bq)

    wo = w_out.astype(BF16)
    x1, x1b = _outproj(y_rwkv, y_diff, wo[:RWKV_WIDTH], wo[RWKV_WIDTH:], x, ln1_gain, ln1_bias, bm=512)
    h = _ff1(x1b, w_ff1, bm=2 * bm, bn=1024)
    return _ff2(h, w_ff2.astype(BF16), x1, ln2_gain, ln2_bias, bm=1024, bk=1024)


def kernel(x, positions, w_in, mu_shift, rwkv_w0, rwkv_w_up, rwkv_a0, rwkv_a_up, rwkv_g_up, rwkv_k_k,
           rwkv_k_a, rwkv_r_k, rwkv_gn_gain, rwkv_gn_bias, diff_lambda_q1, diff_lambda_k1,
           diff_lambda_q2, diff_lambda_k2, diff_subln_gain, w_out, ln1_gain, ln1_bias, w_ff1, w_ff2,
           ln2_gain, ln2_bias):
    batch, seq, _ = x.shape
    assert batch == 1 and w_in.shape[0] == 1
    out = _layer(
        x[0], positions[0], w_in[0], mu_shift[0], rwkv_w0[0], rwkv_w_up[0], rwkv_a0[0], rwkv_a_up[0],
        rwkv_g_up[0], rwkv_k_k[0], rwkv_k_a[0], rwkv_r_k[0], rwkv_gn_gain[0], rwkv_gn_bias[0],
        diff_lambda_q1[0], diff_lambda_k1[0], diff_lambda_q2[0], diff_lambda_k2[0],
        diff_subln_gain[0], w_out[0], ln1_gain[0], ln1_bias[0], w_ff1[0], w_ff2[0], ln2_gain[0],
        ln2_bias[0], bm=1024, prep_tb=256, nc=4, bq=1024)
    return out[None]
```

```python
import functools
import math

import jax
import jax.numpy as jnp
from jax import lax
from jax.experimental import pallas as pl
from jax.experimental.pallas import tpu as pltpu

F32 = jnp.float32
BF16 = jnp.bfloat16

D_MODEL = 2048
RWKV_WIDTH = 1024
HEAD_DIM = 64
DECAY_LORA = 64
AAA_LORA = 64
GATE_LORA = 160
DIFF_WIDTH = 1024
DIFF_V_DIM = 128
DIFF_HEADS = 8
D_FF = 8192
ROPE_THETA = 10000.0
LN_EPS = 1e-5
RWKV_GN_EPS = 64e-5
SUBLN_EPS = 1e-5
DEEPNORM_ALPHA = 2.0 ** 0.25
LAMBDA_INIT = 0.8 - 0.6 * math.exp(0.0)
LOG2_E = math.log2(math.e)

LANES = 128
SUBLANES = 8
VMEM_LIMIT_BYTES = 56 * 1024 * 1024

DECAY_PAD = 128
AAA_PAD = 128
GATE_PAD = 256
RWKV_PCOLS = 3 * RWKV_WIDTH + DECAY_PAD + AAA_PAD + GATE_PAD
CHUNK = 64
GROUP_HEADS = 4
GROUP_W = GROUP_HEADS * HEAD_DIM


def _cparams(n_axes):
    return pltpu.CompilerParams(
        dimension_semantics=("arbitrary",) * n_axes,
        vmem_limit_bytes=VMEM_LIMIT_BYTES,
    )


def _dot(a, b):
    return jnp.dot(a.astype(BF16), b.astype(BF16), preferred_element_type=F32)


def _split3(x):
    h1 = x.astype(BF16)
    r1 = x - h1.astype(F32)
    h2 = r1.astype(BF16)
    h3 = (r1 - h2.astype(F32)).astype(BF16)
    return h1, h2, h3


def _split2(x):
    h1 = x.astype(BF16)
    h2 = (x - h1.astype(F32)).astype(BF16)
    return h1, h2


def _mm_kernel(a_ref, b_ref, o_ref, *, transpose_out):
    acc = jnp.dot(a_ref[...], b_ref[...], preferred_element_type=F32)
    if transpose_out:
        acc = acc.T
    o_ref[...] = acc.astype(o_ref.dtype)


def _matmul(a, b, *, bm, bn, n, col0=0, out_dtype, transpose_out=False, name):
    m, k = a.shape
    assert m % bm == 0 and n % bn == 0 and col0 % bn == 0
    jb = col0 // bn
    if transpose_out:
        out_spec = pl.BlockSpec((bn, bm), lambda i, j: (j, i))
        out_shape = jax.ShapeDtypeStruct((n, m), out_dtype)
    else:
        out_spec = pl.BlockSpec((bm, bn), lambda i, j: (i, j))
        out_shape = jax.ShapeDtypeStruct((m, n), out_dtype)
    return pl.pallas_call(
        functools.partial(_mm_kernel, transpose_out=transpose_out),
        grid=(m // bm, n // bn),
        in_specs=[pl.BlockSpec((bm, k), lambda i, j: (i, 0)),
                  pl.BlockSpec((k, bn), lambda i, j: (0, jb + j))],
        out_specs=out_spec,
        out_shape=out_shape,
        compiler_params=_cparams(2),
        name=name,
    )(a, b)


def _proj_cast_kernel(x_ref, w_ref, o_ref, xb_ref):
    @pl.when(pl.program_id(1) == 0)
    def _():
        xb_ref[...] = x_ref[...].astype(BF16)

    o_ref[...] = jnp.dot(xb_ref[...], w_ref[...], preferred_element_type=F32)


def _proj_cast(x, w_all, *, bm, bn, n):
    m, k = x.shape
    assert m % bm == 0 and n % bn == 0
    return pl.pallas_call(
        _proj_cast_kernel,
        grid=(m // bm, n // bn),
        in_specs=[pl.BlockSpec((bm, k), lambda i, j: (i, 0)),
                  pl.BlockSpec((k, bn), lambda i, j: (0, j))],
        out_specs=[pl.BlockSpec((bm, bn), lambda i, j: (i, j)),
                   pl.BlockSpec((bm, k), lambda i, j: (i, 0))],
        out_shape=[jax.ShapeDtypeStruct((m, n), F32), jax.ShapeDtypeStruct((m, k), BF16)],
        compiler_params=_cparams(2),
        name="proj_rwkv",
    )(x, w_all)


def _ff1_kernel(x_ref, w_ref, w2_ref, o_ref, w2b_ref, wb_ref):
    @pl.when(pl.program_id(1) == 0)
    def _():
        wb_ref[...] = w_ref[...].astype(BF16)

    acc = jnp.dot(x_ref[...], wb_ref[...], preferred_element_type=F32)
    o_ref[...] = jnp.square(jnp.maximum(acc, 0.0)).astype(o_ref.dtype)
    w2b_ref[...] = w2_ref[...].astype(BF16)


def _ff1(x, w, w2, *, bm, bn):
    m, k = x.shape
    _, n = w.shape
    assert m % bm == 0 and n % bn == 0
    steps = (n // bn) * (m // bm)
    rows2, cols2 = w2.shape
    assert rows2 % steps == 0 and (rows2 // steps) % (2 * SUBLANES) == 0
    slab = pl.BlockSpec((rows2 // steps, cols2), lambda j, i: (j * (m // bm) + i, 0))
    return pl.pallas_call(
        _ff1_kernel,
        grid=(n // bn, m // bm),
        in_specs=[pl.BlockSpec((bm, k), lambda j, i: (i, 0)),
                  pl.BlockSpec((k, bn), lambda j, i: (0, j)),
                  slab],
        out_specs=[pl.BlockSpec((bm, bn), lambda j, i: (i, j)), slab],
        out_shape=[jax.ShapeDtypeStruct((m, n), BF16), jax.ShapeDtypeStruct(w2.shape, BF16)],
        scratch_shapes=[pltpu.VMEM((k, bn), BF16)],
        compiler_params=_cparams(2),
        name="ff1",
    )(x, w, w2)


def _relayout_kernel(wt_ref, o_ref):
    cols = wt_ref.shape[1]
    w3 = 3 * RWKV_WIDTH
    a0 = w3 + DECAY_LORA
    g0 = a0 + AAA_LORA
    q0 = g0 + GATE_LORA
    zeros = lambda n: jnp.zeros((n, cols), F32)
    pieces = [wt_ref[:a0, :], zeros(DECAY_PAD - DECAY_LORA),
              wt_ref[a0:g0, :], zeros(AAA_PAD - AAA_LORA),
              wt_ref[g0:q0, :], zeros(GATE_PAD - GATE_LORA),
              wt_ref[q0:q0 + DIFF_WIDTH, :] * (HEAD_DIM ** -0.5 * LOG2_E),
              wt_ref[q0 + DIFF_WIDTH:, :]]
    for c0 in range(0, o_ref.shape[1], RELAYOUT_COLS):
        rows = slice(c0, c0 + RELAYOUT_COLS)
        tile = jnp.concatenate(_row_window(pieces, c0, c0 + RELAYOUT_COLS), axis=0)
        o_ref[:, rows] = tile.T.astype(o_ref.dtype)


def _row_window(pieces, lo, hi):
    out, start = [], 0
    for p in pieces:
        stop = start + p.shape[0]
        a, b = max(lo, start), min(hi, stop)
        if a < b:
            out.append(p[a - start:b - start])
        start = stop
    return out


RELAYOUT_COLS = 512


def _relayout_w_in(w_in_t, *, bk):
    n, k = w_in_t.shape
    n_out = RWKV_PCOLS + 3 * DIFF_WIDTH
    assert n == n_out - (RWKV_PCOLS - 3 * RWKV_WIDTH - DECAY_LORA - AAA_LORA - GATE_LORA)
    assert n_out % RELAYOUT_COLS == 0
    return pl.pallas_call(
        _relayout_kernel,
        grid=(k // bk,),
        in_specs=[pl.BlockSpec((n, bk), lambda i: (0, i))],
        out_specs=pl.BlockSpec((bk, n_out), lambda i: (i, 0)),
        out_shape=jax.ShapeDtypeStruct((k, n_out), BF16),
        compiler_params=_cparams(1),
        name="relayout_w_in",
    )(w_in_t)


def _rwkv_prep_kernel(p_ref, mu_ref, w0_ref, wup_ref, a0_ref, aup_ref, gup_ref, kk_ref, ka_ref,
                      rk_ref, segin_ref, segout_ref, tri_ref,
                      at_ref, bt_ref, kt_ref, rt_ref, v_ref, wc_ref, bonus_ref, g_ref,
                      carry_ref, *, tb):
    i = pl.program_id(0)

    @pl.when(i == 0)
    def _():
        carry_ref[...] = jnp.zeros_like(carry_ref)

    def mixed(lo, hi):
        p = p_ref[:, lo:hi]
        prev_last = carry_ref[0:1, lo:hi]
        shifted = pltpu.roll(p, 1, axis=0)
        row = lax.broadcasted_iota(jnp.int32, p.shape, 0)
        prev = jnp.where(row == 0, prev_last, shifted)
        carry_ref[0:1, lo:hi] = p[tb - 1:tb, :]
        return p + (prev - p) * mu_ref[:, lo:hi]

    w3 = 3 * RWKV_WIDTH
    r = mixed(0, RWKV_WIDTH)
    k = mixed(RWKV_WIDTH, 2 * RWKV_WIDTH)
    v = mixed(2 * RWKV_WIDTH, w3)
    w_d = mixed(w3, w3 + DECAY_PAD)
    a_d = mixed(w3 + DECAY_PAD, w3 + DECAY_PAD + AAA_PAD)
    g_d = mixed(w3 + DECAY_PAD + AAA_PAD, RWKV_PCOLS)

    z = w0_ref[...] + _dot(jnp.tanh(w_d), wup_ref[...])
    softplus = jnp.maximum(-z, 0.0) + jnp.log(1.0 + jnp.exp(-jnp.abs(z)))
    lw = -jnp.exp(-softplus - 0.5)
    a = jax.nn.sigmoid(a0_ref[...] + _dot(a_d, aup_ref[...]))
    g_ref[...] = _dot(jax.nn.sigmoid(g_d), gup_ref[...])

    def segsum(x):
        h1, h2 = _split2(jnp.dot(x.astype(BF16), segin_ref[...], preferred_element_type=F32))
        return (jnp.dot(h1, segout_ref[...], preferred_element_type=F32)
                + jnp.dot(h2, segout_ref[...], preferred_element_type=F32))

    kk0 = k * kk_ref[...]
    kk = kk0 * lax.rsqrt(jnp.maximum(segsum(kk0 * kk0), 1e-24))
    k2 = k * (1.0 + (a - 1.0) * ka_ref[...])
    bonus_ref[...] = segsum(r * k2 * rk_ref[...]) * v

    tri = tri_ref[...]
    l1, l2, l3 = _split3(lw)
    cs = (jnp.dot(tri, l1, preferred_element_type=F32)
          + jnp.dot(tri, l2, preferred_element_type=F32)
          + jnp.dot(tri, l3, preferred_element_type=F32))
    cl = cs[:tb]
    tot = cs[tb:]
    e_neg = jnp.exp(-cl)
    at_ref[...] = (-kk * jnp.exp(cl - lw)).astype(BF16)
    bt_ref[...] = (kk * a * e_neg).astype(BF16)
    kt_ref[...] = (k2 * e_neg).astype(BF16)
    rt_ref[...] = (r * jnp.exp(cl)).astype(BF16)
    v_ref[...] = v.astype(BF16)
    for c in range(tb // CHUNK):
        wc_ref[c] = jnp.exp(tot[c * CHUNK:c * CHUNK + 1, :])


def _rwkv_prep(proj_r, mu, w0, wup, a0, aup, gup, k_k, k_a, r_k, *, tb):
    s = proj_r.shape[0]
    assert s % tb == 0 and tb % CHUNK == 0
    w = RWKV_WIDTH
    lane = jnp.arange(w)
    seg_in = (lane[:, None] // HEAD_DIM == jnp.arange(LANES)[None, :]).astype(BF16)
    t = jnp.arange(tb)
    same = t[:, None] // CHUNK == t[None, :] // CHUNK
    tri = jnp.concatenate([(same & (t[:, None] >= t[None, :])), same], axis=0).astype(BF16)
    row = lambda x: x.reshape(1, -1).astype(F32)
    const = lambda shape: pl.BlockSpec(shape, lambda i: (0,) * len(shape))
    tile = lambda width: pl.BlockSpec((tb, width), lambda i: (i, 0))
    outs = pl.pallas_call(
        functools.partial(_rwkv_prep_kernel, tb=tb),
        grid=(s // tb,),
        in_specs=[tile(RWKV_PCOLS), const((1, RWKV_PCOLS)), const((1, w)), const((DECAY_PAD, w)),
                  const((1, w)), const((AAA_PAD, w)), const((GATE_PAD, w)), const((1, w)),
                  const((1, w)), const((1, w)), const((w, LANES)), const((LANES, w)),
                  const((2 * tb, tb))],
        out_specs=[tile(w), tile(w), tile(w), tile(w), tile(w),
                   pl.BlockSpec((tb // CHUNK, 1, w), lambda i: (i, 0, 0)),
                   tile(w), tile(w)],
        out_shape=[jax.ShapeDtypeStruct((s, w), BF16)] * 5
        + [jax.ShapeDtypeStruct((s // CHUNK, 1, w), F32),
           jax.ShapeDtypeStruct((s, w), F32), jax.ShapeDtypeStruct((s, w), F32)],
        scratch_shapes=[pltpu.VMEM((SUBLANES, RWKV_PCOLS), F32)],
        compiler_params=_cparams(1),
        name="rwkv_prep",
    )(proj_r, row(mu), row(w0), wup, row(a0), aup, gup, row(k_k), row(k_a), row(r_k), seg_in, seg_in.T, tri)
    return outs


def _bdot(a, b):
    return jnp.dot(a, b, preferred_element_type=F32)


def _bdot_nt(a, b):
    return lax.dot_general(a, b, (((1,), (1,)), ((), ())), preferred_element_type=F32)


(MASK_STRICT, MASK_INCL, MASK_EYE, MASK_BLK8, MASK_OFF16, MASK_OFF32, MASK_OFF64) = range(7)


def _chunk_masks():
    a = jnp.arange(CHUNK)[:, None]
    b = jnp.arange(GROUP_W)[None, :] % CHUNK
    blk = lambda size: (a // size) == (b // size)
    masks = [a > b, a >= b, a == b, blk(8),
             blk(16) & ~blk(8), blk(32) & ~blk(16), blk(64) & ~blk(32)]
    return jnp.stack(masks).astype(F32)


def _rwkv_chunk_kernel(at_ref, bt_ref, kt_ref, rt_ref, v_ref, wc_ref, bonus_ref, g_ref,
                       gain_ref, bias_ref, mask_ref, head_ref, o_ref, state_ref, *, nc):
    @pl.when(pl.program_id(0) == 0)
    def _():
        state_ref[...] = jnp.zeros_like(state_ref)

    n = GROUP_W
    ng = RWKV_WIDTH // GROUP_W
    items = [(c, g) for c in range(nc) for g in range(ng)]
    lane_head = lax.broadcasted_iota(jnp.int32, (CHUNK, n), 1) >> 6

    def tile(ref, c, g):
        return ref[c * CHUNK:(c + 1) * CHUNK, g * n:(g + 1) * n]

    bf = lambda x: x.astype(BF16)

    def expand(x):
        x = bf(x)
        zero = jnp.zeros_like(x)
        return jnp.concatenate([jnp.where(lane_head == h, x, zero) for h in range(GROUP_HEADS)],
                               axis=0)

    def collapse(x):
        return (x[0:CHUNK] + x[CHUNK:2 * CHUNK] + x[2 * CHUNK:3 * CHUNK] + x[3 * CHUNK:4 * CHUNK])

    mask = lambda k: mask_ref[k]
    each = lambda f: [f(*it) for it in items]
    zipped = lambda f, *lists: [f(*args) for args in zip(*lists)]
    wdot = lambda w, e: _bdot(bf(w), e)

    at = each(lambda c, g: tile(at_ref, c, g))
    bt = each(lambda c, g: tile(bt_ref, c, g))
    kt = each(lambda c, g: tile(kt_ref, c, g))
    rt = each(lambda c, g: tile(rt_ref, c, g))
    wc = each(lambda c, g: wc_ref[c][:, g * n:(g + 1) * n])
    ae = [expand(x) for x in at]
    be = [expand(x) for x in bt]
    ke = [expand(x) for x in kt]
    ve = each(lambda c, g: expand(tile(v_ref, c, g)))
    bhe = zipped(lambda x, w: expand(x.astype(F32) * w), bt, wc)
    khe = zipped(lambda x, w: expand(x.astype(F32) * w), kt, wc)

    sc = zipped(lambda a, r, b, k: _bdot_nt(jnp.concatenate([a, r], axis=0),
                                            jnp.concatenate([b, k], axis=0)), at, rt, be, ke)
    lab = [x[:CHUNK, :n] * mask(MASK_STRICT) for x in sc]
    lak = [bf(x[:CHUNK, n:] * mask(MASK_STRICT)) for x in sc]
    arb = [bf(x[CHUNK:, :n] * mask(MASK_INCL)) for x in sc]
    ark = [bf(x[CHUNK:, n:] * mask(MASK_INCL)) for x in sc]
    bht = [collapse(x.T) for x in bhe]
    kht = [collapse(x.T) for x in khe]
    xv = zipped(lambda a, b, c, v: _bdot(jnp.concatenate([a, b, c], axis=0), v), lak, ark, kht, ve)
    lv = [x[:CHUNK] for x in xv]
    ark_v = [x[CHUNK:2 * CHUNK] for x in xv]
    kht_v = [x[2 * CHUNK:] for x in xv]

    pw = [x * mask(MASK_BLK8) for x in lab]
    minv = [mask(MASK_EYE) + x for x in pw]
    for _ in range(2):
        pw = zipped(lambda x: wdot(x, expand(x)), pw)
        minv = zipped(lambda x, m: m + wdot(x, expand(m)), pw, minv)
    for k in (MASK_OFF16, MASK_OFF32, MASK_OFF64):
        off = [expand(x * mask(k)) for x in lab]
        tmp = zipped(wdot, minv, off)
        minv = zipped(lambda m, t: m + wdot(t, expand(m)), minv, tmp)
    mb = [bf(m) for m in minv]

    p = [expand(_bdot(m, a)) for m, a in zip(mb, ae)]
    q = [expand(_bdot(m, expand(x))) for m, x in zip(mb, lv)]
    ab = zipped(lambda a, b: jnp.concatenate([a, b], axis=0), arb, bht)
    xp = zipped(_bdot, ab, p)
    xq = zipped(_bdot, ab, q)
    g_mat = zipped(lambda r, x: r.astype(F32) + x[:CHUNK], rt, xp)
    phi = zipped(lambda w, x: mask(MASK_EYE) * w + x[CHUNK:], wc, xp)
    gp = zipped(lambda g, f: bf(jnp.concatenate([g, f], axis=0)), g_mat, phi)
    y0 = zipped(lambda x, v: x[:CHUNK] + v, xq, ark_v)
    psi = zipped(lambda x, v: x[CHUNK:] + v, xq, kht_v)

    head = head_ref[...]

    def head_mean(x):
        h1, h2 = _split2(x)
        return (_bdot(h1, head) + _bdot(h2, head)) * (1.0 / HEAD_DIM)

    state = [state_ref[g] for g in range(ng)]
    for c in range(nc):
        idx = [c * ng + g for g in range(ng)]
        se = [expand(s) for s in state]
        xs = [_bdot(gp[i], se[g]) for g, i in enumerate(idx)]
        ys = [x[:CHUNK] + y0[i] for x, i in zip(xs, idx)]
        state = [x[CHUNK:] + psi[i] for x, i in zip(xs, idx)]
        for g, y in enumerate(ys):
            yc = y - head_mean(y)
            yn = yc * lax.rsqrt(head_mean(yc * yc) + RWKV_GN_EPS)
            rows = slice(c * CHUNK, (c + 1) * CHUNK)
            cols = slice(g * n, (g + 1) * n)
            out = (yn * gain_ref[:, cols] + bias_ref[:, cols] + bonus_ref[rows, cols]) * g_ref[rows, cols]
            o_ref[rows, cols] = out.astype(o_ref.dtype)
    for g in range(ng):
        state_ref[g] = state[g]


def _rwkv_chunks(at, bt, kt, rt, v, wc, bonus, g, gn_gain, gn_bias, *, nc):
    s, w = at.shape
    tb = nc * CHUNK
    assert s % tb == 0 and w % GROUP_W == 0
    tile = pl.BlockSpec((tb, w), lambda ti: (ti, 0))
    rowspec = pl.BlockSpec((1, w), lambda ti: (0, 0))
    masks = _chunk_masks()
    lane = jnp.arange(GROUP_W)
    head = (lane[:, None] // HEAD_DIM == lane[None, :] // HEAD_DIM).astype(BF16)
    return pl.pallas_call(
        functools.partial(_rwkv_chunk_kernel, nc=nc),
        grid=(s // tb,),
        in_specs=[tile, tile, tile, tile, tile,
                  pl.BlockSpec((nc, 1, w), lambda ti: (ti, 0, 0)),
                  tile, tile, rowspec, rowspec,
                  pl.BlockSpec(masks.shape, lambda ti: (0, 0, 0)),
                  pl.BlockSpec(head.shape, lambda ti: (0, 0))],
        out_specs=tile,
        out_shape=jax.ShapeDtypeStruct((s, w), BF16),
        scratch_shapes=[pltpu.VMEM((w // GROUP_W, CHUNK, GROUP_W), F32)],
        compiler_params=_cparams(1),
        name="rwkv_chunks",
    )(at, bt, kt, rt, v, wc, bonus, g, gn_gain.reshape(1, -1).astype(F32),
      gn_bias.reshape(1, -1).astype(F32), masks, head)


ROPE_ROWS = 256


def _proj_rope_kernel(pos_ref, invf_ref, x_ref, w_ref, o_ref, cos_ref, sin_ref, *, bm, bn):
    def first_half(rows):
        lane = lax.broadcasted_iota(jnp.int32, (rows, LANES), 1)
        return (lane & (HEAD_DIM - 1)) < (HEAD_DIM // 2)

    @pl.when(pl.program_id(1) == 0)
    def _():
        ang = pos_ref[...].astype(F32) * invf_ref[...]
        cos_ref[...] = jnp.cos(ang)
        sin = jnp.sin(ang)
        sin_ref[...] = jnp.where(first_half(bm), -sin, sin)

    half = first_half(ROPE_ROWS)

    for r in range(bm // ROPE_ROWS):
        rows = slice(r * ROPE_ROWS, (r + 1) * ROPE_ROWS)
        acc = jnp.dot(x_ref[rows, :], w_ref[...], preferred_element_type=F32)
        cos = cos_ref[rows, :]
        sin = sin_ref[rows, :]
        for c in range(bn // LANES):
            t = acc[:, c * LANES:(c + 1) * LANES]
            partner = jnp.where(half, pltpu.roll(t, LANES - HEAD_DIM // 2, axis=1),
                                pltpu.roll(t, HEAD_DIM // 2, axis=1))
            o_ref[rows, c * LANES:(c + 1) * LANES] = (t * cos + partner * sin).astype(o_ref.dtype)


def _proj_rope(xb, w_all, positions, *, bm, bn, n, col0):
    s, k = xb.shape
    assert n % bn == 0 and col0 % bn == 0
    jb = col0 // bn
    half = jnp.arange(0, HEAD_DIM, 2, dtype=F32)
    inv_freq = ROPE_THETA ** (-half / HEAD_DIM)
    invf = jnp.tile(inv_freq, 2 * LANES // HEAD_DIM).reshape(1, LANES)
    return pl.pallas_call(
        functools.partial(_proj_rope_kernel, bm=bm, bn=bn),
        grid=(s // bm, n // bn),
        in_specs=[pl.BlockSpec((bm, 1), lambda i, j: (i, 0)),
                  pl.BlockSpec((1, LANES), lambda i, j: (0, 0)),
                  pl.BlockSpec((bm, k), lambda i, j: (i, 0)),
                  pl.BlockSpec((k, bn), lambda i, j: (0, jb + j))],
        out_specs=pl.BlockSpec((bm, bn), lambda i, j: (i, j)),
        out_shape=jax.ShapeDtypeStruct((s, n), BF16),
        scratch_shapes=[pltpu.VMEM((bm, LANES), F32), pltpu.VMEM((bm, LANES), F32)],
        compiler_params=_cparams(2),
        name="proj_qk_rope",
    )(positions.reshape(s, 1), invf, xb, w_all)


ATT_STRIP = 128


ATT_KB = 512


def _attn_kernel(lam_ref, gain_ref, q_ref, k_ref, vt_ref, o_ref, acc_ref, s0_ref, s1_ref, *, bq):
    i = pl.program_id(1)
    nq = 2 * bq
    kb = ATT_KB
    q = q_ref[...]
    lane = lax.broadcasted_iota(jnp.int32, q.shape, 1)
    zero = jnp.zeros_like(q)
    q_all = jnp.concatenate([jnp.where(lane < HEAD_DIM, q, zero),
                             jnp.where(lane >= HEAD_DIM, q, zero)], axis=0)
    s_slots = (s0_ref, s1_ref)
    all_cols = ((0, nq),)
    late_cols = ((kb, bq), (bq + kb, nq))

    def scores(j, slot):
        start = pl.multiple_of(j * kb, kb)
        s_slots[slot][...] = lax.dot_general(k_ref[pl.ds(start, kb), :], q_all,
                                             (((1,), (1,)), ((), ())), preferred_element_type=F32)

    def consume(j, slot, m, l, groups, key0):
        start = pl.multiple_of(j * kb, kb)
        vtb = vt_ref[:, pl.ds(start, kb)]
        m_parts = {c: m[:, c * ATT_STRIP:(c + 1) * ATT_STRIP] for c in range(nq // ATT_STRIP)}
        l_parts = {c: l[:, c * ATT_STRIP:(c + 1) * ATT_STRIP] for c in range(nq // ATT_STRIP)}
        for lo, hi in groups:
            p_strips, alphas = [], []
            for c in range(lo // ATT_STRIP, hi // ATT_STRIP):
                sl = slice(c * ATT_STRIP, (c + 1) * ATT_STRIP)
                s = s_slots[slot][:, sl]
                q_first = (c * ATT_STRIP) % bq
                if key0 is not None and key0 + kb - 1 > q_first:
                    key = lax.broadcasted_iota(jnp.int32, s.shape, 0) + key0
                    qry = lax.broadcasted_iota(jnp.int32, s.shape, 1) + q_first
                    s = jnp.where(key <= qry, s, -1e30)
                mn = jnp.maximum(m_parts[c], jnp.max(s, axis=0, keepdims=True))
                p = jnp.exp2(s - mn)
                alpha = jnp.exp2(m_parts[c] - mn)
                l_parts[c] = alpha * l_parts[c] + jnp.sum(p, axis=0, keepdims=True)
                m_parts[c] = mn
                alphas.append(alpha)
                p_strips.append(p.astype(BF16))
            p_all = jnp.concatenate(p_strips, axis=1)
            alpha = jnp.concatenate(alphas, axis=1)
            acc_ref[:, lo:hi] = alpha * acc_ref[:, lo:hi] + jnp.dot(vtb, p_all,
                                                                    preferred_element_type=F32)
        order = range(nq // ATT_STRIP)
        return (jnp.concatenate([m_parts[c] for c in order], axis=1),
                jnp.concatenate([l_parts[c] for c in order], axis=1))

    def pair(jj, carry):
        j = 2 * jj
        scores(j + 1, 1)
        m, l = consume(j, 0, carry[0], carry[1], all_cols, None)
        scores(j + 2, 0)
        return consume(j + 1, 1, m, l, all_cols, None)

    acc_ref[...] = jnp.zeros_like(acc_ref)
    scores(0, 0)
    init = (jnp.full((1, nq), -1e30, F32), jnp.zeros((1, nq), F32))
    m, l = lax.fori_loop(0, i, pair, init)
    scores(2 * i + 1, 1)
    m, l = consume(2 * i, 0, m, l, all_cols, 0)
    m, l = consume(2 * i + 1, 1, m, l, late_cols, kb)

    lam_v = lam_ref[...]
    lam = (jnp.exp(jnp.sum(lam_v[0:1] * lam_v[1:2], axis=1, keepdims=True))
           - jnp.exp(jnp.sum(lam_v[2:3] * lam_v[3:4], axis=1, keepdims=True)) + LAMBDA_INIT)
    o = acc_ref[...] * (1.0 / l)
    out = o[:, :bq] - lam * o[:, bq:]
    out = out * lax.rsqrt(jnp.mean(jnp.square(out), axis=0, keepdims=True) + SUBLN_EPS)
    out = out * gain_ref[...] * (1.0 - LAMBDA_INIT)
    o_ref[...] = out.T.astype(o_ref.dtype)


def _attention(qk, vt, lam_vecs, subln_gain, *, bq):
    s = qk.shape[0]
    assert s % bq == 0 and bq == 2 * ATT_KB
    return pl.pallas_call(
        functools.partial(_attn_kernel, bq=bq),
        grid=(DIFF_HEADS, s // bq),
        in_specs=[pl.BlockSpec((4, HEAD_DIM), lambda h, i: (0, 0)),
                  pl.BlockSpec((DIFF_V_DIM, 1), lambda h, i: (0, 0)),
                  pl.BlockSpec((bq, DIFF_V_DIM), lambda h, i: (i, h)),
                  pl.BlockSpec((s, DIFF_V_DIM), lambda h, i: (0, DIFF_HEADS + h)),
                  pl.BlockSpec((DIFF_V_DIM, s), lambda h, i: (h, 0))],
        out_specs=pl.BlockSpec((bq, DIFF_V_DIM), lambda h, i: (i, h)),
        out_shape=jax.ShapeDtypeStruct((s, DIFF_WIDTH), BF16),
        scratch_shapes=[pltpu.VMEM((DIFF_V_DIM, 2 * bq), F32), pltpu.VMEM((ATT_KB, 2 * bq), F32),
                        pltpu.VMEM((ATT_KB, 2 * bq), F32)],
        compiler_params=_cparams(2),
        name="diff_attn",
    )(lam_vecs, subln_gain.reshape(-1, 1).astype(F32), qk, qk, vt)


def _layer_norm(z, gain, bias):
    mean = jnp.mean(z, axis=-1, keepdims=True)
    zc = z - mean
    var = jnp.mean(zc * zc, axis=-1, keepdims=True)
    return zc * lax.rsqrt(var + LN_EPS) * gain + bias


LN_ROWS = 128


def _outproj_kernel(yr_ref, yd_ref, wr_ref, wd_ref, x_ref, gain_ref, bias_ref, o32_ref, o16_ref):
    for r in range(x_ref.shape[0] // LN_ROWS):
        rows = slice(r * LN_ROWS, (r + 1) * LN_ROWS)
        mix = (jnp.dot(yr_ref[rows, :], wr_ref[...], preferred_element_type=F32)
               + jnp.dot(yd_ref[rows, :], wd_ref[...], preferred_element_type=F32))
        y = _layer_norm(DEEPNORM_ALPHA * x_ref[rows, :] + mix, gain_ref[...], bias_ref[...])
        o32_ref[rows, :] = y
        o16_ref[rows, :] = y.astype(BF16)


def _outproj(y_rwkv, y_diff, w_r, w_d, x, gain, bias, *, bm):
    s = x.shape[0]
    const = lambda shape: pl.BlockSpec(shape, lambda i: (0, 0))
    return pl.pallas_call(
        _outproj_kernel,
        grid=(s // bm,),
        in_specs=[pl.BlockSpec((bm, RWKV_WIDTH), lambda i: (i, 0)),
                  pl.BlockSpec((bm, DIFF_WIDTH), lambda i: (i, 0)),
                  const((RWKV_WIDTH, D_MODEL)), const((DIFF_WIDTH, D_MODEL)),
                  pl.BlockSpec((bm, D_MODEL), lambda i: (i, 0)),
                  const((1, D_MODEL)), const((1, D_MODEL))],
        out_specs=[pl.BlockSpec((bm, D_MODEL), lambda i: (i, 0))] * 2,
        out_shape=[jax.ShapeDtypeStruct((s, D_MODEL), F32), jax.ShapeDtypeStruct((s, D_MODEL), BF16)],
        compiler_params=_cparams(1),
        name="outproj_ln",
    )(y_rwkv, y_diff, w_r, w_d, x, gain.reshape(1, -1), bias.reshape(1, -1))


def _ff2_kernel(h_ref, w_ref, x_ref, gain_ref, bias_ref, o_ref):
    kk = pl.program_id(1)
    part = lambda: jnp.dot(h_ref[...], w_ref[...], preferred_element_type=F32)

    @pl.when(kk == 0)
    def _():
        o_ref[...] = part()

    @pl.when((kk > 0) & (kk < pl.num_programs(1) - 1))
    def _():
        o_ref[...] += part()

    @pl.when(kk == pl.num_programs(1) - 1)
    def _():
        for r in range(o_ref.shape[0] // LN_ROWS):
            rows = slice(r * LN_ROWS, (r + 1) * LN_ROWS)
            acc = o_ref[rows, :] + jnp.dot(h_ref[rows, :], w_ref[...], preferred_element_type=F32)
            o_ref[rows, :] = _layer_norm(DEEPNORM_ALPHA * x_ref[rows, :] + acc,
                                         gain_ref[...], bias_ref[...])


def _ff2(h, w, x, gain, bias, *, bm, bk):
    s, kdim = h.shape
    assert s % bm == 0 and kdim % bk == 0 and kdim // bk >= 2
    const = lambda shape: pl.BlockSpec(shape, lambda i, k: (0, 0))
    return pl.pallas_call(
        _ff2_kernel,
        grid=(s // bm, kdim // bk),
        in_specs=[pl.BlockSpec((bm, bk), lambda i, k: (i, k)),
                  pl.BlockSpec((bk, D_MODEL), lambda i, k: (k, 0)),
                  pl.BlockSpec((bm, D_MODEL), lambda i, k: (i, 0)),
                  const((1, D_MODEL)), const((1, D_MODEL))],
        out_specs=pl.BlockSpec((bm, D_MODEL), lambda i, k: (i, 0)),
        out_shape=jax.ShapeDtypeStruct((s, D_MODEL), F32),
        compiler_params=_cparams(2),
        name="ff2_ln",
    )(h, w, x, gain.reshape(1, -1), bias.reshape(1, -1))


def _pad_rows(w, rows):
    return jnp.pad(w, ((0, rows - w.shape[0]), (0, 0)))


def _layer(x, positions, w_in, mu_shift, w0, w_up, a0, a_up, g_up, k_k, k_a, r_k, gn_gain, gn_bias,
           lq1, lk1, lq2, lk2, subln_gain, w_out, ln1_gain, ln1_bias, w_ff1, w_ff2, ln2_gain, ln2_bias,
           *, bm, prep_tb, nc, bq):
    w3 = 3 * RWKV_WIDTH
    rwkv_cols = w3 + DECAY_LORA + AAA_LORA + GATE_LORA
    qk0 = RWKV_PCOLS
    v0 = qk0 + 2 * DIFF_WIDTH

    def pad_cols(a, lo, hi, width):
        return jnp.pad(a[..., lo:hi], [(0, 0)] * (a.ndim - 1) + [(0, width - (hi - lo))])

    def rwkv_layout(a):
        return jnp.concatenate([
            a[..., :w3],
            pad_cols(a, w3, w3 + DECAY_LORA, DECAY_PAD),
            pad_cols(a, w3 + DECAY_LORA, w3 + DECAY_LORA + AAA_LORA, AAA_PAD),
            pad_cols(a, w3 + DECAY_LORA + AAA_LORA, rwkv_cols, GATE_PAD)], axis=-1)

    w_all = _relayout_w_in(w_in.T, bk=256)
    proj_r, xb = _proj_cast(x, w_all, bm=bm, bn=RWKV_PCOLS // 2, n=RWKV_PCOLS)
    vt = _matmul(xb, w_all, bm=bm, bn=512, n=DIFF_WIDTH, col0=v0, out_dtype=BF16,
                 transpose_out=True, name="proj_vt")

    at, bt, kt, rt, v_r, wc, bonus, gate = _rwkv_prep(
        proj_r, rwkv_layout(mu_shift), w0,
        _pad_rows(w_up, DECAY_PAD).astype(BF16), a0, _pad_rows(a_up, AAA_PAD).astype(BF16),
        _pad_rows(g_up, GATE_PAD).astype(BF16), k_k, k_a, r_k.reshape(-1), tb=prep_tb)
    y_rwkv = _rwkv_chunks(at, bt, kt, rt, v_r, wc, bonus, gate, gn_gain, gn_bias, nc=nc)

    qk = _proj_rope(xb, w_all, positions, bm=bm, bn=512, n=2 * DIFF_WIDTH, col0=qk0)
    lam_vecs = jnp.stack([lq1, lk1, lq2, lk2]).astype(F32)
    y_diff = _attention(qk, vt, lam_vecs, subln_gain, bq=bq)

    wo = w_out.astype(BF16)
    x1, x1b = _outproj(y_rwkv, y_diff, wo[:RWKV_WIDTH], wo[RWKV_WIDTH:], x, ln1_gain, ln1_bias, bm=512)
    h, w_ff2b = _ff1(x1b, w_ff1, w_ff2, bm=bm, bn=1024)
    return _ff2(h, w_ff2b, x1, ln2_gain, ln2_bias, bm=1024, bk=1024)


def kernel(x, positions, w_in, mu_shift, rwkv_w0, rwkv_w_up, rwkv_a0, rwkv_a_up, rwkv_g_up, rwkv_k_k,
           rwkv_k_a, rwkv_r_k, rwkv_gn_gain, rwkv_gn_bias, diff_lambda_q1, diff_lambda_k1,
           diff_lambda_q2, diff_lambda_k2, diff_subln_gain, w_out, ln1_gain, ln1_bias, w_ff1, w_ff2,
           ln2_gain, ln2_bias):
    batch, seq, _ = x.shape
    assert batch == 1 and w_in.shape[0] == 1
    out = _layer(
        x[0], positions[0], w_in[0], mu_shift[0], rwkv_w0[0], rwkv_w_up[0], rwkv_a0[0], rwkv_a_up[0],
        rwkv_g_up[0], rwkv_k_k[0], rwkv_k_a[0], rwkv_r_k[0], rwkv_gn_gain[0], rwkv_gn_bias[0],
        diff_lambda_q1[0], diff_lambda_k1[0], diff_lambda_q2[0], diff_lambda_k2[0],
        diff_subln_gain[0], w_out[0], ln1_gain[0], ln1_bias[0], w_ff1[0], w_ff2[0], ln2_gain[0],
        ln2_bias[0], bm=1024, prep_tb=256, nc=4, bq=1024)
    return out[None]
```

```python
import functools
import math

import jax
import jax.numpy as jnp
from jax import lax
from jax.experimental import pallas as pl
from jax.experimental.pallas import tpu as pltpu

F32 = jnp.float32
BF16 = jnp.bfloat16

D_MODEL = 2048
RWKV_WIDTH = 1024
HEAD_DIM = 64
DECAY_LORA = 64
AAA_LORA = 64
GATE_LORA = 160
DIFF_WIDTH = 1024
DIFF_V_DIM = 128
DIFF_HEADS = 8
D_FF = 8192
ROPE_THETA = 10000.0
LN_EPS = 1e-5
RWKV_GN_EPS = 64e-5
SUBLN_EPS = 1e-5
DEEPNORM_ALPHA = 2.0 ** 0.25
LAMBDA_INIT = 0.8 - 0.6 * math.exp(0.0)
LOG2_E = math.log2(math.e)

LANES = 128
SUBLANES = 8
VMEM_LIMIT_BYTES = 56 * 1024 * 1024

DECAY_PAD = 128
AAA_PAD = 128
GATE_PAD = 256
RWKV_PCOLS = 3 * RWKV_WIDTH + DECAY_PAD + AAA_PAD + GATE_PAD
QK_COL0 = 4096
V_COL0 = QK_COL0 + 2 * DIFF_WIDTH
CHUNK = 64
GROUP_HEADS = 4
GROUP_W = GROUP_HEADS * HEAD_DIM


def _cparams(n_axes):
    return pltpu.CompilerParams(
        dimension_semantics=("arbitrary",) * n_axes,
        vmem_limit_bytes=VMEM_LIMIT_BYTES,
    )


def _dot(a, b):
    return jnp.dot(a.astype(BF16), b.astype(BF16), preferred_element_type=F32)


def _split3(x):
    h1 = x.astype(BF16)
    r1 = x - h1.astype(F32)
    h2 = r1.astype(BF16)
    h3 = (r1 - h2.astype(F32)).astype(BF16)
    return h1, h2, h3


def _split2(x):
    h1 = x.astype(BF16)
    h2 = (x - h1.astype(F32)).astype(BF16)
    return h1, h2


def _mm_kernel(a_ref, b_ref, o_ref, *, transpose_out):
    acc = jnp.dot(a_ref[...], b_ref[...], preferred_element_type=F32)
    if transpose_out:
        acc = acc.T
    o_ref[...] = acc.astype(o_ref.dtype)


def _matmul(a, b, *, bm, bn, n, col0=0, out_dtype, transpose_out=False, name):
    m, k = a.shape
    assert m % bm == 0 and n % bn == 0 and col0 % bn == 0
    jb = col0 // bn
    if transpose_out:
        out_spec = pl.BlockSpec((bn, bm), lambda i, j: (j, i))
        out_shape = jax.ShapeDtypeStruct((n, m), out_dtype)
    else:
        out_spec = pl.BlockSpec((bm, bn), lambda i, j: (i, j))
        out_shape = jax.ShapeDtypeStruct((m, n), out_dtype)
    return pl.pallas_call(
        functools.partial(_mm_kernel, transpose_out=transpose_out),
        grid=(m // bm, n // bn),
        in_specs=[pl.BlockSpec((bm, k), lambda i, j: (i, 0)),
                  pl.BlockSpec((k, bn), lambda i, j: (0, jb + j))],
        out_specs=out_spec,
        out_shape=out_shape,
        compiler_params=_cparams(2),
        name=name,
    )(a, b)


def _proj_cast_kernel(x_ref, w_ref, o_ref, xb_ref):
    @pl.when(pl.program_id(1) == 0)
    def _():
        xb_ref[...] = x_ref[...].astype(BF16)

    o_ref[...] = jnp.dot(xb_ref[...], w_ref[...], preferred_element_type=F32)


def _proj_cast(x, w_all, *, bm, bn, n):
    m, k = x.shape
    assert m % bm == 0 and n % bn == 0
    return pl.pallas_call(
        _proj_cast_kernel,
        grid=(m // bm, n // bn),
        in_specs=[pl.BlockSpec((bm, k), lambda i, j: (i, 0)),
                  pl.BlockSpec((k, bn), lambda i, j: (0, j))],
        out_specs=[pl.BlockSpec((bm, bn), lambda i, j: (i, j)),
                   pl.BlockSpec((bm, k), lambda i, j: (i, 0))],
        out_shape=[jax.ShapeDtypeStruct((m, n), F32), jax.ShapeDtypeStruct((m, k), BF16)],
        compiler_params=_cparams(2),
        name="proj_rwkv",
    )(x, w_all)


def _ff1_kernel(x_ref, w_ref, w2_ref, o_ref, w2b_ref, wb_ref):
    @pl.when(pl.program_id(1) == 0)
    def _():
        wb_ref[...] = w_ref[...].astype(BF16)

    acc = jnp.dot(x_ref[...], wb_ref[...], preferred_element_type=F32)
    o_ref[...] = jnp.square(jnp.maximum(acc, 0.0)).astype(o_ref.dtype)
    w2b_ref[...] = w2_ref[...].astype(BF16)


def _ff1(x, w, w2, *, bm, bn):
    m, k = x.shape
    _, n = w.shape
    assert m % bm == 0 and n % bn == 0
    steps = (n // bn) * (m // bm)
    rows2, cols2 = w2.shape
    assert rows2 % steps == 0 and (rows2 // steps) % (2 * SUBLANES) == 0
    slab = pl.BlockSpec((rows2 // steps, cols2), lambda j, i: (j * (m // bm) + i, 0))
    return pl.pallas_call(
        _ff1_kernel,
        grid=(n // bn, m // bm),
        in_specs=[pl.BlockSpec((bm, k), lambda j, i: (i, 0)),
                  pl.BlockSpec((k, bn), lambda j, i: (0, j)),
                  slab],
        out_specs=[pl.BlockSpec((bm, bn), lambda j, i: (i, j)), slab],
        out_shape=[jax.ShapeDtypeStruct((m, n), BF16), jax.ShapeDtypeStruct(w2.shape, BF16)],
        scratch_shapes=[pltpu.VMEM((k, bn), BF16)],
        compiler_params=_cparams(2),
        name="ff1",
    )(x, w, w2)


def _relayout_kernel(wt_ref, o_ref):
    cols = wt_ref.shape[1]
    w3 = 3 * RWKV_WIDTH
    a0 = w3 + DECAY_LORA
    g0 = a0 + AAA_LORA
    q0 = g0 + GATE_LORA
    zeros = lambda n: jnp.zeros((n, cols), F32)
    pieces = [wt_ref[:a0, :], zeros(DECAY_PAD - DECAY_LORA),
              wt_ref[a0:g0, :], zeros(AAA_PAD - AAA_LORA),
              wt_ref[g0:q0, :], zeros(GATE_PAD - GATE_LORA + QK_COL0 - RWKV_PCOLS),
              wt_ref[q0:q0 + DIFF_WIDTH, :] * (HEAD_DIM ** -0.5 * LOG2_E),
              wt_ref[q0 + DIFF_WIDTH:, :]]
    for c0 in range(0, o_ref.shape[1], RELAYOUT_COLS):
        rows = slice(c0, c0 + RELAYOUT_COLS)
        tile = jnp.concatenate(_row_window(pieces, c0, c0 + RELAYOUT_COLS), axis=0)
        o_ref[:, rows] = tile.T.astype(o_ref.dtype)


def _row_window(pieces, lo, hi):
    out, start = [], 0
    for p in pieces:
        stop = start + p.shape[0]
        a, b = max(lo, start), min(hi, stop)
        if a < b:
            out.append(p[a - start:b - start])
        start = stop
    return out


RELAYOUT_COLS = 512


def _relayout_w_in(w_in_t, *, bk):
    n, k = w_in_t.shape
    n_out = V_COL0 + DIFF_WIDTH
    assert n == 3 * RWKV_WIDTH + DECAY_LORA + AAA_LORA + GATE_LORA + 3 * DIFF_WIDTH
    assert n_out % RELAYOUT_COLS == 0
    return pl.pallas_call(
        _relayout_kernel,
        grid=(k // bk,),
        in_specs=[pl.BlockSpec((n, bk), lambda i: (0, i))],
        out_specs=pl.BlockSpec((bk, n_out), lambda i: (i, 0)),
        out_shape=jax.ShapeDtypeStruct((k, n_out), BF16),
        compiler_params=_cparams(1),
        name="relayout_w_in",
    )(w_in_t)


def _rwkv_prep_kernel(p_ref, mu_ref, w0_ref, wup_ref, a0_ref, aup_ref, gup_ref, kk_ref, ka_ref,
                      rk_ref, segin_ref, segout_ref, tri_ref,
                      at_ref, bt_ref, kt_ref, rt_ref, v_ref, wc_ref, bonus_ref, g_ref,
                      carry_ref, *, tb):
    i = pl.program_id(0)

    @pl.when(i == 0)
    def _():
        carry_ref[...] = jnp.zeros_like(carry_ref)

    def mixed(lo, hi):
        p = p_ref[:, lo:hi]
        prev_last = carry_ref[0:1, lo:hi]
        shifted = pltpu.roll(p, 1, axis=0)
        row = lax.broadcasted_iota(jnp.int32, p.shape, 0)
        prev = jnp.where(row == 0, prev_last, shifted)
        carry_ref[0:1, lo:hi] = p[tb - 1:tb, :]
        return p + (prev - p) * mu_ref[:, lo:hi]

    w3 = 3 * RWKV_WIDTH
    r = mixed(0, RWKV_WIDTH)
    k = mixed(RWKV_WIDTH, 2 * RWKV_WIDTH)
    v = mixed(2 * RWKV_WIDTH, w3)
    w_d = mixed(w3, w3 + DECAY_PAD)
    a_d = mixed(w3 + DECAY_PAD, w3 + DECAY_PAD + AAA_PAD)
    g_d = mixed(w3 + DECAY_PAD + AAA_PAD, RWKV_PCOLS)

    z = w0_ref[...] + _dot(jnp.tanh(w_d), wup_ref[...])
    softplus = jnp.maximum(-z, 0.0) + jnp.log(1.0 + jnp.exp(-jnp.abs(z)))
    lw = -jnp.exp(-softplus - 0.5)
    a = jax.nn.sigmoid(a0_ref[...] + _dot(a_d, aup_ref[...]))
    g_ref[...] = _dot(jax.nn.sigmoid(g_d), gup_ref[...])

    def segsum(x):
        h1, h2 = _split2(jnp.dot(x.astype(BF16), segin_ref[...], preferred_element_type=F32))
        return (jnp.dot(h1, segout_ref[...], preferred_element_type=F32)
                + jnp.dot(h2, segout_ref[...], preferred_element_type=F32))

    kk0 = k * kk_ref[...]
    kk = kk0 * lax.rsqrt(jnp.maximum(segsum(kk0 * kk0), 1e-24))
    k2 = k * (1.0 + (a - 1.0) * ka_ref[...])
    bonus_ref[...] = segsum(r * k2 * rk_ref[...]) * v

    tri = tri_ref[...]
    l1, l2, l3 = _split3(lw)
    cs = (jnp.dot(tri, l1, preferred_element_type=F32)
          + jnp.dot(tri, l2, preferred_element_type=F32)
          + jnp.dot(tri, l3, preferred_element_type=F32))
    cl = cs[:tb]
    tot = cs[tb:]
    e_neg = jnp.exp(-cl)
    at_ref[...] = (-kk * jnp.exp(cl - lw)).astype(BF16)
    bt_ref[...] = (kk * a * e_neg).astype(BF16)
    kt_ref[...] = (k2 * e_neg).astype(BF16)
    rt_ref[...] = (r * jnp.exp(cl)).astype(BF16)
    v_ref[...] = v.astype(BF16)
    for c in range(tb // CHUNK):
        wc_ref[c] = jnp.exp(tot[c * CHUNK:c * CHUNK + 1, :])


def _rwkv_prep(proj_r, mu, w0, wup, a0, aup, gup, k_k, k_a, r_k, *, tb):
    s = proj_r.shape[0]
    assert s % tb == 0 and tb % CHUNK == 0
    w = RWKV_WIDTH
    lane = jnp.arange(w)
    seg_in = (lane[:, None] // HEAD_DIM == jnp.arange(LANES)[None, :]).astype(BF16)
    t = jnp.arange(tb)
    same = t[:, None] // CHUNK == t[None, :] // CHUNK
    tri = jnp.concatenate([(same & (t[:, None] >= t[None, :])), same], axis=0).astype(BF16)
    row = lambda x: x.reshape(1, -1).astype(F32)
    const = lambda shape: pl.BlockSpec(shape, lambda i: (0,) * len(shape))
    tile = lambda width: pl.BlockSpec((tb, width), lambda i: (i, 0))
    outs = pl.pallas_call(
        functools.partial(_rwkv_prep_kernel, tb=tb),
        grid=(s // tb,),
        in_specs=[tile(RWKV_PCOLS), const((1, RWKV_PCOLS)), const((1, w)), const((DECAY_PAD, w)),
                  const((1, w)), const((AAA_PAD, w)), const((GATE_PAD, w)), const((1, w)),
                  const((1, w)), const((1, w)), const((w, LANES)), const((LANES, w)),
                  const((2 * tb, tb))],
        out_specs=[tile(w), tile(w), tile(w), tile(w), tile(w),
                   pl.BlockSpec((tb // CHUNK, 1, w), lambda i: (i, 0, 0)),
                   tile(w), tile(w)],
        out_shape=[jax.ShapeDtypeStruct((s, w), BF16)] * 5
        + [jax.ShapeDtypeStruct((s // CHUNK, 1, w), F32),
           jax.ShapeDtypeStruct((s, w), F32), jax.ShapeDtypeStruct((s, w), F32)],
        scratch_shapes=[pltpu.VMEM((SUBLANES, RWKV_PCOLS), F32)],
        compiler_params=_cparams(1),
        name="rwkv_prep",
    )(proj_r, row(mu), row(w0), wup, row(a0), aup, gup, row(k_k), row(k_a), row(r_k), seg_in, seg_in.T, tri)
    return outs


def _bdot(a, b):
    return jnp.dot(a, b, preferred_element_type=F32)


def _bdot_nt(a, b):
    return lax.dot_general(a, b, (((1,), (1,)), ((), ())), preferred_element_type=F32)


(MASK_STRICT, MASK_INCL, MASK_EYE, MASK_BLK8, MASK_OFF16, MASK_OFF32, MASK_OFF64) = range(7)


def _chunk_masks():
    a = jnp.arange(CHUNK)[:, None]
    b = jnp.arange(GROUP_W)[None, :] % CHUNK
    blk = lambda size: (a // size) == (b // size)
    masks = [a > b, a >= b, a == b, blk(8),
             blk(16) & ~blk(8), blk(32) & ~blk(16), blk(64) & ~blk(32)]
    return jnp.stack(masks).astype(F32)


def _rwkv_chunk_kernel(at_ref, bt_ref, kt_ref, rt_ref, v_ref, wc_ref, bonus_ref, g_ref,
                       gain_ref, bias_ref, mask_ref, head_ref, o_ref, state_ref, *, nc):
    @pl.when(pl.program_id(0) == 0)
    def _():
        state_ref[...] = jnp.zeros_like(state_ref)

    n = GROUP_W
    ng = RWKV_WIDTH // GROUP_W
    items = [(c, g) for c in range(nc) for g in range(ng)]
    lane_head = lax.broadcasted_iota(jnp.int32, (CHUNK, n), 1) >> 6

    def tile(ref, c, g):
        return ref[c * CHUNK:(c + 1) * CHUNK, g * n:(g + 1) * n]

    bf = lambda x: x.astype(BF16)

    def expand(x):
        x = bf(x)
        zero = jnp.zeros_like(x)
        return jnp.concatenate([jnp.where(lane_head == h, x, zero) for h in range(GROUP_HEADS)],
                               axis=0)

    def collapse(x):
        return (x[0:CHUNK] + x[CHUNK:2 * CHUNK] + x[2 * CHUNK:3 * CHUNK] + x[3 * CHUNK:4 * CHUNK])

    mask = lambda k: mask_ref[k]
    each = lambda f: [f(*it) for it in items]
    zipped = lambda f, *lists: [f(*args) for args in zip(*lists)]
    wdot = lambda w, e: _bdot(bf(w), e)

    at = each(lambda c, g: tile(at_ref, c, g))
    bt = each(lambda c, g: tile(bt_ref, c, g))
    kt = each(lambda c, g: tile(kt_ref, c, g))
    rt = each(lambda c, g: tile(rt_ref, c, g))
    wc = each(lambda c, g: wc_ref[c][:, g * n:(g + 1) * n])
    ae = [expand(x) for x in at]
    be = [expand(x) for x in bt]
    ke = [expand(x) for x in kt]
    ve = each(lambda c, g: expand(tile(v_ref, c, g)))
    bhe = zipped(lambda x, w: expand(x.astype(F32) * w), bt, wc)
    khe = zipped(lambda x, w: expand(x.astype(F32) * w), kt, wc)

    sc = zipped(lambda a, r, b, k: _bdot_nt(jnp.concatenate([a, r], axis=0),
                                            jnp.concatenate([b, k], axis=0)), at, rt, be, ke)
    lab = [x[:CHUNK, :n] * mask(MASK_STRICT) for x in sc]
    lak = [bf(x[:CHUNK, n:] * mask(MASK_STRICT)) for x in sc]
    arb = [bf(x[CHUNK:, :n] * mask(MASK_INCL)) for x in sc]
    ark = [bf(x[CHUNK:, n:] * mask(MASK_INCL)) for x in sc]
    bht = [collapse(x.T) for x in bhe]
    kht = [collapse(x.T) for x in khe]
    xv = zipped(lambda a, b, c, v: _bdot(jnp.concatenate([a, b, c], axis=0), v), lak, ark, kht, ve)
    lv = [x[:CHUNK] for x in xv]
    ark_v = [x[CHUNK:2 * CHUNK] for x in xv]
    kht_v = [x[2 * CHUNK:] for x in xv]

    pw = [x * mask(MASK_BLK8) for x in lab]
    minv = [mask(MASK_EYE) + x for x in pw]
    for _ in range(2):
        pw = zipped(lambda x: wdot(x, expand(x)), pw)
        minv = zipped(lambda x, m: m + wdot(x, expand(m)), pw, minv)
    for k in (MASK_OFF16, MASK_OFF32, MASK_OFF64):
        off = [expand(x * mask(k)) for x in lab]
        tmp = zipped(wdot, minv, off)
        minv = zipped(lambda m, t: m + wdot(t, expand(m)), minv, tmp)
    mb = [bf(m) for m in minv]

    p = [expand(_bdot(m, a)) for m, a in zip(mb, ae)]
    q = [expand(_bdot(m, expand(x))) for m, x in zip(mb, lv)]
    ab = zipped(lambda a, b: jnp.concatenate([a, b], axis=0), arb, bht)
    xp = zipped(_bdot, ab, p)
    xq = zipped(_bdot, ab, q)
    g_mat = zipped(lambda r, x: r.astype(F32) + x[:CHUNK], rt, xp)
    phi = zipped(lambda w, x: mask(MASK_EYE) * w + x[CHUNK:], wc, xp)
    gp = zipped(lambda g, f: bf(jnp.concatenate([g, f], axis=0)), g_mat, phi)
    y0 = zipped(lambda x, v: x[:CHUNK] + v, xq, ark_v)
    psi = zipped(lambda x, v: x[CHUNK:] + v, xq, kht_v)

    head = head_ref[...]

    def head_mean(x):
        h1, h2 = _split2(x)
        return (_bdot(h1, head) + _bdot(h2, head)) * (1.0 / HEAD_DIM)

    state = [state_ref[g] for g in range(ng)]
    for c in range(nc):
        idx = [c * ng + g for g in range(ng)]
        se = [expand(s) for s in state]
        xs = [_bdot(gp[i], se[g]) for g, i in enumerate(idx)]
        ys = [x[:CHUNK] + y0[i] for x, i in zip(xs, idx)]
        state = [x[CHUNK:] + psi[i] for x, i in zip(xs, idx)]
        for g, y in enumerate(ys):
            yc = y - head_mean(y)
            yn = yc * lax.rsqrt(head_mean(yc * yc) + RWKV_GN_EPS)
            rows = slice(c * CHUNK, (c + 1) * CHUNK)
            cols = slice(g * n, (g + 1) * n)
            out = (yn * gain_ref[:, cols] + bias_ref[:, cols] + bonus_ref[rows, cols]) * g_ref[rows, cols]
            o_ref[rows, cols] = out.astype(o_ref.dtype)
    for g in range(ng):
        state_ref[g] = state[g]


def _rwkv_chunks(at, bt, kt, rt, v, wc, bonus, g, gn_gain, gn_bias, *, nc):
    s, w = at.shape
    tb = nc * CHUNK
    assert s % tb == 0 and w % GROUP_W == 0
    tile = pl.BlockSpec((tb, w), lambda ti: (ti, 0))
    rowspec = pl.BlockSpec((1, w), lambda ti: (0, 0))
    masks = _chunk_masks()
    lane = jnp.arange(GROUP_W)
    head = (lane[:, None] // HEAD_DIM == lane[None, :] // HEAD_DIM).astype(BF16)
    return pl.pallas_call(
        functools.partial(_rwkv_chunk_kernel, nc=nc),
        grid=(s // tb,),
        in_specs=[tile, tile, tile, tile, tile,
                  pl.BlockSpec((nc, 1, w), lambda ti: (ti, 0, 0)),
                  tile, tile, rowspec, rowspec,
                  pl.BlockSpec(masks.shape, lambda ti: (0, 0, 0)),
                  pl.BlockSpec(head.shape, lambda ti: (0, 0))],
        out_specs=tile,
        out_shape=jax.ShapeDtypeStruct((s, w), BF16),
        scratch_shapes=[pltpu.VMEM((w // GROUP_W, CHUNK, GROUP_W), F32)],
        compiler_params=_cparams(1),
        name="rwkv_chunks",
    )(at, bt, kt, rt, v, wc, bonus, g, gn_gain.reshape(1, -1).astype(F32),
      gn_bias.reshape(1, -1).astype(F32), masks, head)


ROPE_ROWS = 256


def _proj_rope_kernel(pos_ref, invf_ref, x_ref, w_ref, o_ref, cos_ref, sin_ref, *, bm, bn):
    def first_half(rows):
        lane = lax.broadcasted_iota(jnp.int32, (rows, LANES), 1)
        return (lane & (HEAD_DIM - 1)) < (HEAD_DIM // 2)

    @pl.when(pl.program_id(1) == 0)
    def _():
        ang = pos_ref[...].astype(F32) * invf_ref[...]
        cos_ref[...] = jnp.cos(ang)
        sin = jnp.sin(ang)
        sin_ref[...] = jnp.where(first_half(bm), -sin, sin)

    half = first_half(ROPE_ROWS)

    for r in range(bm // ROPE_ROWS):
        rows = slice(r * ROPE_ROWS, (r + 1) * ROPE_ROWS)
        acc = jnp.dot(x_ref[rows, :], w_ref[...], preferred_element_type=F32)
        cos = cos_ref[rows, :]
        sin = sin_ref[rows, :]
        for c in range(bn // LANES):
            t = acc[:, c * LANES:(c + 1) * LANES]
            partner = jnp.where(half, pltpu.roll(t, LANES - HEAD_DIM // 2, axis=1),
                                pltpu.roll(t, HEAD_DIM // 2, axis=1))
            o_ref[rows, c * LANES:(c + 1) * LANES] = (t * cos + partner * sin).astype(o_ref.dtype)


def _proj_rope(xb, w_all, positions, *, bm, bn, n, col0):
    s, k = xb.shape
    assert n % bn == 0 and col0 % bn == 0
    jb = col0 // bn
    half = jnp.arange(0, HEAD_DIM, 2, dtype=F32)
    inv_freq = ROPE_THETA ** (-half / HEAD_DIM)
    invf = jnp.tile(inv_freq, 2 * LANES // HEAD_DIM).reshape(1, LANES)
    return pl.pallas_call(
        functools.partial(_proj_rope_kernel, bm=bm, bn=bn),
        grid=(s // bm, n // bn),
        in_specs=[pl.BlockSpec((bm, 1), lambda i, j: (i, 0)),
                  pl.BlockSpec((1, LANES), lambda i, j: (0, 0)),
                  pl.BlockSpec((bm, k), lambda i, j: (i, 0)),
                  pl.BlockSpec((k, bn), lambda i, j: (0, jb + j))],
        out_specs=pl.BlockSpec((bm, bn), lambda i, j: (i, j)),
        out_shape=jax.ShapeDtypeStruct((s, n), BF16),
        scratch_shapes=[pltpu.VMEM((bm, LANES), F32), pltpu.VMEM((bm, LANES), F32)],
        compiler_params=_cparams(2),
        name="proj_qk_rope",
    )(positions.reshape(s, 1), invf, xb, w_all)


ATT_STRIP = 512


ATT_KB = 512


def _attn_kernel(lam_ref, gain_ref, q_ref, k_ref, vt_ref, o_ref, acc_ref, s0_ref, s1_ref, *, bq):
    i = pl.program_id(1)
    nq = 2 * bq
    kb = ATT_KB
    q = q_ref[...]
    lane = lax.broadcasted_iota(jnp.int32, q.shape, 1)
    zero = jnp.zeros_like(q)
    q_all = jnp.concatenate([jnp.where(lane < HEAD_DIM, q, zero),
                             jnp.where(lane >= HEAD_DIM, q, zero)], axis=0)
    s_slots = (s0_ref, s1_ref)
    all_cols = ((0, nq),)
    late_cols = ((kb, bq), (bq + kb, nq))

    def scores(j, slot):
        start = pl.multiple_of(j * kb, kb)
        s_slots[slot][...] = lax.dot_general(k_ref[pl.ds(start, kb), :], q_all,
                                             (((1,), (1,)), ((), ())), preferred_element_type=F32)

    def consume(j, slot, m, l, groups, key0):
        start = pl.multiple_of(j * kb, kb)
        vtb = vt_ref[:, pl.ds(start, kb)]
        m_parts = {c: m[:, c * ATT_STRIP:(c + 1) * ATT_STRIP] for c in range(nq // ATT_STRIP)}
        l_parts = {c: l[:, c * ATT_STRIP:(c + 1) * ATT_STRIP] for c in range(nq // ATT_STRIP)}
        for lo, hi in groups:
            p_strips, alphas = [], []
            for c in range(lo // ATT_STRIP, hi // ATT_STRIP):
                sl = slice(c * ATT_STRIP, (c + 1) * ATT_STRIP)
                s = s_slots[slot][:, sl]
                q_first = (c * ATT_STRIP) % bq
                if key0 is not None and key0 + kb - 1 > q_first:
                    key = lax.broadcasted_iota(jnp.int32, s.shape, 0) + key0
                    qry = lax.broadcasted_iota(jnp.int32, s.shape, 1) + q_first
                    s = jnp.where(key <= qry, s, -1e30)
                mn = jnp.maximum(m_parts[c], jnp.max(s, axis=0, keepdims=True))
                p = jnp.exp2(s - mn)
                alpha = jnp.exp2(m_parts[c] - mn)
                l_parts[c] = alpha * l_parts[c] + jnp.sum(p, axis=0, keepdims=True)
                m_parts[c] = mn
                alphas.append(alpha)
                p_strips.append(p.astype(BF16))
            p_all = jnp.concatenate(p_strips, axis=1)
            alpha = jnp.concatenate(alphas, axis=1)
            acc_ref[:, lo:hi] = alpha * acc_ref[:, lo:hi] + jnp.dot(vtb, p_all,
                                                                    preferred_element_type=F32)
        order = range(nq // ATT_STRIP)
        return (jnp.concatenate([m_parts[c] for c in order], axis=1),
                jnp.concatenate([l_parts[c] for c in order], axis=1))

    def pair(jj, carry):
        j = 2 * jj
        scores(j + 1, 1)
        m, l = consume(j, 0, carry[0], carry[1], all_cols, None)
        scores(j + 2, 0)
        return consume(j + 1, 1, m, l, all_cols, None)

    acc_ref[...] = jnp.zeros_like(acc_ref)
    scores(0, 0)
    init = (jnp.full((1, nq), -1e30, F32), jnp.zeros((1, nq), F32))
    m, l = lax.fori_loop(0, i, pair, init)
    scores(2 * i + 1, 1)
    m, l = consume(2 * i, 0, m, l, all_cols, 0)
    m, l = consume(2 * i + 1, 1, m, l, late_cols, kb)

    lam_v = lam_ref[...]
    lam = (jnp.exp(jnp.sum(lam_v[0:1] * lam_v[1:2], axis=1, keepdims=True))
           - jnp.exp(jnp.sum(lam_v[2:3] * lam_v[3:4], axis=1, keepdims=True)) + LAMBDA_INIT)
    o = acc_ref[...] * (1.0 / l)
    out = o[:, :bq] - lam * o[:, bq:]
    out = out * lax.rsqrt(jnp.mean(jnp.square(out), axis=0, keepdims=True) + SUBLN_EPS)
    out = out * gain_ref[...] * (1.0 - LAMBDA_INIT)
    o_ref[...] = out.T.astype(o_ref.dtype)


def _attention(qk, vt, lam_vecs, subln_gain, *, bq):
    s = qk.shape[0]
    assert s % bq == 0 and bq == 2 * ATT_KB
    return pl.pallas_call(
        functools.partial(_attn_kernel, bq=bq),
        grid=(DIFF_HEADS, s // bq),
        in_specs=[pl.BlockSpec((4, HEAD_DIM), lambda h, i: (0, 0)),
                  pl.BlockSpec((DIFF_V_DIM, 1), lambda h, i: (0, 0)),
                  pl.BlockSpec((bq, DIFF_V_DIM), lambda h, i: (i, h)),
                  pl.BlockSpec((s, DIFF_V_DIM), lambda h, i: (0, DIFF_HEADS + h)),
                  pl.BlockSpec((DIFF_V_DIM, s), lambda h, i: (h, 0))],
        out_specs=pl.BlockSpec((bq, DIFF_V_DIM), lambda h, i: (i, h)),
        out_shape=jax.ShapeDtypeStruct((s, DIFF_WIDTH), BF16),
        scratch_shapes=[pltpu.VMEM((DIFF_V_DIM, 2 * bq), F32), pltpu.VMEM((ATT_KB, 2 * bq), F32),
                        pltpu.VMEM((ATT_KB, 2 * bq), F32)],
        compiler_params=_cparams(2),
        name="diff_attn",
    )(lam_vecs, subln_gain.reshape(-1, 1).astype(F32), qk, qk, vt)


def _layer_norm(z, gain, bias):
    mean = jnp.mean(z, axis=-1, keepdims=True)
    zc = z - mean
    var = jnp.mean(zc * zc, axis=-1, keepdims=True)
    return zc * lax.rsqrt(var + LN_EPS) * gain + bias


LN_ROWS = 128


def _outproj_kernel(yr_ref, yd_ref, wr_ref, wd_ref, x_ref, gain_ref, bias_ref, o32_ref, o16_ref):
    for r in range(x_ref.shape[0] // LN_ROWS):
        rows = slice(r * LN_ROWS, (r + 1) * LN_ROWS)
        mix = (jnp.dot(yr_ref[rows, :], wr_ref[...], preferred_element_type=F32)
               + jnp.dot(yd_ref[rows, :], wd_ref[...], preferred_element_type=F32))
        y = _layer_norm(DEEPNORM_ALPHA * x_ref[rows, :] + mix, gain_ref[...], bias_ref[...])
        o32_ref[rows, :] = y
        o16_ref[rows, :] = y.astype(BF16)


def _outproj(y_rwkv, y_diff, w_r, w_d, x, gain, bias, *, bm):
    s = x.shape[0]
    const = lambda shape: pl.BlockSpec(shape, lambda i: (0, 0))
    return pl.pallas_call(
        _outproj_kernel,
        grid=(s // bm,),
        in_specs=[pl.BlockSpec((bm, RWKV_WIDTH), lambda i: (i, 0)),
                  pl.BlockSpec((bm, DIFF_WIDTH), lambda i: (i, 0)),
                  const((RWKV_WIDTH, D_MODEL)), const((DIFF_WIDTH, D_MODEL)),
                  pl.BlockSpec((bm, D_MODEL), lambda i: (i, 0)),
                  const((1, D_MODEL)), const((1, D_MODEL))],
        out_specs=[pl.BlockSpec((bm, D_MODEL), lambda i: (i, 0))] * 2,
        out_shape=[jax.ShapeDtypeStruct((s, D_MODEL), F32), jax.ShapeDtypeStruct((s, D_MODEL), BF16)],
        compiler_params=_cparams(1),
        name="outproj_ln",
    )(y_rwkv, y_diff, w_r, w_d, x, gain.reshape(1, -1), bias.reshape(1, -1))


def _ff2_kernel(h_ref, w_ref, x_ref, gain_ref, bias_ref, o_ref):
    kk = pl.program_id(1)
    part = lambda: jnp.dot(h_ref[...], w_ref[...], preferred_element_type=F32)

    @pl.when(kk == 0)
    def _():
        o_ref[...] = part()

    @pl.when((kk > 0) & (kk < pl.num_programs(1) - 1))
    def _():
        o_ref[...] += part()

    @pl.when(kk == pl.num_programs(1) - 1)
    def _():
        for r in range(o_ref.shape[0] // LN_ROWS):
            rows = slice(r * LN_ROWS, (r + 1) * LN_ROWS)
            acc = o_ref[rows, :] + jnp.dot(h_ref[rows, :], w_ref[...], preferred_element_type=F32)
            o_ref[rows, :] = _layer_norm(DEEPNORM_ALPHA * x_ref[rows, :] + acc,
                                         gain_ref[...], bias_ref[...])


def _ff2(h, w, x, gain, bias, *, bm, bk):
    s, kdim = h.shape
    assert s % bm == 0 and kdim % bk == 0 and kdim // bk >= 2
    const = lambda shape: pl.BlockSpec(shape, lambda i, k: (0, 0))
    return pl.pallas_call(
        _ff2_kernel,
        grid=(s // bm, kdim // bk),
        in_specs=[pl.BlockSpec((bm, bk), lambda i, k: (i, k)),
                  pl.BlockSpec((bk, D_MODEL), lambda i, k: (k, 0)),
                  pl.BlockSpec((bm, D_MODEL), lambda i, k: (i, 0)),
                  const((1, D_MODEL)), const((1, D_MODEL))],
        out_specs=pl.BlockSpec((bm, D_MODEL), lambda i, k: (i, 0)),
        out_shape=jax.ShapeDtypeStruct((s, D_MODEL), F32),
        compiler_params=_cparams(2),
        name="ff2_ln",
    )(h, w, x, gain.reshape(1, -1), bias.reshape(1, -1))


def _pad_rows(w, rows):
    return jnp.pad(w, ((0, rows - w.shape[0]), (0, 0)))


def _layer(x, positions, w_in, mu_shift, w0, w_up, a0, a_up, g_up, k_k, k_a, r_k, gn_gain, gn_bias,
           lq1, lk1, lq2, lk2, subln_gain, w_out, ln1_gain, ln1_bias, w_ff1, w_ff2, ln2_gain, ln2_bias,
           *, bm, prep_tb, nc, bq):
    w3 = 3 * RWKV_WIDTH
    rwkv_cols = w3 + DECAY_LORA + AAA_LORA + GATE_LORA

    def pad_cols(a, lo, hi, width):
        return jnp.pad(a[..., lo:hi], [(0, 0)] * (a.ndim - 1) + [(0, width - (hi - lo))])

    def rwkv_layout(a):
        return jnp.concatenate([
            a[..., :w3],
            pad_cols(a, w3, w3 + DECAY_LORA, DECAY_PAD),
            pad_cols(a, w3 + DECAY_LORA, w3 + DECAY_LORA + AAA_LORA, AAA_PAD),
            pad_cols(a, w3 + DECAY_LORA + AAA_LORA, rwkv_cols, GATE_PAD)], axis=-1)

    w_all = _relayout_w_in(w_in.T, bk=256)
    proj_r, xb = _proj_cast(x, w_all, bm=bm, bn=RWKV_PCOLS // 2, n=RWKV_PCOLS)
    vt = _matmul(xb, w_all, bm=bm, bn=1024, n=DIFF_WIDTH, col0=V_COL0, out_dtype=BF16,
                 transpose_out=True, name="proj_vt")

    at, bt, kt, rt, v_r, wc, bonus, gate = _rwkv_prep(
        proj_r, rwkv_layout(mu_shift), w0,
        _pad_rows(w_up, DECAY_PAD).astype(BF16), a0, _pad_rows(a_up, AAA_PAD).astype(BF16),
        _pad_rows(g_up, GATE_PAD).astype(BF16), k_k, k_a, r_k.reshape(-1), tb=prep_tb)
    y_rwkv = _rwkv_chunks(at, bt, kt, rt, v_r, wc, bonus, gate, gn_gain, gn_bias, nc=nc)

    qk = _proj_rope(xb, w_all, positions, bm=bm, bn=1024, n=2 * DIFF_WIDTH, col0=QK_COL0)
    lam_vecs = jnp.stack([lq1, lk1, lq2, lk2]).astype(F32)
    y_diff = _attention(qk, vt, lam_vecs, subln_gain, bq=bq)

    wo = w_out.astype(BF16)
    x1, x1b = _outproj(y_rwkv, y_diff, wo[:RWKV_WIDTH], wo[RWKV_WIDTH:], x, ln1_gain, ln1_bias, bm=512)
    h, w_ff2b = _ff1(x1b, w_ff1, w_ff2, bm=bm, bn=1024)
    return _ff2(h, w_ff2b, x1, ln2_gain, ln2_bias, bm=1024, bk=1024)


def kernel(x, positions, w_in, mu_shift, rwkv_w0, rwkv_w_up, rwkv_a0, rwkv_a_up, rwkv_g_up, rwkv_k_k,
           rwkv_k_a, rwkv_r_k, rwkv_gn_gain, rwkv_gn_bias, diff_lambda_q1, diff_lambda_k1,
           diff_lambda_q2, diff_lambda_k2, diff_subln_gain, w_out, ln1_gain, ln1_bias, w_ff1, w_ff2,
           ln2_gain, ln2_bias):
    batch, seq, _ = x.shape
    assert batch == 1 and w_in.shape[0] == 1
    out = _layer(
        x[0], positions[0], w_in[0], mu_shift[0], rwkv_w0[0], rwkv_w_up[0], rwkv_a0[0], rwkv_a_up[0],
        rwkv_g_up[0], rwkv_k_k[0], rwkv_k_a[0], rwkv_r_k[0], rwkv_gn_gain[0], rwkv_gn_bias[0],
        diff_lambda_q1[0], diff_lambda_k1[0], diff_lambda_q2[0], diff_lambda_k2[0],
        diff_subln_gain[0], w_out[0], ln1_gain[0], ln1_bias[0], w_ff1[0], w_ff2[0], ln2_gain[0],
        ln2_bias[0], bm=1024, prep_tb=256, nc=4, bq=1024)
    return out[None]
```

```python
import functools
import math

import jax
import jax.numpy as jnp
from jax import lax
from jax.experimental import pallas as pl
from jax.experimental.pallas import tpu as pltpu

F32 = jnp.float32
BF16 = jnp.bfloat16

D_MODEL = 2048
RWKV_WIDTH = 1024
HEAD_DIM = 64
DECAY_LORA = 64
AAA_LORA = 64
GATE_LORA = 160
DIFF_WIDTH = 1024
DIFF_V_DIM = 128
DIFF_HEADS = 8
D_FF = 8192
ROPE_THETA = 10000.0
LN_EPS = 1e-5
RWKV_GN_EPS = 64e-5
SUBLN_EPS = 1e-5
DEEPNORM_ALPHA = 2.0 ** 0.25
LAMBDA_INIT = 0.8 - 0.6 * math.exp(0.0)
LOG2_E = math.log2(math.e)

LANES = 128
SUBLANES = 8
VMEM_LIMIT_BYTES = 56 * 1024 * 1024

DECAY_PAD = 128
AAA_PAD = 128
GATE_PAD = 256
RWKV_PCOLS = 3 * RWKV_WIDTH + DECAY_PAD + AAA_PAD + GATE_PAD
QK_COL0 = 4096
V_COL0 = QK_COL0 + 2 * DIFF_WIDTH
CHUNK = 64
GROUP_HEADS = 4
GROUP_W = GROUP_HEADS * HEAD_DIM


def _cparams(n_axes):
    return pltpu.CompilerParams(
        dimension_semantics=("arbitrary",) * n_axes,
        vmem_limit_bytes=VMEM_LIMIT_BYTES,
    )


def _dot(a, b):
    return jnp.dot(a.astype(BF16), b.astype(BF16), preferred_element_type=F32)


def _split3(x):
    h1 = x.astype(BF16)
    r1 = x - h1.astype(F32)
    h2 = r1.astype(BF16)
    h3 = (r1 - h2.astype(F32)).astype(BF16)
    return h1, h2, h3


def _split2(x):
    h1 = x.astype(BF16)
    h2 = (x - h1.astype(F32)).astype(BF16)
    return h1, h2


def _mm_kernel(a_ref, b_ref, o_ref, *, transpose_out):
    acc = jnp.dot(a_ref[...], b_ref[...], preferred_element_type=F32)
    if transpose_out:
        acc = acc.T
    o_ref[...] = acc.astype(o_ref.dtype)


def _matmul(a, b, *, bm, bn, n, col0=0, out_dtype, transpose_out=False, name):
    m, k = a.shape
    assert m % bm == 0 and n % bn == 0 and col0 % bn == 0
    jb = col0 // bn
    if transpose_out:
        out_spec = pl.BlockSpec((bn, bm), lambda i, j: (j, i))
        out_shape = jax.ShapeDtypeStruct((n, m), out_dtype)
    else:
        out_spec = pl.BlockSpec((bm, bn), lambda i, j: (i, j))
        out_shape = jax.ShapeDtypeStruct((m, n), out_dtype)
    return pl.pallas_call(
        functools.partial(_mm_kernel, transpose_out=transpose_out),
        grid=(m // bm, n // bn),
        in_specs=[pl.BlockSpec((bm, k), lambda i, j: (i, 0)),
                  pl.BlockSpec((k, bn), lambda i, j: (0, jb + j))],
        out_specs=out_spec,
        out_shape=out_shape,
        compiler_params=_cparams(2),
        name=name,
    )(a, b)


def _proj_cast_kernel(x_ref, w_ref, o_ref, xb_ref):
    @pl.when(pl.program_id(1) == 0)
    def _():
        xb_ref[...] = x_ref[...].astype(BF16)

    o_ref[...] = jnp.dot(xb_ref[...], w_ref[...], preferred_element_type=F32)


def _proj_cast(x, w_all, *, bm, bn, n):
    m, k = x.shape
    assert m % bm == 0 and n % bn == 0
    return pl.pallas_call(
        _proj_cast_kernel,
        grid=(m // bm, n // bn),
        in_specs=[pl.BlockSpec((bm, k), lambda i, j: (i, 0)),
                  pl.BlockSpec((k, bn), lambda i, j: (0, j))],
        out_specs=[pl.BlockSpec((bm, bn), lambda i, j: (i, j)),
                   pl.BlockSpec((bm, k), lambda i, j: (i, 0))],
        out_shape=[jax.ShapeDtypeStruct((m, n), F32), jax.ShapeDtypeStruct((m, k), BF16)],
        compiler_params=_cparams(2),
        name="proj_rwkv",
    )(x, w_all)


def _ff1_kernel(x_ref, w_ref, w2_ref, o_ref, w2b_ref, wb_ref):
    @pl.when(pl.program_id(1) == 0)
    def _():
        wb_ref[...] = w_ref[...].astype(BF16)

    acc = jnp.dot(x_ref[...], wb_ref[...], preferred_element_type=F32)
    o_ref[...] = jnp.square(jnp.maximum(acc, 0.0)).astype(o_ref.dtype)
    w2b_ref[...] = w2_ref[...].astype(BF16)


def _ff1(x, w, w2, *, bm, bn):
    m, k = x.shape
    _, n = w.shape
    assert m % bm == 0 and n % bn == 0
    steps = (n // bn) * (m // bm)
    rows2, cols2 = w2.shape
    assert rows2 % steps == 0 and (rows2 // steps) % (2 * SUBLANES) == 0
    slab = pl.BlockSpec((rows2 // steps, cols2), lambda j, i: (j * (m // bm) + i, 0))
    return pl.pallas_call(
        _ff1_kernel,
        grid=(n // bn, m // bm),
        in_specs=[pl.BlockSpec((bm, k), lambda j, i: (i, 0)),
                  pl.BlockSpec((k, bn), lambda j, i: (0, j)),
                  slab],
        out_specs=[pl.BlockSpec((bm, bn), lambda j, i: (i, j)), slab],
        out_shape=[jax.ShapeDtypeStruct((m, n), BF16), jax.ShapeDtypeStruct(w2.shape, BF16)],
        scratch_shapes=[pltpu.VMEM((k, bn), BF16)],
        compiler_params=_cparams(2),
        name="ff1",
    )(x, w, w2)


def _relayout_kernel(wt_ref, o_ref):
    cols = wt_ref.shape[1]
    w3 = 3 * RWKV_WIDTH
    a0 = w3 + DECAY_LORA
    g0 = a0 + AAA_LORA
    q0 = g0 + GATE_LORA
    zeros = lambda n: jnp.zeros((n, cols), F32)
    pieces = [wt_ref[:a0, :], zeros(DECAY_PAD - DECAY_LORA),
              wt_ref[a0:g0, :], zeros(AAA_PAD - AAA_LORA),
              wt_ref[g0:q0, :], zeros(GATE_PAD - GATE_LORA + QK_COL0 - RWKV_PCOLS),
              wt_ref[q0:q0 + DIFF_WIDTH, :] * (HEAD_DIM ** -0.5 * LOG2_E),
              wt_ref[q0 + DIFF_WIDTH:, :]]
    for c0 in range(0, o_ref.shape[1], RELAYOUT_COLS):
        rows = slice(c0, c0 + RELAYOUT_COLS)
        tile = jnp.concatenate(_row_window(pieces, c0, c0 + RELAYOUT_COLS), axis=0)
        o_ref[:, rows] = tile.T.astype(o_ref.dtype)


def _row_window(pieces, lo, hi):
    out, start = [], 0
    for p in pieces:
        stop = start + p.shape[0]
        a, b = max(lo, start), min(hi, stop)
        if a < b:
            out.append(p[a - start:b - start])
        start = stop
    return out


RELAYOUT_COLS = 512


def _relayout_w_in(w_in_t, *, bk):
    n, k = w_in_t.shape
    n_out = V_COL0 + DIFF_WIDTH
    assert n == 3 * RWKV_WIDTH + DECAY_LORA + AAA_LORA + GATE_LORA + 3 * DIFF_WIDTH
    assert n_out % RELAYOUT_COLS == 0
    return pl.pallas_call(
        _relayout_kernel,
        grid=(k // bk,),
        in_specs=[pl.BlockSpec((n, bk), lambda i: (0, i))],
        out_specs=pl.BlockSpec((bk, n_out), lambda i: (i, 0)),
        out_shape=jax.ShapeDtypeStruct((k, n_out), BF16),
        compiler_params=_cparams(1),
        name="relayout_w_in",
    )(w_in_t)


def _rwkv_prep_kernel(p_ref, mu_ref, w0_ref, wup_ref, a0_ref, aup_ref, gup_ref, kk_ref, ka_ref,
                      rk_ref, segin_ref, segout_ref, tri_ref,
                      at_ref, bt_ref, kt_ref, rt_ref, v_ref, wc_ref, bonus_ref, g_ref,
                      carry_ref, *, tb):
    i = pl.program_id(0)

    @pl.when(i == 0)
    def _():
        carry_ref[...] = jnp.zeros_like(carry_ref)

    def mixed(lo, hi):
        p = p_ref[:, lo:hi]
        prev_last = carry_ref[0:1, lo:hi]
        shifted = pltpu.roll(p, 1, axis=0)
        row = lax.broadcasted_iota(jnp.int32, p.shape, 0)
        prev = jnp.where(row == 0, prev_last, shifted)
        carry_ref[0:1, lo:hi] = p[tb - 1:tb, :]
        return p + (prev - p) * mu_ref[:, lo:hi]

    w3 = 3 * RWKV_WIDTH
    r = mixed(0, RWKV_WIDTH)
    k = mixed(RWKV_WIDTH, 2 * RWKV_WIDTH)
    v = mixed(2 * RWKV_WIDTH, w3)
    w_d = mixed(w3, w3 + DECAY_PAD)
    a_d = mixed(w3 + DECAY_PAD, w3 + DECAY_PAD + AAA_PAD)
    g_d = mixed(w3 + DECAY_PAD + AAA_PAD, RWKV_PCOLS)

    z = w0_ref[...] + _dot(jnp.tanh(w_d), wup_ref[...])
    lw = -math.exp(-0.5) * jax.nn.sigmoid(z)
    a = jax.nn.sigmoid(a0_ref[...] + _dot(a_d, aup_ref[...]))
    g_ref[...] = _dot(jax.nn.sigmoid(g_d), gup_ref[...])

    def segsum(x):
        h1, h2 = _split2(jnp.dot(x.astype(BF16), segin_ref[...], preferred_element_type=F32))
        return (jnp.dot(h1, segout_ref[...], preferred_element_type=F32)
                + jnp.dot(h2, segout_ref[...], preferred_element_type=F32))

    kk0 = k * kk_ref[...]
    kk = kk0 * lax.rsqrt(jnp.maximum(segsum(kk0 * kk0), 1e-24))
    k2 = k * (1.0 + (a - 1.0) * ka_ref[...])
    bonus_ref[...] = segsum(r * k2 * rk_ref[...]) * v

    tri = tri_ref[...]
    l1, l2, l3 = _split3(lw)
    cs = (jnp.dot(tri, l1, preferred_element_type=F32)
          + jnp.dot(tri, l2, preferred_element_type=F32)
          + jnp.dot(tri, l3, preferred_element_type=F32))
    cl = cs[:tb]
    tot = cs[tb:]
    e_neg = jnp.exp(-cl)
    at_ref[...] = (-kk * jnp.exp(cl - lw)).astype(BF16)
    bt_ref[...] = (kk * a * e_neg).astype(BF16)
    kt_ref[...] = (k2 * e_neg).astype(BF16)
    rt_ref[...] = (r * jnp.exp(cl)).astype(BF16)
    v_ref[...] = v.astype(BF16)
    for c in range(tb // CHUNK):
        wc_ref[c] = jnp.exp(tot[c * CHUNK:c * CHUNK + 1, :])


def _rwkv_prep(proj_r, mu, w0, wup, a0, aup, gup, k_k, k_a, r_k, *, tb):
    s = proj_r.shape[0]
    assert s % tb == 0 and tb % CHUNK == 0
    w = RWKV_WIDTH
    lane = jnp.arange(w)
    seg_in = (lane[:, None] // HEAD_DIM == jnp.arange(LANES)[None, :]).astype(BF16)
    t = jnp.arange(tb)
    same = t[:, None] // CHUNK == t[None, :] // CHUNK
    tri = jnp.concatenate([(same & (t[:, None] >= t[None, :])), same], axis=0).astype(BF16)
    row = lambda x: x.reshape(1, -1).astype(F32)
    const = lambda shape: pl.BlockSpec(shape, lambda i: (0,) * len(shape))
    tile = lambda width: pl.BlockSpec((tb, width), lambda i: (i, 0))
    outs = pl.pallas_call(
        functools.partial(_rwkv_prep_kernel, tb=tb),
        grid=(s // tb,),
        in_specs=[tile(RWKV_PCOLS), const((1, RWKV_PCOLS)), const((1, w)), const((DECAY_PAD, w)),
                  const((1, w)), const((AAA_PAD, w)), const((GATE_PAD, w)), const((1, w)),
                  const((1, w)), const((1, w)), const((w, LANES)), const((LANES, w)),
                  const((2 * tb, tb))],
        out_specs=[tile(w), tile(w), tile(w), tile(w), tile(w),
                   pl.BlockSpec((tb // CHUNK, 1, w), lambda i: (i, 0, 0)),
                   tile(w), tile(w)],
        out_shape=[jax.ShapeDtypeStruct((s, w), BF16)] * 5
        + [jax.ShapeDtypeStruct((s // CHUNK, 1, w), F32),
           jax.ShapeDtypeStruct((s, w), F32), jax.ShapeDtypeStruct((s, w), F32)],
        scratch_shapes=[pltpu.VMEM((SUBLANES, RWKV_PCOLS), F32)],
        compiler_params=_cparams(1),
        name="rwkv_prep",
    )(proj_r, row(mu), row(w0), wup, row(a0), aup, gup, row(k_k), row(k_a), row(r_k), seg_in, seg_in.T, tri)
    return outs


def _bdot(a, b):
    return jnp.dot(a, b, preferred_element_type=F32)


def _bdot_nt(a, b):
    return lax.dot_general(a, b, (((1,), (1,)), ((), ())), preferred_element_type=F32)


(MASK_STRICT, MASK_INCL, MASK_EYE, MASK_BLK8, MASK_OFF16, MASK_OFF32, MASK_OFF64) = range(7)


def _chunk_masks():
    a = jnp.arange(CHUNK)[:, None]
    b = jnp.arange(GROUP_W)[None, :] % CHUNK
    blk = lambda size: (a // size) == (b // size)
    masks = [a > b, a >= b, a == b, blk(8),
             blk(16) & ~blk(8), blk(32) & ~blk(16), blk(64) & ~blk(32)]
    return jnp.stack(masks).astype(F32)


def _rwkv_chunk_kernel(at_ref, bt_ref, kt_ref, rt_ref, v_ref, wc_ref, bonus_ref, g_ref,
                       gain_ref, bias_ref, mask_ref, head_ref, o_ref, state_ref, *, nc):
    @pl.when(pl.program_id(0) == 0)
    def _():
        state_ref[...] = jnp.zeros_like(state_ref)

    n = GROUP_W
    ng = RWKV_WIDTH // GROUP_W
    items = [(c, g) for c in range(nc) for g in range(ng)]
    lane_head = lax.broadcasted_iota(jnp.int32, (CHUNK, n), 1) >> 6

    def tile(ref, c, g):
        return ref[c * CHUNK:(c + 1) * CHUNK, g * n:(g + 1) * n]

    bf = lambda x: x.astype(BF16)

    def expand(x):
        x = bf(x)
        zero = jnp.zeros_like(x)
        return jnp.concatenate([jnp.where(lane_head == h, x, zero) for h in range(GROUP_HEADS)],
                               axis=0)

    def collapse(x):
        return (x[0:CHUNK] + x[CHUNK:2 * CHUNK] + x[2 * CHUNK:3 * CHUNK] + x[3 * CHUNK:4 * CHUNK])

    mask = lambda k: mask_ref[k]
    each = lambda f: [f(*it) for it in items]
    zipped = lambda f, *lists: [f(*args) for args in zip(*lists)]
    wdot = lambda w, e: _bdot(bf(w), e)

    at = each(lambda c, g: tile(at_ref, c, g))
    bt = each(lambda c, g: tile(bt_ref, c, g))
    kt = each(lambda c, g: tile(kt_ref, c, g))
    rt = each(lambda c, g: tile(rt_ref, c, g))
    wc = each(lambda c, g: wc_ref[c][:, g * n:(g + 1) * n])
    ae = [expand(x) for x in at]
    be = [expand(x) for x in bt]
    ke = [expand(x) for x in kt]
    ve = each(lambda c, g: expand(tile(v_ref, c, g)))
    bhe = zipped(lambda x, w: expand(x.astype(F32) * w), bt, wc)
    khe = zipped(lambda x, w: expand(x.astype(F32) * w), kt, wc)

    sc = zipped(lambda a, r, b, k: _bdot_nt(jnp.concatenate([a, r], axis=0),
                                            jnp.concatenate([b, k], axis=0)), at, rt, be, ke)
    lab = [x[:CHUNK, :n] * mask(MASK_STRICT) for x in sc]
    lak = [bf(x[:CHUNK, n:] * mask(MASK_STRICT)) for x in sc]
    arb = [bf(x[CHUNK:, :n] * mask(MASK_INCL)) for x in sc]
    ark = [bf(x[CHUNK:, n:] * mask(MASK_INCL)) for x in sc]
    bht = [collapse(x.T) for x in bhe]
    kht = [collapse(x.T) for x in khe]
    xv = zipped(lambda a, b, c, v: _bdot(jnp.concatenate([a, b, c], axis=0), v), lak, ark, kht, ve)
    lv = [x[:CHUNK] for x in xv]
    ark_v = [x[CHUNK:2 * CHUNK] for x in xv]
    kht_v = [x[2 * CHUNK:] for x in xv]

    pw = [x * mask(MASK_BLK8) for x in lab]
    minv = [mask(MASK_EYE) + x for x in pw]
    for _ in range(2):
        pw = zipped(lambda x: wdot(x, expand(x)), pw)
        minv = zipped(lambda x, m: m + wdot(x, expand(m)), pw, minv)
    for k in (MASK_OFF16, MASK_OFF32, MASK_OFF64):
        off = [expand(x * mask(k)) for x in lab]
        tmp = zipped(wdot, minv, off)
        minv = zipped(lambda m, t: m + wdot(t, expand(m)), minv, tmp)
    mb = [bf(m) for m in minv]

    p = [expand(_bdot(m, a)) for m, a in zip(mb, ae)]
    q = [expand(_bdot(m, expand(x))) for m, x in zip(mb, lv)]
    ab = zipped(lambda a, b: jnp.concatenate([a, b], axis=0), arb, bht)
    xp = zipped(_bdot, ab, p)
    xq = zipped(_bdot, ab, q)
    g_mat = zipped(lambda r, x: r.astype(F32) + x[:CHUNK], rt, xp)
    phi = zipped(lambda w, x: mask(MASK_EYE) * w + x[CHUNK:], wc, xp)
    gp = zipped(lambda g, f: bf(jnp.concatenate([g, f], axis=0)), g_mat, phi)
    y0 = zipped(lambda x, v: x[:CHUNK] + v, xq, ark_v)
    psi = zipped(lambda x, v: x[CHUNK:] + v, xq, kht_v)

    head = head_ref[...]

    def head_mean(x):
        h1, h2 = _split2(x)
        both = _bdot(jnp.concatenate([h1, h2], axis=0), head)
        return (both[:x.shape[0]] + both[x.shape[0]:]) * (1.0 / HEAD_DIM)

    state = [state_ref[g] for g in range(ng)]
    for c in range(nc):
        idx = [c * ng + g for g in range(ng)]
        se = [expand(s) for s in state]
        xs = [_bdot(gp[i], se[g]) for g, i in enumerate(idx)]
        state = [x[CHUNK:] + psi[i] for x, i in zip(xs, idx)]
        y = jnp.concatenate([x[:CHUNK] + y0[i] for x, i in zip(xs, idx)], axis=0)
        yc = y - head_mean(y)
        yn = yc * lax.rsqrt(head_mean(yc * yc) + RWKV_GN_EPS)
        rows = slice(c * CHUNK, (c + 1) * CHUNK)
        for g in range(ng):
            cols = slice(g * n, (g + 1) * n)
            out = (yn[g * CHUNK:(g + 1) * CHUNK] * gain_ref[:, cols] + bias_ref[:, cols]
                   + bonus_ref[rows, cols]) * g_ref[rows, cols]
            o_ref[rows, cols] = out.astype(o_ref.dtype)
    for g in range(ng):
        state_ref[g] = state[g]


def _rwkv_chunks(at, bt, kt, rt, v, wc, bonus, g, gn_gain, gn_bias, *, nc):
    s, w = at.shape
    tb = nc * CHUNK
    assert s % tb == 0 and w % GROUP_W == 0
    tile = pl.BlockSpec((tb, w), lambda ti: (ti, 0))
    rowspec = pl.BlockSpec((1, w), lambda ti: (0, 0))
    masks = _chunk_masks()
    lane = jnp.arange(GROUP_W)
    head = (lane[:, None] // HEAD_DIM == lane[None, :] // HEAD_DIM).astype(BF16)
    return pl.pallas_call(
        functools.partial(_rwkv_chunk_kernel, nc=nc),
        grid=(s // tb,),
        in_specs=[tile, tile, tile, tile, tile,
                  pl.BlockSpec((nc, 1, w), lambda ti: (ti, 0, 0)),
                  tile, tile, rowspec, rowspec,
                  pl.BlockSpec(masks.shape, lambda ti: (0, 0, 0)),
                  pl.BlockSpec(head.shape, lambda ti: (0, 0))],
        out_specs=tile,
        out_shape=jax.ShapeDtypeStruct((s, w), BF16),
        scratch_shapes=[pltpu.VMEM((w // GROUP_W, CHUNK, GROUP_W), F32)],
        compiler_params=_cparams(1),
        name="rwkv_chunks",
    )(at, bt, kt, rt, v, wc, bonus, g, gn_gain.reshape(1, -1).astype(F32),
      gn_bias.reshape(1, -1).astype(F32), masks, head)


ROPE_ROWS = 256


def _proj_rope_kernel(pos_ref, invf_ref, x_ref, w_ref, o_ref, cos_ref, sin_ref, *, bm, bn):
    def first_half(rows):
        lane = lax.broadcasted_iota(jnp.int32, (rows, LANES), 1)
        return (lane & (HEAD_DIM - 1)) < (HEAD_DIM // 2)

    @pl.when(pl.program_id(1) == 0)
    def _():
        ang = pos_ref[...].astype(F32) * invf_ref[...]
        cos_ref[...] = jnp.cos(ang)
        sin = jnp.sin(ang)
        sin_ref[...] = jnp.where(first_half(bm), -sin, sin)

    half = first_half(ROPE_ROWS)

    for r in range(bm // ROPE_ROWS):
        rows = slice(r * ROPE_ROWS, (r + 1) * ROPE_ROWS)
        acc = jnp.dot(x_ref[rows, :], w_ref[...], preferred_element_type=F32)
        cos = cos_ref[rows, :]
        sin = sin_ref[rows, :]
        for c in range(bn // LANES):
            t = acc[:, c * LANES:(c + 1) * LANES]
            partner = jnp.where(half, pltpu.roll(t, LANES - HEAD_DIM // 2, axis=1),
                                pltpu.roll(t, HEAD_DIM // 2, axis=1))
            o_ref[rows, c * LANES:(c + 1) * LANES] = (t * cos + partner * sin).astype(o_ref.dtype)


def _proj_rope(xb, w_all, positions, *, bm, bn, n, col0):
    s, k = xb.shape
    assert n % bn == 0 and col0 % bn == 0
    jb = col0 // bn
    half = jnp.arange(0, HEAD_DIM, 2, dtype=F32)
    inv_freq = ROPE_THETA ** (-half / HEAD_DIM)
    invf = jnp.tile(inv_freq, 2 * LANES // HEAD_DIM).reshape(1, LANES)
    return pl.pallas_call(
        functools.partial(_proj_rope_kernel, bm=bm, bn=bn),
        grid=(s // bm, n // bn),
        in_specs=[pl.BlockSpec((bm, 1), lambda i, j: (i, 0)),
                  pl.BlockSpec((1, LANES), lambda i, j: (0, 0)),
                  pl.BlockSpec((bm, k), lambda i, j: (i, 0)),
                  pl.BlockSpec((k, bn), lambda i, j: (0, jb + j))],
        out_specs=pl.BlockSpec((bm, bn), lambda i, j: (i, j)),
        out_shape=jax.ShapeDtypeStruct((s, n), BF16),
        scratch_shapes=[pltpu.VMEM((bm, LANES), F32), pltpu.VMEM((bm, LANES), F32)],
        compiler_params=_cparams(2),
        name="proj_qk_rope",
    )(positions.reshape(s, 1), invf, xb, w_all)


ATT_STRIP = 512


ATT_KB = 512


def _attn_kernel(lam_ref, gain_ref, q_ref, k_ref, vt_ref, o_ref, acc_ref, s0_ref, s1_ref, *, bq):
    i = pl.program_id(1)
    nq = 2 * bq
    kb = ATT_KB
    q = q_ref[...]
    lane = lax.broadcasted_iota(jnp.int32, q.shape, 1)
    zero = jnp.zeros_like(q)
    q_all = jnp.concatenate([jnp.where(lane < HEAD_DIM, q, zero),
                             jnp.where(lane >= HEAD_DIM, q, zero)], axis=0)
    s_slots = (s0_ref, s1_ref)
    all_cols = ((0, nq),)
    late_cols = ((kb, bq), (bq + kb, nq))

    def scores(j, slot):
        start = pl.multiple_of(j * kb, kb)
        s_slots[slot][...] = lax.dot_general(k_ref[pl.ds(start, kb), :], q_all,
                                             (((1,), (1,)), ((), ())), preferred_element_type=F32)

    def consume(j, slot, m, l, groups, key0):
        start = pl.multiple_of(j * kb, kb)
        vtb = vt_ref[:, pl.ds(start, kb)]
        m_parts = {c: m[:, c * ATT_STRIP:(c + 1) * ATT_STRIP] for c in range(nq // ATT_STRIP)}
        l_parts = {c: l[:, c * ATT_STRIP:(c + 1) * ATT_STRIP] for c in range(nq // ATT_STRIP)}
        for lo, hi in groups:
            p_strips, alphas = [], []
            for c in range(lo // ATT_STRIP, hi // ATT_STRIP):
                sl = slice(c * ATT_STRIP, (c + 1) * ATT_STRIP)
                s = s_slots[slot][:, sl]
                q_first = (c * ATT_STRIP) % bq
                if key0 is not None and key0 + kb - 1 > q_first:
                    key = lax.broadcasted_iota(jnp.int32, s.shape, 0) + key0
                    qry = lax.broadcasted_iota(jnp.int32, s.shape, 1) + q_first
                    s = jnp.where(key <= qry, s, -1e30)
                mn = jnp.maximum(m_parts[c], jnp.max(s, axis=0, keepdims=True))
                p = jnp.exp2(s - mn)
                alpha = jnp.exp2(m_parts[c] - mn)
                l_parts[c] = alpha * l_parts[c] + jnp.sum(p, axis=0, keepdims=True)
                m_parts[c] = mn
                alphas.append(alpha)
                p_strips.append(p.astype(BF16))
            p_all = jnp.concatenate(p_strips, axis=1)
            alpha = jnp.concatenate(alphas, axis=1)
            acc_ref[:, lo:hi] = alpha * acc_ref[:, lo:hi] + jnp.dot(vtb, p_all,
                                                                    preferred_element_type=F32)
        order = range(nq // ATT_STRIP)
        return (jnp.concatenate([m_parts[c] for c in order], axis=1),
                jnp.concatenate([l_parts[c] for c in order], axis=1))

    def pair(jj, carry):
        j = 2 * jj
        scores(j + 1, 1)
        m, l = consume(j, 0, carry[0], carry[1], all_cols, None)
        scores(j + 2, 0)
        return consume(j + 1, 1, m, l, all_cols, None)

    acc_ref[...] = jnp.zeros_like(acc_ref)
    scores(0, 0)
    init = (jnp.full((1, nq), -1e30, F32), jnp.zeros((1, nq), F32))
    m, l = lax.fori_loop(0, i, pair, init)
    scores(2 * i + 1, 1)
    m, l = consume(2 * i, 0, m, l, all_cols, 0)
    m, l = consume(2 * i + 1, 1, m, l, late_cols, kb)

    lam_v = lam_ref[...]
    lam = (jnp.exp(jnp.sum(lam_v[0:1] * lam_v[1:2], axis=1, keepdims=True))
           - jnp.exp(jnp.sum(lam_v[2:3] * lam_v[3:4], axis=1, keepdims=True)) + LAMBDA_INIT)
    o = acc_ref[...] * (1.0 / l)
    out = o[:, :bq] - lam * o[:, bq:]
    out = out * lax.rsqrt(jnp.mean(jnp.square(out), axis=0, keepdims=True) + SUBLN_EPS)
    out = out * gain_ref[...] * (1.0 - LAMBDA_INIT)
    o_ref[...] = out.T.astype(o_ref.dtype)


def _attention(qk, vt, lam_vecs, subln_gain, *, bq):
    s = qk.shape[0]
    assert s % bq == 0 and bq == 2 * ATT_KB
    return pl.pallas_call(
        functools.partial(_attn_kernel, bq=bq),
        grid=(DIFF_HEADS, s // bq),
        in_specs=[pl.BlockSpec((4, HEAD_DIM), lambda h, i: (0, 0)),
                  pl.BlockSpec((DIFF_V_DIM, 1), lambda h, i: (0, 0)),
                  pl.BlockSpec((bq, DIFF_V_DIM), lambda h, i: (i, h)),
                  pl.BlockSpec((s, DIFF_V_DIM), lambda h, i: (0, DIFF_HEADS + h)),
                  pl.BlockSpec((DIFF_V_DIM, s), lambda h, i: (h, 0))],
        out_specs=pl.BlockSpec((bq, DIFF_V_DIM), lambda h, i: (i, h)),
        out_shape=jax.ShapeDtypeStruct((s, DIFF_WIDTH), BF16),
        scratch_shapes=[pltpu.VMEM((DIFF_V_DIM, 2 * bq), F32), pltpu.VMEM((ATT_KB, 2 * bq), F32),
                        pltpu.VMEM((ATT_KB, 2 * bq), F32)],
        compiler_params=_cparams(2),
        name="diff_attn",
    )(lam_vecs, subln_gain.reshape(-1, 1).astype(F32), qk, qk, vt)


def _layer_norm(z, gain, bias):
    mean = jnp.mean(z, axis=-1, keepdims=True)
    zc = z - mean
    var = jnp.mean(zc * zc, axis=-1, keepdims=True)
    return zc * lax.rsqrt(var + LN_EPS) * gain + bias


LN_ROWS = 128


def _outproj_kernel(yr_ref, yd_ref, wr_ref, wd_ref, x_ref, gain_ref, bias_ref, o32_ref, o16_ref):
    for r in range(x_ref.shape[0] // LN_ROWS):
        rows = slice(r * LN_ROWS, (r + 1) * LN_ROWS)
        mix = (jnp.dot(yr_ref[rows, :], wr_ref[...], preferred_element_type=F32)
               + jnp.dot(yd_ref[rows, :], wd_ref[...], preferred_element_type=F32))
        y = _layer_norm(DEEPNORM_ALPHA * x_ref[rows, :] + mix, gain_ref[...], bias_ref[...])
        o32_ref[rows, :] = y
        o16_ref[rows, :] = y.astype(BF16)


def _outproj(y_rwkv, y_diff, w_r, w_d, x, gain, bias, *, bm):
    s = x.shape[0]
    const = lambda shape: pl.BlockSpec(shape, lambda i: (0, 0))
    return pl.pallas_call(
        _outproj_kernel,
        grid=(s // bm,),
        in_specs=[pl.BlockSpec((bm, RWKV_WIDTH), lambda i: (i, 0)),
                  pl.BlockSpec((bm, DIFF_WIDTH), lambda i: (i, 0)),
                  const((RWKV_WIDTH, D_MODEL)), const((DIFF_WIDTH, D_MODEL)),
                  pl.BlockSpec((bm, D_MODEL), lambda i: (i, 0)),
                  const((1, D_MODEL)), const((1, D_MODEL))],
        out_specs=[pl.BlockSpec((bm, D_MODEL), lambda i: (i, 0))] * 2,
        out_shape=[jax.ShapeDtypeStruct((s, D_MODEL), F32), jax.ShapeDtypeStruct((s, D_MODEL), BF16)],
        compiler_params=_cparams(1),
        name="outproj_ln",
    )(y_rwkv, y_diff, w_r, w_d, x, gain.reshape(1, -1), bias.reshape(1, -1))


def _ff2_kernel(h_ref, w_ref, x_ref, gain_ref, bias_ref, o_ref):
    kk = pl.program_id(1)
    part = lambda: jnp.dot(h_ref[...], w_ref[...], preferred_element_type=F32)

    @pl.when(kk == 0)
    def _():
        o_ref[...] = part()

    @pl.when((kk > 0) & (kk < pl.num_programs(1) - 1))
    def _():
        o_ref[...] += part()

    @pl.when(kk == pl.num_programs(1) - 1)
    def _():
        for r in range(o_ref.shape[0] // LN_ROWS):
            rows = slice(r * LN_ROWS, (r + 1) * LN_ROWS)
            acc = o_ref[rows, :] + jnp.dot(h_ref[rows, :], w_ref[...], preferred_element_type=F32)
            o_ref[rows, :] = _layer_norm(DEEPNORM_ALPHA * x_ref[rows, :] + acc,
                                         gain_ref[...], bias_ref[...])


def _ff2(h, w, x, gain, bias, *, bm, bk):
    s, kdim = h.shape
    assert s % bm == 0 and kdim % bk == 0 and kdim // bk >= 2
    const = lambda shape: pl.BlockSpec(shape, lambda i, k: (0, 0))
    return pl.pallas_call(
        _ff2_kernel,
        grid=(s // bm, kdim // bk),
        in_specs=[pl.BlockSpec((bm, bk), lambda i, k: (i, k)),
                  pl.BlockSpec((bk, D_MODEL), lambda i, k: (k, 0)),
                  pl.BlockSpec((bm, D_MODEL), lambda i, k: (i, 0)),
                  const((1, D_MODEL)), const((1, D_MODEL))],
        out_specs=pl.BlockSpec((bm, D_MODEL), lambda i, k: (i, 0)),
        out_shape=jax.ShapeDtypeStruct((s, D_MODEL), F32),
        compiler_params=_cparams(2),
        name="ff2_ln",
    )(h, w, x, gain.reshape(1, -1), bias.reshape(1, -1))


def _pad_rows(w, rows):
    return jnp.pad(w, ((0, rows - w.shape[0]), (0, 0)))


def _layer(x, positions, w_in, mu_shift, w0, w_up, a0, a_up, g_up, k_k, k_a, r_k, gn_gain, gn_bias,
           lq1, lk1, lq2, lk2, subln_gain, w_out, ln1_gain, ln1_bias, w_ff1, w_ff2, ln2_gain, ln2_bias,
           *, bm, prep_tb, nc, bq):
    w3 = 3 * RWKV_WIDTH
    rwkv_cols = w3 + DECAY_LORA + AAA_LORA + GATE_LORA

    def pad_cols(a, lo, hi, width):
        return jnp.pad(a[..., lo:hi], [(0, 0)] * (a.ndim - 1) + [(0, width - (hi - lo))])

    def rwkv_layout(a):
        return jnp.concatenate([
            a[..., :w3],
            pad_cols(a, w3, w3 + DECAY_LORA, DECAY_PAD),
            pad_cols(a, w3 + DECAY_LORA, w3 + DECAY_LORA + AAA_LORA, AAA_PAD),
            pad_cols(a, w3 + DECAY_LORA + AAA_LORA, rwkv_cols, GATE_PAD)], axis=-1)

    w_all = _relayout_w_in(w_in.T, bk=256)
    proj_r, xb = _proj_cast(x, w_all, bm=bm, bn=RWKV_PCOLS // 2, n=RWKV_PCOLS)
    vt = _matmul(xb, w_all, bm=bm, bn=1024, n=DIFF_WIDTH, col0=V_COL0, out_dtype=BF16,
                 transpose_out=True, name="proj_vt")

    at, bt, kt, rt, v_r, wc, bonus, gate = _rwkv_prep(
        proj_r, rwkv_layout(mu_shift), w0,
        _pad_rows(w_up, DECAY_PAD).astype(BF16), a0, _pad_rows(a_up, AAA_PAD).astype(BF16),
        _pad_rows(g_up, GATE_PAD).astype(BF16), k_k, k_a, r_k.reshape(-1), tb=prep_tb)
    y_rwkv = _rwkv_chunks(at, bt, kt, rt, v_r, wc, bonus, gate, gn_gain, gn_bias, nc=nc)

    qk = _proj_rope(xb, w_all, positions, bm=bm, bn=1024, n=2 * DIFF_WIDTH, col0=QK_COL0)
    lam_vecs = jnp.stack([lq1, lk1, lq2, lk2]).astype(F32)
    y_diff = _attention(qk, vt, lam_vecs, subln_gain, bq=bq)

    wo = w_out.astype(BF16)
    x1, x1b = _outproj(y_rwkv, y_diff, wo[:RWKV_WIDTH], wo[RWKV_WIDTH:], x, ln1_gain, ln1_bias, bm=512)
    h, w_ff2b = _ff1(x1b, w_ff1, w_ff2, bm=bm, bn=1024)
    return _ff2(h, w_ff2b, x1, ln2_gain, ln2_bias, bm=1024, bk=1024)


def kernel(x, positions, w_in, mu_shift, rwkv_w0, rwkv_w_up, rwkv_a0, rwkv_a_up, rwkv_g_up, rwkv_k_k,
           rwkv_k_a, rwkv_r_k, rwkv_gn_gain, rwkv_gn_bias, diff_lambda_q1, diff_lambda_k1,
           diff_lambda_q2, diff_lambda_k2, diff_subln_gain, w_out, ln1_gain, ln1_bias, w_ff1, w_ff2,
           ln2_gain, ln2_bias):
    batch, seq, _ = x.shape
    assert batch == 1 and w_in.shape[0] == 1
    out = _layer(
        x[0], positions[0], w_in[0], mu_shift[0], rwkv_w0[0], rwkv_w_up[0], rwkv_a0[0], rwkv_a_up[0],
        rwkv_g_up[0], rwkv_k_k[0], rwkv_k_a[0], rwkv_r_k[0], rwkv_gn_gain[0], rwkv_gn_bias[0],
        diff_lambda_q1[0], diff_lambda_k1[0], diff_lambda_q2[0], diff_lambda_k2[0],
        diff_subln_gain[0], w_out[0], ln1_gain[0], ln1_bias[0], w_ff1[0], w_ff2[0], ln2_gain[0],
        ln2_bias[0], bm=1024, prep_tb=256, nc=4, bq=1024)
    return out[None]
```

```python
import functools
import math

import jax
import jax.numpy as jnp
from jax import lax
from jax.experimental import pallas as pl
from jax.experimental.pallas import tpu as pltpu

F32 = jnp.float32
BF16 = jnp.bfloat16

D_MODEL = 2048
RWKV_WIDTH = 1024
HEAD_DIM = 64
DECAY_LORA = 64
AAA_LORA = 64
GATE_LORA = 160
DIFF_WIDTH = 1024
DIFF_V_DIM = 128
DIFF_HEADS = 8
ROPE_THETA = 10000.0
LN_EPS = 1e-5
RWKV_GN_EPS = 64e-5
SUBLN_EPS = 1e-5
DEEPNORM_ALPHA = 2.0 ** 0.25
LAMBDA_INIT = 0.8 - 0.6 * math.exp(0.0)
LOG2_E = math.log2(math.e)

LANES = 128
SUBLANES = 8
VMEM_LIMIT_BYTES = 56 * 1024 * 1024

DECAY_PAD = 128
AAA_PAD = 128
GATE_PAD = 256
RWKV_PCOLS = 3 * RWKV_WIDTH + DECAY_PAD + AAA_PAD + GATE_PAD
PROJ_BN = 1024
QK_COL0 = -(-RWKV_PCOLS // PROJ_BN) * PROJ_BN
V_COL0 = QK_COL0 + 2 * DIFF_WIDTH
CHUNK = 64
GROUP_HEADS = 4
GROUP_W = GROUP_HEADS * HEAD_DIM


def _cparams(n_axes):
    return pltpu.CompilerParams(
        dimension_semantics=("arbitrary",) * n_axes,
        vmem_limit_bytes=VMEM_LIMIT_BYTES,
    )


def _dot(a, b):
    return jnp.dot(a.astype(BF16), b.astype(BF16), preferred_element_type=F32)


def _split3(x):
    h1 = x.astype(BF16)
    r1 = x - h1.astype(F32)
    h2 = r1.astype(BF16)
    h3 = (r1 - h2.astype(F32)).astype(BF16)
    return h1, h2, h3


def _split2(x):
    h1 = x.astype(BF16)
    h2 = (x - h1.astype(F32)).astype(BF16)
    return h1, h2


def _mm_kernel(a_ref, b_ref, o_ref, *, transpose_out):
    acc = jnp.dot(a_ref[...], b_ref[...], preferred_element_type=F32)
    if transpose_out:
        acc = acc.T
    o_ref[...] = acc.astype(o_ref.dtype)


def _matmul(a, b, *, bm, bn, n, col0=0, out_dtype, transpose_out=False, name):
    m, k = a.shape
    assert m % bm == 0 and n % bn == 0 and col0 % bn == 0
    jb = col0 // bn
    if transpose_out:
        out_spec = pl.BlockSpec((bn, bm), lambda i, j: (j, i))
        out_shape = jax.ShapeDtypeStruct((n, m), out_dtype)
    else:
        out_spec = pl.BlockSpec((bm, bn), lambda i, j: (i, j))
        out_shape = jax.ShapeDtypeStruct((m, n), out_dtype)
    return pl.pallas_call(
        functools.partial(_mm_kernel, transpose_out=transpose_out),
        grid=(m // bm, n // bn),
        in_specs=[pl.BlockSpec((bm, k), lambda i, j: (i, 0)),
                  pl.BlockSpec((k, bn), lambda i, j: (0, jb + j))],
        out_specs=out_spec,
        out_shape=out_shape,
        compiler_params=_cparams(2),
        name=name,
    )(a, b)


def _proj_cast_kernel(x_ref, w_ref, o_ref, xb_ref):
    @pl.when(pl.program_id(1) == 0)
    def _():
        xb_ref[...] = x_ref[...].astype(BF16)

    o_ref[...] = jnp.dot(xb_ref[...], w_ref[...], preferred_element_type=F32)


def _proj_cast(x, w_all, *, bm, bn, n):
    m, k = x.shape
    assert m % bm == 0 and n % bn == 0
    return pl.pallas_call(
        _proj_cast_kernel,
        grid=(m // bm, n // bn),
        in_specs=[pl.BlockSpec((bm, k), lambda i, j: (i, 0)),
                  pl.BlockSpec((k, bn), lambda i, j: (0, j))],
        out_specs=[pl.BlockSpec((bm, bn), lambda i, j: (i, j)),
                   pl.BlockSpec((bm, k), lambda i, j: (i, 0))],
        out_shape=[jax.ShapeDtypeStruct((m, n), F32), jax.ShapeDtypeStruct((m, k), BF16)],
        compiler_params=_cparams(2),
        name="proj_rwkv",
    )(x, w_all)


def _ff1_kernel(x_ref, w_ref, w2_ref, o_ref, w2b_ref, wb_ref):
    @pl.when(pl.program_id(1) == 0)
    def _():
        wb_ref[...] = w_ref[...].astype(BF16)

    acc = jnp.dot(x_ref[...], wb_ref[...], preferred_element_type=F32)
    o_ref[...] = jnp.square(jnp.maximum(acc, 0.0)).astype(o_ref.dtype)
    w2b_ref[...] = w2_ref[...].astype(BF16)


def _ff1(x, w, w2, *, bm, bn):
    m, k = x.shape
    _, n = w.shape
    assert m % bm == 0 and n % bn == 0
    steps = (n // bn) * (m // bm)
    rows2, cols2 = w2.shape
    assert rows2 % steps == 0 and (rows2 // steps) % (2 * SUBLANES) == 0
    slab = pl.BlockSpec((rows2 // steps, cols2), lambda j, i: (j * (m // bm) + i, 0))
    return pl.pallas_call(
        _ff1_kernel,
        grid=(n // bn, m // bm),
        in_specs=[pl.BlockSpec((bm, k), lambda j, i: (i, 0)),
                  pl.BlockSpec((k, bn), lambda j, i: (0, j)),
                  slab],
        out_specs=[pl.BlockSpec((bm, bn), lambda j, i: (i, j)), slab],
        out_shape=[jax.ShapeDtypeStruct((m, n), BF16), jax.ShapeDtypeStruct(w2.shape, BF16)],
        scratch_shapes=[pltpu.VMEM((k, bn), BF16)],
        compiler_params=_cparams(2),
        name="ff1",
    )(x, w, w2)


def _relayout_kernel(wt_ref, o_ref):
    cols = wt_ref.shape[1]
    w3 = 3 * RWKV_WIDTH
    a0 = w3 + DECAY_LORA
    g0 = a0 + AAA_LORA
    q0 = g0 + GATE_LORA
    zeros = lambda n: jnp.zeros((n, cols), F32)
    pieces = [wt_ref[:a0, :], zeros(DECAY_PAD - DECAY_LORA),
              wt_ref[a0:g0, :], zeros(AAA_PAD - AAA_LORA),
              wt_ref[g0:q0, :], zeros(GATE_PAD - GATE_LORA + QK_COL0 - RWKV_PCOLS),
              wt_ref[q0:q0 + DIFF_WIDTH, :] * (HEAD_DIM ** -0.5 * LOG2_E),
              wt_ref[q0 + DIFF_WIDTH:, :]]
    for c0 in range(0, o_ref.shape[1], RELAYOUT_COLS):
        rows = slice(c0, c0 + RELAYOUT_COLS)
        tile = jnp.concatenate(_row_window(pieces, c0, c0 + RELAYOUT_COLS), axis=0)
        o_ref[:, rows] = tile.T.astype(o_ref.dtype)


def _row_window(pieces, lo, hi):
    out, start = [], 0
    for p in pieces:
        stop = start + p.shape[0]
        a, b = max(lo, start), min(hi, stop)
        if a < b:
            out.append(p[a - start:b - start])
        start = stop
    return out


RELAYOUT_COLS = 512


def _relayout_w_in(w_in_t, *, bk):
    n, k = w_in_t.shape
    n_out = V_COL0 + DIFF_WIDTH
    assert n == 3 * RWKV_WIDTH + DECAY_LORA + AAA_LORA + GATE_LORA + 3 * DIFF_WIDTH
    assert n_out % RELAYOUT_COLS == 0
    return pl.pallas_call(
        _relayout_kernel,
        grid=(k // bk,),
        in_specs=[pl.BlockSpec((n, bk), lambda i: (0, i))],
        out_specs=pl.BlockSpec((bk, n_out), lambda i: (i, 0)),
        out_shape=jax.ShapeDtypeStruct((k, n_out), BF16),
        compiler_params=_cparams(1),
        name="relayout_w_in",
    )(w_in_t)


def _rwkv_prep_kernel(p_ref, mu_ref, w0_ref, wup_ref, a0_ref, aup_ref, gup_ref, kk_ref, ka_ref,
                      rk_ref, segin_ref, segout_ref, tri_ref,
                      at_ref, bt_ref, kt_ref, rt_ref, v_ref, wc_ref, bonus_ref, g_ref,
                      carry_ref, *, tb):
    i = pl.program_id(0)

    @pl.when(i == 0)
    def _():
        carry_ref[...] = jnp.zeros_like(carry_ref)

    def mixed(lo, hi):
        p = p_ref[:, lo:hi]
        prev_last = carry_ref[0:1, lo:hi]
        shifted = pltpu.roll(p, 1, axis=0)
        row = lax.broadcasted_iota(jnp.int32, p.shape, 0)
        prev = jnp.where(row == 0, prev_last, shifted)
        carry_ref[0:1, lo:hi] = p[tb - 1:tb, :]
        return p + (prev - p) * mu_ref[:, lo:hi]

    w3 = 3 * RWKV_WIDTH
    r = mixed(0, RWKV_WIDTH)
    k = mixed(RWKV_WIDTH, 2 * RWKV_WIDTH)
    v = mixed(2 * RWKV_WIDTH, w3)
    w_d = mixed(w3, w3 + DECAY_PAD)
    a_d = mixed(w3 + DECAY_PAD, w3 + DECAY_PAD + AAA_PAD)
    g_d = mixed(w3 + DECAY_PAD + AAA_PAD, RWKV_PCOLS)

    z = w0_ref[...] + _dot(jnp.tanh(w_d), wup_ref[...])
    lw = -math.exp(-0.5) * jax.nn.sigmoid(z)
    a = jax.nn.sigmoid(a0_ref[...] + _dot(a_d, aup_ref[...]))
    g_ref[...] = _dot(jax.nn.sigmoid(g_d), gup_ref[...])

    def segsum(x):
        h1, h2 = _split2(jnp.dot(x.astype(BF16), segin_ref[...], preferred_element_type=F32))
        return (jnp.dot(h1, segout_ref[...], preferred_element_type=F32)
                + jnp.dot(h2, segout_ref[...], preferred_element_type=F32))

    kk0 = k * kk_ref[...]
    kk = kk0 * lax.rsqrt(jnp.maximum(segsum(kk0 * kk0), 1e-24))
    k2 = k * (1.0 + (a - 1.0) * ka_ref[...])
    bonus_ref[...] = segsum(r * k2 * rk_ref[...]) * v

    tri = tri_ref[...]
    l1, l2, l3 = _split3(lw)
    cs = (jnp.dot(tri, l1, preferred_element_type=F32)
          + jnp.dot(tri, l2, preferred_element_type=F32)
          + jnp.dot(tri, l3, preferred_element_type=F32))
    cl = cs[:tb]
    tot = cs[tb:]
    e_neg = jnp.exp(-cl)
    at_ref[...] = (-kk * jnp.exp(cl - lw)).astype(BF16)
    bt_ref[...] = (kk * a * e_neg).astype(BF16)
    kt_ref[...] = (k2 * e_neg).astype(BF16)
    rt_ref[...] = (r * jnp.exp(cl)).astype(BF16)
    v_ref[...] = v.astype(BF16)
    for c in range(tb // CHUNK):
        wc_ref[c] = jnp.exp(tot[c * CHUNK:c * CHUNK + 1, :])


def _rwkv_prep(proj_r, mu, w0, wup, a0, aup, gup, k_k, k_a, r_k, *, tb):
    s = proj_r.shape[0]
    assert s % tb == 0 and tb % CHUNK == 0
    w = RWKV_WIDTH
    lane = jnp.arange(w)
    seg_in = (lane[:, None] // HEAD_DIM == jnp.arange(LANES)[None, :]).astype(BF16)
    t = jnp.arange(tb)
    same = t[:, None] // CHUNK == t[None, :] // CHUNK
    tri = jnp.concatenate([(same & (t[:, None] >= t[None, :])), same], axis=0).astype(BF16)
    row = lambda x: x.reshape(1, -1).astype(F32)
    const = lambda shape: pl.BlockSpec(shape, lambda i: (0,) * len(shape))
    tile = lambda width: pl.BlockSpec((tb, width), lambda i: (i, 0))
    outs = pl.pallas_call(
        functools.partial(_rwkv_prep_kernel, tb=tb),
        grid=(s // tb,),
        in_specs=[tile(RWKV_PCOLS), const((1, RWKV_PCOLS)), const((1, w)), const((DECAY_PAD, w)),
                  const((1, w)), const((AAA_PAD, w)), const((GATE_PAD, w)), const((1, w)),
                  const((1, w)), const((1, w)), const((w, LANES)), const((LANES, w)),
                  const((2 * tb, tb))],
        out_specs=[tile(w), tile(w), tile(w), tile(w), tile(w),
                   pl.BlockSpec((tb // CHUNK, 1, w), lambda i: (i, 0, 0)),
                   tile(w), tile(w)],
        out_shape=[jax.ShapeDtypeStruct((s, w), BF16)] * 5
        + [jax.ShapeDtypeStruct((s // CHUNK, 1, w), F32),
           jax.ShapeDtypeStruct((s, w), F32), jax.ShapeDtypeStruct((s, w), F32)],
        scratch_shapes=[pltpu.VMEM((SUBLANES, RWKV_PCOLS), F32)],
        compiler_params=_cparams(1),
        name="rwkv_prep",
    )(proj_r, row(mu), row(w0), wup, row(a0), aup, gup, row(k_k), row(k_a), row(r_k), seg_in, seg_in.T, tri)
    return outs


def _bdot(a, b):
    return jnp.dot(a, b, preferred_element_type=F32)


def _bdot_nt(a, b):
    return lax.dot_general(a, b, (((1,), (1,)), ((), ())), preferred_element_type=F32)


(MASK_STRICT, MASK_INCL, MASK_EYE, MASK_BLK8, MASK_OFF16, MASK_OFF32, MASK_OFF64) = range(7)


def _chunk_masks():
    a = jnp.arange(CHUNK)[:, None]
    b = jnp.arange(GROUP_W)[None, :] % CHUNK
    blk = lambda size: (a // size) == (b // size)
    masks = [a > b, a >= b, a == b, blk(8),
             blk(16) & ~blk(8), blk(32) & ~blk(16), blk(64) & ~blk(32)]
    return jnp.stack(masks).astype(F32)


def _rwkv_chunk_kernel(at_ref, bt_ref, kt_ref, rt_ref, v_ref, wc_ref, bonus_ref, g_ref,
                       gain_ref, bias_ref, mask_ref, head_ref, o_ref, state_ref, *, nc):
    @pl.when(pl.program_id(0) == 0)
    def _():
        state_ref[...] = jnp.zeros_like(state_ref)

    n = GROUP_W
    ng = RWKV_WIDTH // GROUP_W
    items = [(c, g) for c in range(nc) for g in range(ng)]
    lane_head = lax.broadcasted_iota(jnp.int32, (CHUNK, n), 1) >> 6

    def tile(ref, c, g):
        return ref[c * CHUNK:(c + 1) * CHUNK, g * n:(g + 1) * n]

    bf = lambda x: x.astype(BF16)

    def expand(x):
        x = bf(x)
        zero = jnp.zeros_like(x)
        return jnp.concatenate([jnp.where(lane_head == h, x, zero) for h in range(GROUP_HEADS)],
                               axis=0)

    def collapse(x):
        return (x[0:CHUNK] + x[CHUNK:2 * CHUNK] + x[2 * CHUNK:3 * CHUNK] + x[3 * CHUNK:4 * CHUNK])

    mask = lambda k: mask_ref[k]
    each = lambda f: [f(*it) for it in items]
    zipped = lambda f, *lists: [f(*args) for args in zip(*lists)]
    wdot = lambda w, e: _bdot(bf(w), e)

    at = each(lambda c, g: tile(at_ref, c, g))
    bt = each(lambda c, g: tile(bt_ref, c, g))
    kt = each(lambda c, g: tile(kt_ref, c, g))
    rt = each(lambda c, g: tile(rt_ref, c, g))
    wc = each(lambda c, g: wc_ref[c][:, g * n:(g + 1) * n])
    ae = [expand(x) for x in at]
    be = [expand(x) for x in bt]
    ke = [expand(x) for x in kt]
    ve = each(lambda c, g: expand(tile(v_ref, c, g)))
    bhe = zipped(lambda x, w: expand(x.astype(F32) * w), bt, wc)
    khe = zipped(lambda x, w: expand(x.astype(F32) * w), kt, wc)

    sc = zipped(lambda a, r, b, k: _bdot_nt(jnp.concatenate([a, r], axis=0),
                                            jnp.concatenate([b, k], axis=0)), at, rt, be, ke)
    lab = [x[:CHUNK, :n] * mask(MASK_STRICT) for x in sc]
    lak = [bf(x[:CHUNK, n:] * mask(MASK_STRICT)) for x in sc]
    arb = [bf(x[CHUNK:, :n] * mask(MASK_INCL)) for x in sc]
    ark = [bf(x[CHUNK:, n:] * mask(MASK_INCL)) for x in sc]
    bht = [collapse(x.T) for x in bhe]
    kht = [collapse(x.T) for x in khe]
    xv = zipped(lambda a, b, c, v: _bdot(jnp.concatenate([a, b, c], axis=0), v), lak, ark, kht, ve)
    lv = [x[:CHUNK] for x in xv]
    ark_v = [x[CHUNK:2 * CHUNK] for x in xv]
    kht_v = [x[2 * CHUNK:] for x in xv]

    pw = [x * mask(MASK_BLK8) for x in lab]
    minv = [mask(MASK_EYE) + x for x in pw]
    for _ in range(2):
        pw = zipped(lambda x: wdot(x, expand(x)), pw)
        minv = zipped(lambda x, m: m + wdot(x, expand(m)), pw, minv)
    for k in (MASK_OFF16, MASK_OFF32, MASK_OFF64):
        off = [expand(x * mask(k)) for x in lab]
        tmp = zipped(wdot, minv, off)
        minv = zipped(lambda m, t: m + wdot(t, expand(m)), minv, tmp)
    mb = [bf(m) for m in minv]

    p = [expand(_bdot(m, a)) for m, a in zip(mb, ae)]
    q = [expand(_bdot(m, expand(x))) for m, x in zip(mb, lv)]
    ab = zipped(lambda a, b: jnp.concatenate([a, b], axis=0), arb, bht)
    xp = zipped(_bdot, ab, p)
    xq = zipped(_bdot, ab, q)
    g_mat = zipped(lambda r, x: r.astype(F32) + x[:CHUNK], rt, xp)
    phi = zipped(lambda w, x: mask(MASK_EYE) * w + x[CHUNK:], wc, xp)
    gp = zipped(lambda g, f: bf(jnp.concatenate([g, f], axis=0)), g_mat, phi)
    y0 = zipped(lambda x, v: x[:CHUNK] + v, xq, ark_v)
    psi = zipped(lambda x, v: x[CHUNK:] + v, xq, kht_v)

    head = head_ref[...]

    def head_mean(x):
        h1, h2 = _split2(x)
        both = _bdot(jnp.concatenate([h1, h2], axis=0), head)
        return (both[:x.shape[0]] + both[x.shape[0]:]) * (1.0 / HEAD_DIM)

    state = [state_ref[g] for g in range(ng)]
    for c in range(nc):
        idx = [c * ng + g for g in range(ng)]
        se = [expand(s) for s in state]
        xs = [_bdot(gp[i], se[g]) for g, i in enumerate(idx)]
        state = [x[CHUNK:] + psi[i] for x, i in zip(xs, idx)]
        y = jnp.concatenate([x[:CHUNK] + y0[i] for x, i in zip(xs, idx)], axis=0)
        yc = y - head_mean(y)
        yn = yc * lax.rsqrt(head_mean(yc * yc) + RWKV_GN_EPS)
        rows = slice(c * CHUNK, (c + 1) * CHUNK)
        for g in range(ng):
            cols = slice(g * n, (g + 1) * n)
            out = (yn[g * CHUNK:(g + 1) * CHUNK] * gain_ref[:, cols] + bias_ref[:, cols]
                   + bonus_ref[rows, cols]) * g_ref[rows, cols]
            o_ref[rows, cols] = out.astype(o_ref.dtype)
    for g in range(ng):
        state_ref[g] = state[g]


def _rwkv_chunks(at, bt, kt, rt, v, wc, bonus, g, gn_gain, gn_bias, *, nc):
    s, w = at.shape
    tb = nc * CHUNK
    assert s % tb == 0 and w % GROUP_W == 0
    tile = pl.BlockSpec((tb, w), lambda ti: (ti, 0))
    rowspec = pl.BlockSpec((1, w), lambda ti: (0, 0))
    masks = _chunk_masks()
    lane = jnp.arange(GROUP_W)
    head = (lane[:, None] // HEAD_DIM == lane[None, :] // HEAD_DIM).astype(BF16)
    return pl.pallas_call(
        functools.partial(_rwkv_chunk_kernel, nc=nc),
        grid=(s // tb,),
        in_specs=[tile, tile, tile, tile, tile,
                  pl.BlockSpec((nc, 1, w), lambda ti: (ti, 0, 0)),
                  tile, tile, rowspec, rowspec,
                  pl.BlockSpec(masks.shape, lambda ti: (0, 0, 0)),
                  pl.BlockSpec(head.shape, lambda ti: (0, 0))],
        out_specs=tile,
        out_shape=jax.ShapeDtypeStruct((s, w), BF16),
        scratch_shapes=[pltpu.VMEM((w // GROUP_W, CHUNK, GROUP_W), F32)],
        compiler_params=_cparams(1),
        name="rwkv_chunks",
    )(at, bt, kt, rt, v, wc, bonus, g, gn_gain.reshape(1, -1).astype(F32),
      gn_bias.reshape(1, -1).astype(F32), masks, head)


ROPE_ROWS = 256


def _proj_rope_kernel(pos_ref, invf_ref, x_ref, w_ref, o_ref, cos_ref, sin_ref, *, bm, bn):
    def first_half(rows):
        lane = lax.broadcasted_iota(jnp.int32, (rows, LANES), 1)
        return (lane & (HEAD_DIM - 1)) < (HEAD_DIM // 2)

    @pl.when(pl.program_id(1) == 0)
    def _():
        ang = pos_ref[...].astype(F32) * invf_ref[...]
        cos_ref[...] = jnp.cos(ang)
        sin = jnp.sin(ang)
        sin_ref[...] = jnp.where(first_half(bm), -sin, sin)

    half = first_half(ROPE_ROWS)

    for r in range(bm // ROPE_ROWS):
        rows = slice(r * ROPE_ROWS, (r + 1) * ROPE_ROWS)
        acc = jnp.dot(x_ref[rows, :], w_ref[...], preferred_element_type=F32)
        cos = cos_ref[rows, :]
        sin = sin_ref[rows, :]
        for c in range(bn // LANES):
            t = acc[:, c * LANES:(c + 1) * LANES]
            partner = jnp.where(half, pltpu.roll(t, LANES - HEAD_DIM // 2, axis=1),
                                pltpu.roll(t, HEAD_DIM // 2, axis=1))
            o_ref[rows, c * LANES:(c + 1) * LANES] = (t * cos + partner * sin).astype(o_ref.dtype)


def _proj_rope(xb, w_all, positions, *, bm, bn, n, col0):
    s, k = xb.shape
    assert n % bn == 0 and col0 % bn == 0
    jb = col0 // bn
    half = jnp.arange(0, HEAD_DIM, 2, dtype=F32)
    inv_freq = ROPE_THETA ** (-half / HEAD_DIM)
    invf = jnp.tile(inv_freq, 2 * LANES // HEAD_DIM).reshape(1, LANES)
    return pl.pallas_call(
        functools.partial(_proj_rope_kernel, bm=bm, bn=bn),
        grid=(s // bm, n // bn),
        in_specs=[pl.BlockSpec((bm, 1), lambda i, j: (i, 0)),
                  pl.BlockSpec((1, LANES), lambda i, j: (0, 0)),
                  pl.BlockSpec((bm, k), lambda i, j: (i, 0)),
                  pl.BlockSpec((k, bn), lambda i, j: (0, jb + j))],
        out_specs=pl.BlockSpec((bm, bn), lambda i, j: (i, j)),
        out_shape=jax.ShapeDtypeStruct((s, n), BF16),
        scratch_shapes=[pltpu.VMEM((bm, LANES), F32), pltpu.VMEM((bm, LANES), F32)],
        compiler_params=_cparams(2),
        name="proj_qk_rope",
    )(positions.reshape(s, 1), invf, xb, w_all)


ATT_STRIP = 512


ATT_KB = 512


def _attn_kernel(lam_ref, gain_ref, q_ref, k_ref, vt_ref, o_ref, acc_ref, s0_ref, s1_ref, *, bq):
    i = pl.program_id(1)
    nq = 2 * bq
    kb = ATT_KB
    q = q_ref[...]
    lane = lax.broadcasted_iota(jnp.int32, q.shape, 1)
    zero = jnp.zeros_like(q)
    q_all = jnp.concatenate([jnp.where(lane < HEAD_DIM, q, zero),
                             jnp.where(lane >= HEAD_DIM, q, zero)], axis=0)
    s_slots = (s0_ref, s1_ref)
    all_cols = ((0, nq),)
    late_cols = ((kb, bq), (bq + kb, nq))

    def scores(j, slot):
        start = pl.multiple_of(j * kb, kb)
        s_slots[slot][...] = lax.dot_general(k_ref[pl.ds(start, kb), :], q_all,
                                             (((1,), (1,)), ((), ())), preferred_element_type=F32)

    def consume(j, slot, m, l, groups, key0):
        start = pl.multiple_of(j * kb, kb)
        vtb = vt_ref[:, pl.ds(start, kb)]
        m_parts = {c: m[:, c * ATT_STRIP:(c + 1) * ATT_STRIP] for c in range(nq // ATT_STRIP)}
        l_parts = {c: l[:, c * ATT_STRIP:(c + 1) * ATT_STRIP] for c in range(nq // ATT_STRIP)}
        for lo, hi in groups:
            p_strips, alphas = [], []
            for c in range(lo // ATT_STRIP, hi // ATT_STRIP):
                sl = slice(c * ATT_STRIP, (c + 1) * ATT_STRIP)
                s = s_slots[slot][:, sl]
                q_first = (c * ATT_STRIP) % bq
                if key0 is not None and key0 + kb - 1 > q_first:
                    key = lax.broadcasted_iota(jnp.int32, s.shape, 0) + key0
                    qry = lax.broadcasted_iota(jnp.int32, s.shape, 1) + q_first
                    s = jnp.where(key <= qry, s, -1e30)
                mn = jnp.maximum(m_parts[c], jnp.max(s, axis=0, keepdims=True))
                p = jnp.exp2(s - mn)
                alpha = jnp.exp2(m_parts[c] - mn)
                l_parts[c] = alpha * l_parts[c] + jnp.sum(p, axis=0, keepdims=True)
                m_parts[c] = mn
                alphas.append(alpha)
                p_strips.append(p.astype(BF16))
            p_all = jnp.concatenate(p_strips, axis=1)
            alpha = jnp.concatenate(alphas, axis=1)
            acc_ref[:, lo:hi] = alpha * acc_ref[:, lo:hi] + jnp.dot(vtb, p_all,
                                                                    preferred_element_type=F32)
        order = range(nq // ATT_STRIP)
        return (jnp.concatenate([m_parts[c] for c in order], axis=1),
                jnp.concatenate([l_parts[c] for c in order], axis=1))

    def pair(jj, carry):
        j = 2 * jj
        scores(j + 1, 1)
        m, l = consume(j, 0, carry[0], carry[1], all_cols, None)
        scores(j + 2, 0)
        return consume(j + 1, 1, m, l, all_cols, None)

    acc_ref[...] = jnp.zeros_like(acc_ref)
    scores(0, 0)
    init = (jnp.full((1, nq), -1e30, F32), jnp.zeros((1, nq), F32))
    m, l = lax.fori_loop(0, i, pair, init)
    scores(2 * i + 1, 1)
    m, l = consume(2 * i, 0, m, l, all_cols, 0)
    m, l = consume(2 * i + 1, 1, m, l, late_cols, kb)

    lam_v = lam_ref[...]
    lam = (jnp.exp(jnp.sum(lam_v[0:1] * lam_v[1:2], axis=1, keepdims=True))
           - jnp.exp(jnp.sum(lam_v[2:3] * lam_v[3:4], axis=1, keepdims=True)) + LAMBDA_INIT)
    o = acc_ref[...] * (1.0 / l)
    out = o[:, :bq] - lam * o[:, bq:]
    out = out * lax.rsqrt(jnp.mean(jnp.square(out), axis=0, keepdims=True) + SUBLN_EPS)
    out = out * gain_ref[...] * (1.0 - LAMBDA_INIT)
    o_ref[...] = out.T.astype(o_ref.dtype)


def _attention(qk, vt, lam_vecs, subln_gain, *, bq):
    s = qk.shape[0]
    assert s % bq == 0 and bq == 2 * ATT_KB
    return pl.pallas_call(
        functools.partial(_attn_kernel, bq=bq),
        grid=(DIFF_HEADS, s // bq),
        in_specs=[pl.BlockSpec((4, HEAD_DIM), lambda h, i: (0, 0)),
                  pl.BlockSpec((DIFF_V_DIM, 1), lambda h, i: (0, 0)),
                  pl.BlockSpec((bq, DIFF_V_DIM), lambda h, i: (i, h)),
                  pl.BlockSpec((s, DIFF_V_DIM), lambda h, i: (0, DIFF_HEADS + h)),
                  pl.BlockSpec((DIFF_V_DIM, s), lambda h, i: (h, 0))],
        out_specs=pl.BlockSpec((bq, DIFF_V_DIM), lambda h, i: (i, h)),
        out_shape=jax.ShapeDtypeStruct((s, DIFF_WIDTH), BF16),
        scratch_shapes=[pltpu.VMEM((DIFF_V_DIM, 2 * bq), F32), pltpu.VMEM((ATT_KB, 2 * bq), F32),
                        pltpu.VMEM((ATT_KB, 2 * bq), F32)],
        compiler_params=_cparams(2),
        name="diff_attn",
    )(lam_vecs, subln_gain.reshape(-1, 1).astype(F32), qk, qk, vt)


def _layer_norm(z, gain, bias):
    mean = jnp.mean(z, axis=-1, keepdims=True)
    zc = z - mean
    var = jnp.mean(zc * zc, axis=-1, keepdims=True)
    return zc * lax.rsqrt(var + LN_EPS) * gain + bias


LN_ROWS = 128


def _outproj_kernel(yr_ref, yd_ref, wr_ref, wd_ref, x_ref, gain_ref, bias_ref, o32_ref, o16_ref):
    for r in range(x_ref.shape[0] // LN_ROWS):
        rows = slice(r * LN_ROWS, (r + 1) * LN_ROWS)
        mix = (jnp.dot(yr_ref[rows, :], wr_ref[...], preferred_element_type=F32)
               + jnp.dot(yd_ref[rows, :], wd_ref[...], preferred_element_type=F32))
        y = _layer_norm(DEEPNORM_ALPHA * x_ref[rows, :] + mix, gain_ref[...], bias_ref[...])
        o32_ref[rows, :] = y
        o16_ref[rows, :] = y.astype(BF16)


def _outproj(y_rwkv, y_diff, w_r, w_d, x, gain, bias, *, bm):
    s = x.shape[0]
    const = lambda shape: pl.BlockSpec(shape, lambda i: (0, 0))
    return pl.pallas_call(
        _outproj_kernel,
        grid=(s // bm,),
        in_specs=[pl.BlockSpec((bm, RWKV_WIDTH), lambda i: (i, 0)),
                  pl.BlockSpec((bm, DIFF_WIDTH), lambda i: (i, 0)),
                  const((RWKV_WIDTH, D_MODEL)), const((DIFF_WIDTH, D_MODEL)),
                  pl.BlockSpec((bm, D_MODEL), lambda i: (i, 0)),
                  const((1, D_MODEL)), const((1, D_MODEL))],
        out_specs=[pl.BlockSpec((bm, D_MODEL), lambda i: (i, 0))] * 2,
        out_shape=[jax.ShapeDtypeStruct((s, D_MODEL), F32), jax.ShapeDtypeStruct((s, D_MODEL), BF16)],
        compiler_params=_cparams(1),
        name="outproj_ln",
    )(y_rwkv, y_diff, w_r, w_d, x, gain.reshape(1, -1), bias.reshape(1, -1))


def _ff2_kernel(h_ref, w_ref, x_ref, gain_ref, bias_ref, o_ref):
    kk = pl.program_id(1)
    part = lambda: jnp.dot(h_ref[...], w_ref[...], preferred_element_type=F32)

    @pl.when(kk == 0)
    def _():
        o_ref[...] = part()

    @pl.when((kk > 0) & (kk < pl.num_programs(1) - 1))
    def _():
        o_ref[...] += part()

    @pl.when(kk == pl.num_programs(1) - 1)
    def _():
        for r in range(o_ref.shape[0] // LN_ROWS):
            rows = slice(r * LN_ROWS, (r + 1) * LN_ROWS)
            acc = o_ref[rows, :] + jnp.dot(h_ref[rows, :], w_ref[...], preferred_element_type=F32)
            o_ref[rows, :] = _layer_norm(DEEPNORM_ALPHA * x_ref[rows, :] + acc,
                                         gain_ref[...], bias_ref[...])


def _ff2(h, w, x, gain, bias, *, bm, bk):
    s, kdim = h.shape
    assert s % bm == 0 and kdim % bk == 0 and kdim // bk >= 2
    const = lambda shape: pl.BlockSpec(shape, lambda i, k: (0, 0))
    return pl.pallas_call(
        _ff2_kernel,
        grid=(s // bm, kdim // bk),
        in_specs=[pl.BlockSpec((bm, bk), lambda i, k: (i, k)),
                  pl.BlockSpec((bk, D_MODEL), lambda i, k: (k, 0)),
                  pl.BlockSpec((bm, D_MODEL), lambda i, k: (i, 0)),
                  const((1, D_MODEL)), const((1, D_MODEL))],
        out_specs=pl.BlockSpec((bm, D_MODEL), lambda i, k: (i, 0)),
        out_shape=jax.ShapeDtypeStruct((s, D_MODEL), F32),
        compiler_params=_cparams(2),
        name="ff2_ln",
    )(h, w, x, gain.reshape(1, -1), bias.reshape(1, -1))


def _pad_rows(w, rows):
    return jnp.pad(w, ((0, rows - w.shape[0]), (0, 0)))


def _layer(x, positions, w_in, mu_shift, w0, w_up, a0, a_up, g_up, k_k, k_a, r_k, gn_gain, gn_bias,
           lq1, lk1, lq2, lk2, subln_gain, w_out, ln1_gain, ln1_bias, w_ff1, w_ff2, ln2_gain, ln2_bias,
           *, bm, prep_tb, nc, bq):
    w3 = 3 * RWKV_WIDTH
    rwkv_cols = w3 + DECAY_LORA + AAA_LORA + GATE_LORA

    def pad_cols(a, lo, hi, width):
        return jnp.pad(a[..., lo:hi], [(0, 0)] * (a.ndim - 1) + [(0, width - (hi - lo))])

    def rwkv_layout(a):
        return jnp.concatenate([
            a[..., :w3],
            pad_cols(a, w3, w3 + DECAY_LORA, DECAY_PAD),
            pad_cols(a, w3 + DECAY_LORA, w3 + DECAY_LORA + AAA_LORA, AAA_PAD),
            pad_cols(a, w3 + DECAY_LORA + AAA_LORA, rwkv_cols, GATE_PAD)], axis=-1)

    w_all = _relayout_w_in(w_in.T, bk=256)
    proj_r, xb = _proj_cast(x, w_all, bm=bm, bn=RWKV_PCOLS // 2, n=RWKV_PCOLS)
    vt = _matmul(xb, w_all, bm=bm, bn=PROJ_BN, n=DIFF_WIDTH, col0=V_COL0, out_dtype=BF16,
                 transpose_out=True, name="proj_vt")

    at, bt, kt, rt, v_r, wc, bonus, gate = _rwkv_prep(
        proj_r, rwkv_layout(mu_shift), w0,
        _pad_rows(w_up, DECAY_PAD).astype(BF16), a0, _pad_rows(a_up, AAA_PAD).astype(BF16),
        _pad_rows(g_up, GATE_PAD).astype(BF16), k_k, k_a, r_k.reshape(-1), tb=prep_tb)
    y_rwkv = _rwkv_chunks(at, bt, kt, rt, v_r, wc, bonus, gate, gn_gain, gn_bias, nc=nc)

    qk = _proj_rope(xb, w_all, positions, bm=bm, bn=PROJ_BN, n=2 * DIFF_WIDTH, col0=QK_COL0)
    lam_vecs = jnp.stack([lq1, lk1, lq2, lk2]).astype(F32)
    y_diff = _attention(qk, vt, lam_vecs, subln_gain, bq=bq)

    wo = w_out.astype(BF16)
    x1, x1b = _outproj(y_rwkv, y_diff, wo[:RWKV_WIDTH], wo[RWKV_WIDTH:], x, ln1_gain, ln1_bias, bm=512)
    h, w_ff2b = _ff1(x1b, w_ff1, w_ff2, bm=bm, bn=1024)
    return _ff2(h, w_ff2b, x1, ln2_gain, ln2_bias, bm=1024, bk=1024)


def kernel(x, positions, w_in, mu_shift, rwkv_w0, rwkv_w_up, rwkv_a0, rwkv_a_up, rwkv_g_up, rwkv_k_k,
           rwkv_k_a, rwkv_r_k, rwkv_gn_gain, rwkv_gn_bias, diff_lambda_q1, diff_lambda_k1,
           diff_lambda_q2, diff_lambda_k2, diff_subln_gain, w_out, ln1_gain, ln1_bias, w_ff1, w_ff2,
           ln2_gain, ln2_bias):
    batch, seq, _ = x.shape
    assert batch == 1 and w_in.shape[0] == 1
    out = _layer(
        x[0], positions[0], w_in[0], mu_shift[0], rwkv_w0[0], rwkv_w_up[0], rwkv_a0[0], rwkv_a_up[0],
        rwkv_g_up[0], rwkv_k_k[0], rwkv_k_a[0], rwkv_r_k[0], rwkv_gn_gain[0], rwkv_gn_bias[0],
        diff_lambda_q1[0], diff_lambda_k1[0], diff_lambda_q2[0], diff_lambda_k2[0],
        diff_subln_gain[0], w_out[0], ln1_gain[0], ln1_bias[0], w_ff1[0], w_ff2[0], ln2_gain[0],
        ln2_bias[0], bm=1024, prep_tb=256, nc=8, bq=1024)
    return out[None]
```

```python
import functools
import math

import jax
import jax.numpy as jnp
from jax import lax
from jax.experimental import pallas as pl
from jax.experimental.pallas import tpu as pltpu

F32 = jnp.float32
BF16 = jnp.bfloat16

D_MODEL = 2048
RWKV_WIDTH = 1024
HEAD_DIM = 64
DECAY_LORA = 64
AAA_LORA = 64
GATE_LORA = 160
DIFF_WIDTH = 1024
DIFF_V_DIM = 128
DIFF_HEADS = 8
ROPE_THETA = 10000.0
LN_EPS = 1e-5
RWKV_GN_EPS = 64e-5
SUBLN_EPS = 1e-5
DEEPNORM_ALPHA = 2.0 ** 0.25
LAMBDA_INIT = 0.8 - 0.6 * math.exp(0.0)
LOG2_E = math.log2(math.e)

LANES = 128
SUBLANES = 8
VMEM_LIMIT_BYTES = 56 * 1024 * 1024

DECAY_PAD = 128
AAA_PAD = 128
GATE_PAD = 256
RWKV_PCOLS = 3 * RWKV_WIDTH + DECAY_PAD + AAA_PAD + GATE_PAD
PROJ_BN = 1024
QK_COL0 = -(-RWKV_PCOLS // PROJ_BN) * PROJ_BN
V_COL0 = QK_COL0 + 2 * DIFF_WIDTH
CHUNK = 64
GROUP_HEADS = 4
GROUP_W = GROUP_HEADS * HEAD_DIM


def _cparams(n_axes):
    return pltpu.CompilerParams(
        dimension_semantics=("arbitrary",) * n_axes,
        vmem_limit_bytes=VMEM_LIMIT_BYTES,
    )


def _dot(a, b):
    return jnp.dot(a.astype(BF16), b.astype(BF16), preferred_element_type=F32)


def _split3(x):
    h1 = x.astype(BF16)
    r1 = x - h1.astype(F32)
    h2 = r1.astype(BF16)
    h3 = (r1 - h2.astype(F32)).astype(BF16)
    return h1, h2, h3


def _split2(x):
    h1 = x.astype(BF16)
    h2 = (x - h1.astype(F32)).astype(BF16)
    return h1, h2


def _mm_kernel(a_ref, b_ref, o_ref, *, transpose_out):
    acc = jnp.dot(a_ref[...], b_ref[...], preferred_element_type=F32)
    if transpose_out:
        acc = acc.T
    o_ref[...] = acc.astype(o_ref.dtype)


def _matmul(a, b, *, bm, bn, n, col0=0, out_dtype, transpose_out=False, name):
    m, k = a.shape
    assert m % bm == 0 and n % bn == 0 and col0 % bn == 0
    jb = col0 // bn
    if transpose_out:
        out_spec = pl.BlockSpec((bn, bm), lambda i, j: (j, i))
        out_shape = jax.ShapeDtypeStruct((n, m), out_dtype)
    else:
        out_spec = pl.BlockSpec((bm, bn), lambda i, j: (i, j))
        out_shape = jax.ShapeDtypeStruct((m, n), out_dtype)
    return pl.pallas_call(
        functools.partial(_mm_kernel, transpose_out=transpose_out),
        grid=(m // bm, n // bn),
        in_specs=[pl.BlockSpec((bm, k), lambda i, j: (i, 0)),
                  pl.BlockSpec((k, bn), lambda i, j: (0, jb + j))],
        out_specs=out_spec,
        out_shape=out_shape,
        compiler_params=_cparams(2),
        name=name,
    )(a, b)


def _proj_cast_kernel(x_ref, w_ref, o_ref, xb_ref):
    @pl.when(pl.program_id(1) == 0)
    def _():
        xb_ref[...] = x_ref[...].astype(BF16)

    o_ref[...] = jnp.dot(xb_ref[...], w_ref[...], preferred_element_type=F32)


def _proj_cast(x, w_all, *, bm, bn, n):
    m, k = x.shape
    assert m % bm == 0 and n % bn == 0
    return pl.pallas_call(
        _proj_cast_kernel,
        grid=(m // bm, n // bn),
        in_specs=[pl.BlockSpec((bm, k), lambda i, j: (i, 0)),
                  pl.BlockSpec((k, bn), lambda i, j: (0, j))],
        out_specs=[pl.BlockSpec((bm, bn), lambda i, j: (i, j)),
                   pl.BlockSpec((bm, k), lambda i, j: (i, 0))],
        out_shape=[jax.ShapeDtypeStruct((m, n), F32), jax.ShapeDtypeStruct((m, k), BF16)],
        compiler_params=_cparams(2),
        name="proj_rwkv",
    )(x, w_all)


def _ff1_kernel(x_ref, w_ref, w2_ref, o_ref, w2b_ref, wb_ref):
    @pl.when(pl.program_id(1) == 0)
    def _():
        wb_ref[...] = w_ref[...].astype(BF16)

    acc = jnp.dot(x_ref[...], wb_ref[...], preferred_element_type=F32)
    o_ref[...] = jnp.square(jnp.maximum(acc, 0.0)).astype(o_ref.dtype)
    w2b_ref[...] = w2_ref[...].astype(BF16)


def _ff1(x, w, w2, *, bm, bn):
    m, k = x.shape
    _, n = w.shape
    assert m % bm == 0 and n % bn == 0
    steps = (n // bn) * (m // bm)
    rows2, cols2 = w2.shape
    assert rows2 % steps == 0 and (rows2 // steps) % (2 * SUBLANES) == 0
    slab = pl.BlockSpec((rows2 // steps, cols2), lambda j, i: (j * (m // bm) + i, 0))
    return pl.pallas_call(
        _ff1_kernel,
        grid=(n // bn, m // bm),
        in_specs=[pl.BlockSpec((bm, k), lambda j, i: (i, 0)),
                  pl.BlockSpec((k, bn), lambda j, i: (0, j)),
                  slab],
        out_specs=[pl.BlockSpec((bm, bn), lambda j, i: (i, j)), slab],
        out_shape=[jax.ShapeDtypeStruct((m, n), BF16), jax.ShapeDtypeStruct(w2.shape, BF16)],
        scratch_shapes=[pltpu.VMEM((k, bn), BF16)],
        compiler_params=_cparams(2),
        name="ff1",
    )(x, w, w2)


def _relayout_kernel(wt_ref, o_ref):
    cols = wt_ref.shape[1]
    w3 = 3 * RWKV_WIDTH
    a0 = w3 + DECAY_LORA
    g0 = a0 + AAA_LORA
    q0 = g0 + GATE_LORA
    zeros = lambda n: jnp.zeros((n, cols), F32)
    pieces = [wt_ref[:a0, :], zeros(DECAY_PAD - DECAY_LORA),
              wt_ref[a0:g0, :], zeros(AAA_PAD - AAA_LORA),
              wt_ref[g0:q0, :], zeros(GATE_PAD - GATE_LORA + QK_COL0 - RWKV_PCOLS),
              wt_ref[q0:q0 + DIFF_WIDTH, :] * (HEAD_DIM ** -0.5 * LOG2_E),
              wt_ref[q0 + DIFF_WIDTH:, :]]
    for c0 in range(0, o_ref.shape[1], RELAYOUT_COLS):
        rows = slice(c0, c0 + RELAYOUT_COLS)
        tile = jnp.concatenate(_row_window(pieces, c0, c0 + RELAYOUT_COLS), axis=0)
        o_ref[:, rows] = tile.T.astype(o_ref.dtype)


def _row_window(pieces, lo, hi):
    out, start = [], 0
    for p in pieces:
        stop = start + p.shape[0]
        a, b = max(lo, start), min(hi, stop)
        if a < b:
            out.append(p[a - start:b - start])
        start = stop
    return out


RELAYOUT_COLS = 512


def _relayout_w_in(w_in_t, *, bk):
    n, k = w_in_t.shape
    n_out = V_COL0 + DIFF_WIDTH
    assert n == 3 * RWKV_WIDTH + DECAY_LORA + AAA_LORA + GATE_LORA + 3 * DIFF_WIDTH
    assert n_out % RELAYOUT_COLS == 0
    return pl.pallas_call(
        _relayout_kernel,
        grid=(k // bk,),
        in_specs=[pl.BlockSpec((n, bk), lambda i: (0, i))],
        out_specs=pl.BlockSpec((bk, n_out), lambda i: (i, 0)),
        out_shape=jax.ShapeDtypeStruct((k, n_out), BF16),
        compiler_params=_cparams(1),
        name="relayout_w_in",
    )(w_in_t)


def _rwkv_prep_kernel(p_ref, mu_ref, w0_ref, wup_ref, a0_ref, aup_ref, gup_ref, kk_ref, ka_ref,
                      rk_ref, segin_ref, segout_ref, tri_ref,
                      at_ref, bt_ref, kt_ref, rt_ref, v_ref, wc_ref, bonus_ref, g_ref,
                      carry_ref, *, tb):
    i = pl.program_id(0)

    @pl.when(i == 0)
    def _():
        carry_ref[...] = jnp.zeros_like(carry_ref)

    def mixed(lo, hi):
        p = p_ref[:, lo:hi]
        prev_last = carry_ref[0:1, lo:hi]
        shifted = pltpu.roll(p, 1, axis=0)
        row = lax.broadcasted_iota(jnp.int32, p.shape, 0)
        prev = jnp.where(row == 0, prev_last, shifted)
        carry_ref[0:1, lo:hi] = p[tb - 1:tb, :]
        return p + (prev - p) * mu_ref[:, lo:hi]

    w3 = 3 * RWKV_WIDTH
    r = mixed(0, RWKV_WIDTH)
    k = mixed(RWKV_WIDTH, 2 * RWKV_WIDTH)
    v = mixed(2 * RWKV_WIDTH, w3)
    w_d = mixed(w3, w3 + DECAY_PAD)
    a_d = mixed(w3 + DECAY_PAD, w3 + DECAY_PAD + AAA_PAD)
    g_d = mixed(w3 + DECAY_PAD + AAA_PAD, RWKV_PCOLS)

    z = w0_ref[...] + _dot(jnp.tanh(w_d), wup_ref[...])
    lw = -math.exp(-0.5) * jax.nn.sigmoid(z)
    a = jax.nn.sigmoid(a0_ref[...] + _dot(a_d, aup_ref[...]))
    g_ref[...] = _dot(jax.nn.sigmoid(g_d), gup_ref[...])

    def segsum(x):
        h1, h2 = _split2(jnp.dot(x.astype(BF16), segin_ref[...], preferred_element_type=F32))
        return (jnp.dot(h1, segout_ref[...], preferred_element_type=F32)
                + jnp.dot(h2, segout_ref[...], preferred_element_type=F32))

    kk0 = k * kk_ref[...]
    kk = kk0 * lax.rsqrt(jnp.maximum(segsum(kk0 * kk0), 1e-24))
    k2 = k * (1.0 + (a - 1.0) * ka_ref[...])
    bonus_ref[...] = segsum(r * k2 * rk_ref[...]) * v

    tri = tri_ref[...]
    l1, l2, l3 = _split3(lw)
    cs = (jnp.dot(tri, l1, preferred_element_type=F32)
          + jnp.dot(tri, l2, preferred_element_type=F32)
          + jnp.dot(tri, l3, preferred_element_type=F32))
    cl = cs[:tb]
    tot = cs[tb:]
    e_neg = jnp.exp(-cl)
    at_ref[...] = (-kk * jnp.exp(cl - lw)).astype(BF16)
    bt_ref[...] = (kk * a * e_neg).astype(BF16)
    kt_ref[...] = (k2 * e_neg).astype(BF16)
    rt_ref[...] = (r * jnp.exp(cl)).astype(BF16)
    v_ref[...] = v.astype(BF16)
    for c in range(tb // CHUNK):
        wc_ref[c] = jnp.exp(tot[c * CHUNK:c * CHUNK + 1, :])


def _rwkv_prep(proj_r, mu, w0, wup, a0, aup, gup, k_k, k_a, r_k, *, tb):
    s = proj_r.shape[0]
    assert s % tb == 0 and tb % CHUNK == 0
    w = RWKV_WIDTH
    lane = jnp.arange(w)
    seg_in = (lane[:, None] // HEAD_DIM == jnp.arange(LANES)[None, :]).astype(BF16)
    t = jnp.arange(tb)
    same = t[:, None] // CHUNK == t[None, :] // CHUNK
    tri = jnp.concatenate([(same & (t[:, None] >= t[None, :])), same], axis=0).astype(BF16)
    row = lambda x: x.reshape(1, -1).astype(F32)
    const = lambda shape: pl.BlockSpec(shape, lambda i: (0,) * len(shape))
    tile = lambda width: pl.BlockSpec((tb, width), lambda i: (i, 0))
    outs = pl.pallas_call(
        functools.partial(_rwkv_prep_kernel, tb=tb),
        grid=(s // tb,),
        in_specs=[tile(RWKV_PCOLS), const((1, RWKV_PCOLS)), const((1, w)), const((DECAY_PAD, w)),
                  const((1, w)), const((AAA_PAD, w)), const((GATE_PAD, w)), const((1, w)),
                  const((1, w)), const((1, w)), const((w, LANES)), const((LANES, w)),
                  const((2 * tb, tb))],
        out_specs=[tile(w), tile(w), tile(w), tile(w), tile(w),
                   pl.BlockSpec((tb // CHUNK, 1, w), lambda i: (i, 0, 0)),
                   tile(w), tile(w)],
        out_shape=[jax.ShapeDtypeStruct((s, w), BF16)] * 5
        + [jax.ShapeDtypeStruct((s // CHUNK, 1, w), F32),
           jax.ShapeDtypeStruct((s, w), F32), jax.ShapeDtypeStruct((s, w), F32)],
        scratch_shapes=[pltpu.VMEM((SUBLANES, RWKV_PCOLS), F32)],
        compiler_params=_cparams(1),
        name="rwkv_prep",
    )(proj_r, row(mu), row(w0), wup, row(a0), aup, gup, row(k_k), row(k_a), row(r_k), seg_in, seg_in.T, tri)
    return outs


def _bdot(a, b):
    return jnp.dot(a, b, preferred_element_type=F32)


def _bdot_nt(a, b):
    return lax.dot_general(a, b, (((1,), (1,)), ((), ())), preferred_element_type=F32)


(MASK_STRICT, MASK_INCL, MASK_EYE, MASK_BLK8, MASK_OFF16, MASK_OFF32, MASK_OFF64) = range(7)


def _chunk_masks():
    a = jnp.arange(CHUNK)[:, None]
    b = jnp.arange(GROUP_W)[None, :] % CHUNK
    blk = lambda size: (a // size) == (b // size)
    masks = [a > b, a >= b, a == b, blk(8),
             blk(16) & ~blk(8), blk(32) & ~blk(16), blk(64) & ~blk(32)]
    return jnp.stack(masks).astype(F32)


def _rwkv_chunk_kernel(at_ref, bt_ref, kt_ref, rt_ref, v_ref, wc_ref, bonus_ref, g_ref,
                       gain_ref, bias_ref, mask_ref, head_ref, o_ref, state_ref, *, nc):
    @pl.when(pl.program_id(0) == 0)
    def _():
        state_ref[...] = jnp.zeros_like(state_ref)

    n = GROUP_W
    ng = RWKV_WIDTH // GROUP_W
    items = [(c, g) for c in range(nc) for g in range(ng)]
    lane_head = lax.broadcasted_iota(jnp.int32, (CHUNK, n), 1) >> 6

    def tile(ref, c, g):
        return ref[c * CHUNK:(c + 1) * CHUNK, g * n:(g + 1) * n]

    bf = lambda x: x.astype(BF16)

    def expand(x):
        x = bf(x)
        zero = jnp.zeros_like(x)
        return jnp.concatenate([jnp.where(lane_head == h, x, zero) for h in range(GROUP_HEADS)],
                               axis=0)

    def collapse(x):
        return (x[0:CHUNK] + x[CHUNK:2 * CHUNK] + x[2 * CHUNK:3 * CHUNK] + x[3 * CHUNK:4 * CHUNK])

    mask = lambda k: mask_ref[k]
    each = lambda f: [f(*it) for it in items]
    zipped = lambda f, *lists: [f(*args) for args in zip(*lists)]
    wdot = lambda w, e: _bdot(bf(w), e)

    at = each(lambda c, g: tile(at_ref, c, g))
    bt = each(lambda c, g: tile(bt_ref, c, g))
    kt = each(lambda c, g: tile(kt_ref, c, g))
    rt = each(lambda c, g: tile(rt_ref, c, g))
    wc = each(lambda c, g: wc_ref[c][:, g * n:(g + 1) * n])
    ae = [expand(x) for x in at]
    be = [expand(x) for x in bt]
    ke = [expand(x) for x in kt]
    ve = each(lambda c, g: expand(tile(v_ref, c, g)))
    bhe = zipped(lambda x, w: expand(x.astype(F32) * w), bt, wc)
    khe = zipped(lambda x, w: expand(x.astype(F32) * w), kt, wc)

    sc = zipped(lambda a, r, b, k: _bdot_nt(jnp.concatenate([a, r], axis=0),
                                            jnp.concatenate([b, k], axis=0)), at, rt, be, ke)
    lab = [x[:CHUNK, :n] * mask(MASK_STRICT) for x in sc]
    lak = [bf(x[:CHUNK, n:] * mask(MASK_STRICT)) for x in sc]
    arb = [bf(x[CHUNK:, :n] * mask(MASK_INCL)) for x in sc]
    ark = [bf(x[CHUNK:, n:] * mask(MASK_INCL)) for x in sc]
    bht = [collapse(x.T) for x in bhe]
    kht = [collapse(x.T) for x in khe]
    xv = zipped(lambda a, b, c, v: _bdot(jnp.concatenate([a, b, c], axis=0), v), lak, ark, kht, ve)
    lv = [x[:CHUNK] for x in xv]
    ark_v = [x[CHUNK:2 * CHUNK] for x in xv]
    kht_v = [x[2 * CHUNK:] for x in xv]

    pw = [x * mask(MASK_BLK8) for x in lab]
    minv = [mask(MASK_EYE) + x for x in pw]
    for _ in range(2):
        pw = zipped(lambda x: wdot(x, expand(x)), pw)
        minv = zipped(lambda x, m: m + wdot(x, expand(m)), pw, minv)
    for k in (MASK_OFF16, MASK_OFF32, MASK_OFF64):
        off = [expand(x * mask(k)) for x in lab]
        tmp = zipped(wdot, minv, off)
        minv = zipped(lambda m, t: m + wdot(t, expand(m)), minv, tmp)
    mb = [bf(m) for m in minv]

    p = [expand(_bdot(m, a)) for m, a in zip(mb, ae)]
    q = [expand(_bdot(m, expand(x))) for m, x in zip(mb, lv)]
    ab = zipped(lambda a, b: jnp.concatenate([a, b], axis=0), arb, bht)
    xp = zipped(_bdot, ab, p)
    xq = zipped(_bdot, ab, q)
    g_mat = zipped(lambda r, x: r.astype(F32) + x[:CHUNK], rt, xp)
    phi = zipped(lambda w, x: mask(MASK_EYE) * w + x[CHUNK:], wc, xp)
    gp = zipped(lambda g, f: bf(jnp.concatenate([g, f], axis=0)), g_mat, phi)
    y0 = zipped(lambda x, v: x[:CHUNK] + v, xq, ark_v)
    psi = zipped(lambda x, v: x[CHUNK:] + v, xq, kht_v)

    head = head_ref[...]

    def head_mean(x):
        h1, h2 = _split2(x)
        both = _bdot(jnp.concatenate([h1, h2], axis=0), head)
        return (both[:x.shape[0]] + both[x.shape[0]:]) * (1.0 / HEAD_DIM)

    state = [state_ref[g] for g in range(ng)]
    for c in range(nc):
        idx = [c * ng + g for g in range(ng)]
        se = [expand(s) for s in state]
        xs = [_bdot(gp[i], se[g]) for g, i in enumerate(idx)]
        state = [x[CHUNK:] + psi[i] for x, i in zip(xs, idx)]
        y = jnp.concatenate([x[:CHUNK] + y0[i] for x, i in zip(xs, idx)], axis=0)
        yc = y - head_mean(y)
        yn = yc * lax.rsqrt(head_mean(yc * yc) + RWKV_GN_EPS)
        rows = slice(c * CHUNK, (c + 1) * CHUNK)
        for g in range(ng):
            cols = slice(g * n, (g + 1) * n)
            out = (yn[g * CHUNK:(g + 1) * CHUNK] * gain_ref[:, cols] + bias_ref[:, cols]
                   + bonus_ref[rows, cols]) * g_ref[rows, cols]
            o_ref[rows, cols] = out.astype(o_ref.dtype)
    for g in range(ng):
        state_ref[g] = state[g]


def _rwkv_chunks(at, bt, kt, rt, v, wc, bonus, g, gn_gain, gn_bias, *, nc):
    s, w = at.shape
    tb = nc * CHUNK
    assert s % tb == 0 and w % GROUP_W == 0
    tile = pl.BlockSpec((tb, w), lambda ti: (ti, 0))
    rowspec = pl.BlockSpec((1, w), lambda ti: (0, 0))
    masks = _chunk_masks()
    lane = jnp.arange(GROUP_W)
    head = (lane[:, None] // HEAD_DIM == lane[None, :] // HEAD_DIM).astype(BF16)
    return pl.pallas_call(
        functools.partial(_rwkv_chunk_kernel, nc=nc),
        grid=(s // tb,),
        in_specs=[tile, tile, tile, tile, tile,
                  pl.BlockSpec((nc, 1, w), lambda ti: (ti, 0, 0)),
                  tile, tile, rowspec, rowspec,
                  pl.BlockSpec(masks.shape, lambda ti: (0, 0, 0)),
                  pl.BlockSpec(head.shape, lambda ti: (0, 0))],
        out_specs=tile,
        out_shape=jax.ShapeDtypeStruct((s, w), BF16),
        scratch_shapes=[pltpu.VMEM((w // GROUP_W, CHUNK, GROUP_W), F32)],
        compiler_params=_cparams(1),
        name="rwkv_chunks",
    )(at, bt, kt, rt, v, wc, bonus, g, gn_gain.reshape(1, -1).astype(F32),
      gn_bias.reshape(1, -1).astype(F32), masks, head)


ROPE_ROWS = 256


def _proj_rope_kernel(pos_ref, invf_ref, x_ref, w_ref, o_ref, cos_ref, sin_ref, *, bm, bn):
    def first_half(rows):
        lane = lax.broadcasted_iota(jnp.int32, (rows, LANES), 1)
        return (lane & (HEAD_DIM - 1)) < (HEAD_DIM // 2)

    @pl.when(pl.program_id(1) == 0)
    def _():
        ang = pos_ref[...].astype(F32) * invf_ref[...]
        cos_ref[...] = jnp.cos(ang)
        sin = jnp.sin(ang)
        sin_ref[...] = jnp.where(first_half(bm), -sin, sin)

    half = first_half(ROPE_ROWS)

    for r in range(bm // ROPE_ROWS):
        rows = slice(r * ROPE_ROWS, (r + 1) * ROPE_ROWS)
        acc = jnp.dot(x_ref[rows, :], w_ref[...], preferred_element_type=F32)
        cos = cos_ref[rows, :]
        sin = sin_ref[rows, :]
        for c in range(bn // LANES):
            t = acc[:, c * LANES:(c + 1) * LANES]
            partner = jnp.where(half, pltpu.roll(t, LANES - HEAD_DIM // 2, axis=1),
                                pltpu.roll(t, HEAD_DIM // 2, axis=1))
            o_ref[rows, c * LANES:(c + 1) * LANES] = (t * cos + partner * sin).astype(o_ref.dtype)


def _proj_rope(xb, w_all, positions, *, bm, bn, n, col0):
    s, k = xb.shape
    assert n % bn == 0 and col0 % bn == 0
    jb = col0 // bn
    half = jnp.arange(0, HEAD_DIM, 2, dtype=F32)
    inv_freq = ROPE_THETA ** (-half / HEAD_DIM)
    invf = jnp.tile(inv_freq, 2 * LANES // HEAD_DIM).reshape(1, LANES)
    return pl.pallas_call(
        functools.partial(_proj_rope_kernel, bm=bm, bn=bn),
        grid=(s // bm, n // bn),
        in_specs=[pl.BlockSpec((bm, 1), lambda i, j: (i, 0)),
                  pl.BlockSpec((1, LANES), lambda i, j: (0, 0)),
                  pl.BlockSpec((bm, k), lambda i, j: (i, 0)),
                  pl.BlockSpec((k, bn), lambda i, j: (0, jb + j))],
        out_specs=pl.BlockSpec((bm, bn), lambda i, j: (i, j)),
        out_shape=jax.ShapeDtypeStruct((s, n), BF16),
        scratch_shapes=[pltpu.VMEM((bm, LANES), F32), pltpu.VMEM((bm, LANES), F32)],
        compiler_params=_cparams(2),
        name="proj_qk_rope",
    )(positions.reshape(s, 1), invf, xb, w_all)


ATT_STRIP = 512


ATT_KB = 512


def _attn_kernel(lam_ref, gain_ref, q_ref, k_ref, vt_ref, o_ref, acc_ref, s0_ref, s1_ref, *, bq):
    i = pl.program_id(1)
    nq = 2 * bq
    kb = ATT_KB
    q = q_ref[...]
    lane = lax.broadcasted_iota(jnp.int32, q.shape, 1)
    zero = jnp.zeros_like(q)
    q_all = jnp.concatenate([jnp.where(lane < HEAD_DIM, q, zero),
                             jnp.where(lane >= HEAD_DIM, q, zero)], axis=0)
    s_slots = (s0_ref, s1_ref)
    all_cols = ((0, nq),)
    late_cols = ((kb, bq), (bq + kb, nq))

    def scores(j, slot):
        start = pl.multiple_of(j * kb, kb)
        s_slots[slot][...] = lax.dot_general(k_ref[pl.ds(start, kb), :], q_all,
                                             (((1,), (1,)), ((), ())), preferred_element_type=F32)

    def consume(j, slot, m, l, groups, key0):
        start = pl.multiple_of(j * kb, kb)
        vtb = vt_ref[:, pl.ds(start, kb)]
        m_parts = {c: m[:, c * ATT_STRIP:(c + 1) * ATT_STRIP] for c in range(nq // ATT_STRIP)}
        l_parts = {c: l[:, c * ATT_STRIP:(c + 1) * ATT_STRIP] for c in range(nq // ATT_STRIP)}
        for lo, hi in groups:
            p_strips, alphas = [], []
            for c in range(lo // ATT_STRIP, hi // ATT_STRIP):
                sl = slice(c * ATT_STRIP, (c + 1) * ATT_STRIP)
                s = s_slots[slot][:, sl]
                q_first = (c * ATT_STRIP) % bq
                if key0 is not None and key0 + kb - 1 > q_first:
                    key = lax.broadcasted_iota(jnp.int32, s.shape, 0) + key0
                    qry = lax.broadcasted_iota(jnp.int32, s.shape, 1) + q_first
                    s = jnp.where(key <= qry, s, -1e30)
                mn = jnp.maximum(m_parts[c], jnp.max(s, axis=0, keepdims=True))
                p = jnp.exp2(s - mn)
                alpha = jnp.exp2(m_parts[c] - mn)
                l_parts[c] = alpha * l_parts[c] + jnp.sum(p, axis=0, keepdims=True)
                m_parts[c] = mn
                alphas.append(alpha)
                p_strips.append(p.astype(BF16))
            p_all = jnp.concatenate(p_strips, axis=1)
            alpha = jnp.concatenate(alphas, axis=1)
            acc_ref[:, lo:hi] = alpha * acc_ref[:, lo:hi] + jnp.dot(vtb, p_all,
                                                                    preferred_element_type=F32)
        order = range(nq // ATT_STRIP)
        return (jnp.concatenate([m_parts[c] for c in order], axis=1),
                jnp.concatenate([l_parts[c] for c in order], axis=1))

    def pair(jj, carry):
        j = 2 * jj
        scores(j + 1, 1)
        m, l = consume(j, 0, carry[0], carry[1], all_cols, None)
        scores(j + 2, 0)
        return consume(j + 1, 1, m, l, all_cols, None)

    acc_ref[...] = jnp.zeros_like(acc_ref)
    scores(0, 0)
    init = (jnp.full((1, nq), -1e30, F32), jnp.zeros((1, nq), F32))
    m, l = lax.fori_loop(0, i, pair, init)
    scores(2 * i + 1, 1)
    m, l = consume(2 * i, 0, m, l, all_cols, 0)
    m, l = consume(2 * i + 1, 1, m, l, late_cols, kb)

    lam_v = lam_ref[...]
    lam = (jnp.exp(jnp.sum(lam_v[0:1] * lam_v[1:2], axis=1, keepdims=True))
           - jnp.exp(jnp.sum(lam_v[2:3] * lam_v[3:4], axis=1, keepdims=True)) + LAMBDA_INIT)
    o = acc_ref[...] * (1.0 / l)
    out = o[:, :bq] - lam * o[:, bq:]
    out = out * lax.rsqrt(jnp.mean(jnp.square(out), axis=0, keepdims=True) + SUBLN_EPS)
    out = out * gain_ref[...] * (1.0 - LAMBDA_INIT)
    o_ref[...] = out.T.astype(o_ref.dtype)


def _attention(qk, vt, lam_vecs, subln_gain, *, bq):
    s = qk.shape[0]
    assert s % bq == 0 and bq == 2 * ATT_KB
    return pl.pallas_call(
        functools.partial(_attn_kernel, bq=bq),
        grid=(DIFF_HEADS, s // bq),
        in_specs=[pl.BlockSpec((4, HEAD_DIM), lambda h, i: (0, 0)),
                  pl.BlockSpec((DIFF_V_DIM, 1), lambda h, i: (0, 0)),
                  pl.BlockSpec((bq, DIFF_V_DIM), lambda h, i: (i, h)),
                  pl.BlockSpec((s, DIFF_V_DIM), lambda h, i: (0, DIFF_HEADS + h)),
                  pl.BlockSpec((DIFF_V_DIM, s), lambda h, i: (h, 0))],
        out_specs=pl.BlockSpec((bq, DIFF_V_DIM), lambda h, i: (i, h)),
        out_shape=jax.ShapeDtypeStruct((s, DIFF_WIDTH), BF16),
        scratch_shapes=[pltpu.VMEM((DIFF_V_DIM, 2 * bq), F32), pltpu.VMEM((ATT_KB, 2 * bq), F32),
                        pltpu.VMEM((ATT_KB, 2 * bq), F32)],
        compiler_params=_cparams(2),
        name="diff_attn",
    )(lam_vecs, subln_gain.reshape(-1, 1).astype(F32), qk, qk, vt)


def _layer_norm(z, gain, bias):
    mean = jnp.mean(z, axis=-1, keepdims=True)
    zc = z - mean
    var = jnp.mean(zc * zc, axis=-1, keepdims=True)
    return zc * lax.rsqrt(var + LN_EPS) * gain + bias


LN_ROWS = 128


def _outproj_kernel(yr_ref, yd_ref, wr_ref, wd_ref, x_ref, gain_ref, bias_ref, o32_ref, o16_ref):
    for r in range(x_ref.shape[0] // LN_ROWS):
        rows = slice(r * LN_ROWS, (r + 1) * LN_ROWS)
        mix = (jnp.dot(yr_ref[rows, :], wr_ref[...], preferred_element_type=F32)
               + jnp.dot(yd_ref[rows, :], wd_ref[...], preferred_element_type=F32))
        y = _layer_norm(DEEPNORM_ALPHA * x_ref[rows, :] + mix, gain_ref[...], bias_ref[...])
        o32_ref[rows, :] = y
        o16_ref[rows, :] = y.astype(BF16)


def _outproj(y_rwkv, y_diff, w_r, w_d, x, gain, bias, *, bm):
    s = x.shape[0]
    const = lambda shape: pl.BlockSpec(shape, lambda i: (0, 0))
    return pl.pallas_call(
        _outproj_kernel,
        grid=(s // bm,),
        in_specs=[pl.BlockSpec((bm, RWKV_WIDTH), lambda i: (i, 0)),
                  pl.BlockSpec((bm, DIFF_WIDTH), lambda i: (i, 0)),
                  const((RWKV_WIDTH, D_MODEL)), const((DIFF_WIDTH, D_MODEL)),
                  pl.BlockSpec((bm, D_MODEL), lambda i: (i, 0)),
                  const((1, D_MODEL)), const((1, D_MODEL))],
        out_specs=[pl.BlockSpec((bm, D_MODEL), lambda i: (i, 0))] * 2,
        out_shape=[jax.ShapeDtypeStruct((s, D_MODEL), F32), jax.ShapeDtypeStruct((s, D_MODEL), BF16)],
        compiler_params=_cparams(1),
        name="outproj_ln",
    )(y_rwkv, y_diff, w_r, w_d, x, gain.reshape(1, -1), bias.reshape(1, -1))


def _ff2_kernel(h_ref, w_ref, x_ref, gain_ref, bias_ref, o_ref):
    kk = pl.program_id(1)
    part = lambda: jnp.dot(h_ref[...], w_ref[...], preferred_element_type=F32)

    @pl.when(kk == 0)
    def _():
        o_ref[...] = part()

    @pl.when((kk > 0) & (kk < pl.num_programs(1) - 1))
    def _():
        o_ref[...] += part()

    @pl.when(kk == pl.num_programs(1) - 1)
    def _():
        for r in range(o_ref.shape[0] // LN_ROWS):
            rows = slice(r * LN_ROWS, (r + 1) * LN_ROWS)
            acc = o_ref[rows, :] + jnp.dot(h_ref[rows, :], w_ref[...], preferred_element_type=F32)
            o_ref[rows, :] = _layer_norm(DEEPNORM_ALPHA * x_ref[rows, :] + acc,
                                         gain_ref[...], bias_ref[...])


def _ff2(h, w, x, gain, bias, *, bm, bk):
    s, kdim = h.shape
    assert s % bm == 0 and kdim % bk == 0 and kdim // bk >= 2
    const = lambda shape: pl.BlockSpec(shape, lambda i, k: (0, 0))
    return pl.pallas_call(
        _ff2_kernel,
        grid=(s // bm, kdim // bk),
        in_specs=[pl.BlockSpec((bm, bk), lambda i, k: (i, k)),
                  pl.BlockSpec((bk, D_MODEL), lambda i, k: (k, 0)),
                  pl.BlockSpec((bm, D_MODEL), lambda i, k: (i, 0)),
                  const((1, D_MODEL)), const((1, D_MODEL))],
        out_specs=pl.BlockSpec((bm, D_MODEL), lambda i, k: (i, 0)),
        out_shape=jax.ShapeDtypeStruct((s, D_MODEL), F32),
        compiler_params=_cparams(2),
        name="ff2_ln",
    )(h, w, x, gain.reshape(1, -1), bias.reshape(1, -1))


def _mlp_kernel(xb_ref, w1_ref, w2_ref, x_ref, gain_ref, bias_ref, o_ref, w1b_ref, w2b_ref):
    kk = pl.program_id(1)
    w1b_ref[...] = w1_ref[...].astype(BF16)
    w2b_ref[...] = w2_ref[...].astype(BF16)

    def run(sub, finish):
        def hidden(r):
            acc = jnp.dot(xb_ref[r * sub:(r + 1) * sub, :], w1b_ref[...], preferred_element_type=F32)
            return jnp.square(jnp.maximum(acc, 0.0)).astype(BF16)

        nsub = o_ref.shape[0] // sub
        h = hidden(0)
        for r in range(nsub):
            h_next = hidden(r + 1) if r + 1 < nsub else None
            finish(slice(r * sub, (r + 1) * sub), jnp.dot(h, w2b_ref[...], preferred_element_type=F32))
            h = h_next

    @pl.when(kk == 0)
    def _():
        def finish(rows, part):
            o_ref[rows, :] = part
        run(MLP_ROWS, finish)

    @pl.when((kk > 0) & (kk < pl.num_programs(1) - 1))
    def _():
        def finish(rows, part):
            o_ref[rows, :] += part
        run(MLP_ROWS, finish)

    @pl.when(kk == pl.num_programs(1) - 1)
    def _():
        def finish(rows, part):
            o_ref[rows, :] = _layer_norm(DEEPNORM_ALPHA * x_ref[rows, :] + (o_ref[rows, :] + part),
                                         gain_ref[...], bias_ref[...])
        run(LN_ROWS, finish)


MLP_ROWS = 256


def _mlp(xb, w1, w2, x, gain, bias, *, bm, bk):
    s, d = xb.shape
    hidden = w1.shape[1]
    assert s % bm == 0 and hidden % bk == 0 and hidden // bk >= 2
    const = lambda shape: pl.BlockSpec(shape, lambda i, k: (0, 0))
    return pl.pallas_call(
        _mlp_kernel,
        grid=(s // bm, hidden // bk),
        in_specs=[pl.BlockSpec((bm, d), lambda i, k: (i, 0)),
                  pl.BlockSpec((d, bk), lambda i, k: (0, k)),
                  pl.BlockSpec((bk, d), lambda i, k: (k, 0)),
                  pl.BlockSpec((bm, d), lambda i, k: (i, 0)),
                  const((1, d)), const((1, d))],
        out_specs=pl.BlockSpec((bm, d), lambda i, k: (i, 0)),
        out_shape=jax.ShapeDtypeStruct((s, d), F32),
        scratch_shapes=[pltpu.VMEM((d, bk), BF16), pltpu.VMEM((bk, d), BF16)],
        compiler_params=_cparams(2),
        name="mlp_ln",
    )(xb, w1, w2, x, gain.reshape(1, -1), bias.reshape(1, -1))


def _pad_rows(w, rows):
    return jnp.pad(w, ((0, rows - w.shape[0]), (0, 0)))


def _layer(x, positions, w_in, mu_shift, w0, w_up, a0, a_up, g_up, k_k, k_a, r_k, gn_gain, gn_bias,
           lq1, lk1, lq2, lk2, subln_gain, w_out, ln1_gain, ln1_bias, w_ff1, w_ff2, ln2_gain, ln2_bias,
           *, bm, prep_tb, nc, bq):
    w3 = 3 * RWKV_WIDTH
    rwkv_cols = w3 + DECAY_LORA + AAA_LORA + GATE_LORA

    def pad_cols(a, lo, hi, width):
        return jnp.pad(a[..., lo:hi], [(0, 0)] * (a.ndim - 1) + [(0, width - (hi - lo))])

    def rwkv_layout(a):
        return jnp.concatenate([
            a[..., :w3],
            pad_cols(a, w3, w3 + DECAY_LORA, DECAY_PAD),
            pad_cols(a, w3 + DECAY_LORA, w3 + DECAY_LORA + AAA_LORA, AAA_PAD),
            pad_cols(a, w3 + DECAY_LORA + AAA_LORA, rwkv_cols, GATE_PAD)], axis=-1)

    w_all = _relayout_w_in(w_in.T, bk=256)
    proj_r, xb = _proj_cast(x, w_all, bm=bm, bn=RWKV_PCOLS // 2, n=RWKV_PCOLS)
    vt = _matmul(xb, w_all, bm=bm, bn=PROJ_BN, n=DIFF_WIDTH, col0=V_COL0, out_dtype=BF16,
                 transpose_out=True, name="proj_vt")

    at, bt, kt, rt, v_r, wc, bonus, gate = _rwkv_prep(
        proj_r, rwkv_layout(mu_shift), w0,
        _pad_rows(w_up, DECAY_PAD).astype(BF16), a0, _pad_rows(a_up, AAA_PAD).astype(BF16),
        _pad_rows(g_up, GATE_PAD).astype(BF16), k_k, k_a, r_k.reshape(-1), tb=prep_tb)
    y_rwkv = _rwkv_chunks(at, bt, kt, rt, v_r, wc, bonus, gate, gn_gain, gn_bias, nc=nc)

    qk = _proj_rope(xb, w_all, positions, bm=bm, bn=PROJ_BN, n=2 * DIFF_WIDTH, col0=QK_COL0)
    lam_vecs = jnp.stack([lq1, lk1, lq2, lk2]).astype(F32)
    y_diff = _attention(qk, vt, lam_vecs, subln_gain, bq=bq)

    wo = w_out.astype(BF16)
    x1, x1b = _outproj(y_rwkv, y_diff, wo[:RWKV_WIDTH], wo[RWKV_WIDTH:], x, ln1_gain, ln1_bias, bm=512)
    return _mlp(x1b, w_ff1, w_ff2, x1, ln2_gain, ln2_bias, bm=1024, bk=256)


def kernel(x, positions, w_in, mu_shift, rwkv_w0, rwkv_w_up, rwkv_a0, rwkv_a_up, rwkv_g_up, rwkv_k_k,
           rwkv_k_a, rwkv_r_k, rwkv_gn_gain, rwkv_gn_bias, diff_lambda_q1, diff_lambda_k1,
           diff_lambda_q2, diff_lambda_k2, diff_subln_gain, w_out, ln1_gain, ln1_bias, w_ff1, w_ff2,
           ln2_gain, ln2_bias):
    batch, seq, _ = x.shape
    assert batch == 1 and w_in.shape[0] == 1
    out = _layer(
        x[0], positions[0], w_in[0], mu_shift[0], rwkv_w0[0], rwkv_w_up[0], rwkv_a0[0], rwkv_a_up[0],
        rwkv_g_up[0], rwkv_k_k[0], rwkv_k_a[0], rwkv_r_k[0], rwkv_gn_gain[0], rwkv_gn_bias[0],
        diff_lambda_q1[0], diff_lambda_k1[0], diff_lambda_q2[0], diff_lambda_k2[0],
        diff_subln_gain[0], w_out[0], ln1_gain[0], ln1_bias[0], w_ff1[0], w_ff2[0], ln2_gain[0],
        ln2_bias[0], bm=1024, prep_tb=256, nc=4, bq=1024)
    return out[None]
```
